```python
import jax, jax.numpy as jnp
from jax import lax
import numpy as np

D_MODEL = 1024
BATCH = 4
SEQ = 4096
DEPTH = 4
DEC_BATCH = 128
DEC_SEQ = 1
PAST_LEN = 8192
PAGE_SIZE = 128

N_A_LAYERS = DEPTH // 2
N_B_LAYERS = DEPTH - N_A_LAYERS
RET_HEADS = 4
RET_DK = D_MODEL // RET_HEADS
RET_DV = D_MODEL // RET_HEADS
RET_CHUNK = 128
MLA_HEADS = 8
QK_NOPE = D_MODEL // MLA_HEADS
QK_ROPE = 64
V_HEAD = D_MODEL // MLA_HEADS
Q_LORA = D_MODEL // 4
KV_LORA = D_MODEL // 4
MLA_BLOCK_Q = 128
MLA_SCALE = (QK_NOPE + QK_ROPE) ** -0.5
N_EXPERTS = 16
N_GROUPS = 4
EXPERTS_PER_GROUP = N_EXPERTS // N_GROUPS
TOP_K = 2
D_EXPERT = D_MODEL // 2

ROPE_THETA = 10000.0
EPS = 1e-6
NEG = -1e30

kernel_name = "yoco_retention_mla_grouped_moe_step"


def rmsnorm(x, g):
    xf = x.astype(jnp.float32)
    y = xf * lax.rsqrt(jnp.mean(xf * xf, axis=-1, keepdims=True) + EPS)
    return (y * g.astype(jnp.float32)).astype(x.dtype)


def modulate(h, shift, scale):
    return h * (1 + scale[:, None, :]) + shift[:, None, :]


def rope(x, pos):
    d = x.shape[-1]
    inv = ROPE_THETA ** (-jnp.arange(0, d, 2, dtype=jnp.float32) / d)
    ang = pos.astype(jnp.float32)[:, None] * inv[None, :]
    cos = jnp.cos(ang)[None, :, None, :]
    sin = jnp.sin(ang)[None, :, None, :]
    xf = x.astype(jnp.float32)
    x1, x2 = xf[..., : d // 2], xf[..., d // 2:]
    return jnp.concatenate([x1 * cos - x2 * sin, x1 * sin + x2 * cos], axis=-1).astype(x.dtype)


def retention_log_decay():
    return jnp.log1p(-jnp.exp2(-5.0 - jnp.arange(RET_HEADS, dtype=jnp.float32)))


def retention_chunk(q, k, v, S, log_g):
    L = q.shape[1]
    idx = jnp.arange(L, dtype=jnp.float32)
    diff = idx[:, None] - idx[None, :]
    decay = jnp.where(diff[None] >= 0, jnp.exp(jnp.maximum(diff, 0.0)[None] * log_g[:, None, None]), 0.0)
    scores = jnp.einsum('blhd,bmhd->bhlm', q, k) * decay[None]
    inner = jnp.einsum('bhlm,bmhv->blhv', scores, v)
    q_decay = jnp.exp((idx + 1.0)[:, None] * log_g[None, :])
    cross = jnp.einsum('blhd,bhdv->blhv', q * q_decay[None, :, :, None], S)
    k_decay = jnp.exp((L - 1.0 - idx)[:, None] * log_g[None, :])
    S_new = jnp.exp(L * log_g)[None, :, None, None] * S + jnp.einsum('blhd,blhv->bhdv', k * k_decay[None, :, :, None], v)
    return inner + cross, S_new


def retention_prompt(q, k, v, log_g):
    B, S, H, dk = q.shape
    dv = v.shape[-1]
    nc = S // RET_CHUNK
    resh = lambda t: t.reshape(B, nc, RET_CHUNK, H, t.shape[-1]).swapaxes(0, 1)
    S0 = jnp.zeros((B, H, dk, dv), jnp.float32)

    def step(state, qkv):
        o, state = retention_chunk(qkv[0], qkv[1], qkv[2], state, log_g)
        return state, o

    S_fin, o = lax.scan(step, S0, (resh(q), resh(k), resh(v)))
    return o.swapaxes(0, 1).reshape(B, S, H, dv), S_fin


def retention_mixer(h, pos, state, w_in, gn_g, w_out):
    B, S, _ = h.shape
    nqk = RET_HEADS * RET_DK
    nv = RET_HEADS * RET_DV
    proj = h @ w_in
    q = proj[..., :nqk].reshape(B, S, RET_HEADS, RET_DK)
    k = proj[..., nqk:2 * nqk].reshape(B, S, RET_HEADS, RET_DK)
    v = proj[..., 2 * nqk:2 * nqk + nv].reshape(B, S, RET_HEADS, RET_DV).astype(jnp.float32)
    g = proj[..., 2 * nqk + nv:]
    q = rope(q, pos).astype(jnp.float32) * (RET_DK ** -0.5)
    k = rope(k, pos).astype(jnp.float32)
    log_g = retention_log_decay()
    if state is None:
        o, s_new = retention_prompt(q, k, v, log_g)
    else:
        o, s_new = retention_chunk(q, k, v, state.astype(jnp.float32), log_g)
    mu = jnp.mean(o, axis=-1, keepdims=True)
    var = jnp.mean(jnp.square(o - mu), axis=-1, keepdims=True)
    o = ((o - mu) * lax.rsqrt(var + EPS)).reshape(B, S, nv) * gn_g.astype(jnp.float32)
    y = (jax.nn.silu(g.astype(jnp.float32)) * o).astype(h.dtype) @ w_out
    return y, s_new


def mla_kv_stream(x, c_silu, pos, kv_w_ada, kv_b_ada, kv_norm_g, w_dkv, kv_latent_norm_g):
    shift, scale = jnp.split(c_silu @ kv_w_ada + kv_b_ada, 2, axis=-1)
    hn = modulate(rmsnorm(x, kv_norm_g), shift, scale)
    z = hn @ w_dkv
    ckv = rmsnorm(z[..., :KV_LORA], kv_latent_norm_g)
    kpe = rope(z[..., KV_LORA:][:, :, None, :], pos)[:, :, 0, :]
    return ckv, kpe


def mla_query(h, pos, w_dq, q_norm_g, w_uq, w_uk):
    B, S, _ = h.shape
    cq = rmsnorm(h @ w_dq, q_norm_g)
    q = (cq @ w_uq).reshape(B, S, MLA_HEADS, QK_NOPE + QK_ROPE)
    q_nope, q_pe = q[..., :QK_NOPE], q[..., QK_NOPE:]
    q_pe = rope(q_pe, pos)
    q_lat = jnp.einsum('bshn,hnc->bshc', q_nope, w_uk)
    return q_lat, q_pe


def mla_out(o_lat, w_uv, w_o):
    o = jnp.einsum('bqhc,hvc->bqhv', o_lat, w_uv)
    B, S = o.shape[:2]
    return o.reshape(B, S, MLA_HEADS * V_HEAD) @ w_o


def mla_attend_prompt(q_lat, q_pe, ckv, kpe):
    B, S, H, C = q_lat.shape
    nb = S // MLA_BLOCK_Q
    kpos = jnp.arange(S)

    def block(args):
        ql, qp, i = args
        s = (jnp.einsum('bqhc,bkc->bhqk', ql, ckv) + jnp.einsum('bqhr,bkr->bhqk', qp, kpe)).astype(jnp.float32) * MLA_SCALE
        qpos = i * MLA_BLOCK_Q + jnp.arange(MLA_BLOCK_Q)
        s = jnp.where((kpos[None, :] <= qpos[:, None])[None, None], s, NEG)
        p = jax.nn.softmax(s, axis=-1).astype(ckv.dtype)
        return jnp.einsum('bhqk,bkc->bqhc', p, ckv)

    ql_b = q_lat.reshape(B, nb, MLA_BLOCK_Q, H, C).swapaxes(0, 1)
    qp_b = q_pe.reshape(B, nb, MLA_BLOCK_Q, H, QK_ROPE).swapaxes(0, 1)
    o = lax.map(block, (ql_b, qp_b, jnp.arange(nb)))
    return o.swapaxes(0, 1).reshape(B, S, H, C)


def mla_attend_sample(q_lat, q_pe, past_ckv, past_kpe, new_ckv, new_kpe):
    L = new_ckv.shape[1]
    P = past_ckv.shape[1]
    s_past = jnp.einsum('bqhc,bkc->bhqk', q_lat, past_ckv) + jnp.einsum('bqhr,bkr->bhqk', q_pe, past_kpe)
    s_new = jnp.einsum('bqhc,bkc->bhqk', q_lat, new_ckv) + jnp.einsum('bqhr,bkr->bhqk', q_pe, new_kpe)
    causal = jnp.arange(L)[None, :] <= jnp.arange(L)[:, None]
    s_new = jnp.where(causal[None, None], s_new.astype(jnp.float32) * MLA_SCALE, NEG)
    s = jnp.concatenate([s_past.astype(jnp.float32) * MLA_SCALE, s_new], axis=-1)
    p = jax.nn.softmax(s, axis=-1).astype(new_ckv.dtype)
    return (jnp.einsum('bhqk,bkc->bqhc', p[..., :P], past_ckv)
            + jnp.einsum('bhqk,bkc->bqhc', p[..., P:], new_ckv))


def moe_ffn(h, router_w, router_bias, w_gate, w_up, w_down, ws_gate, ws_up, ws_down):
    B, S, D = h.shape
    t = h.reshape(B * S, D)
    scores = jax.nn.sigmoid((t @ router_w).astype(jnp.float32))
    sel = (scores + router_bias.astype(jnp.float32)).reshape(-1, N_GROUPS, EXPERTS_PER_GROUP)
    group_score = jnp.sum(lax.top_k(sel, TOP_K)[0], axis=-1)
    best = jnp.argmax(group_score, axis=-1)
    in_group = best[:, None] == jnp.arange(N_GROUPS)[None, :]
    masked = jnp.where(in_group[..., None], sel, -jnp.inf).reshape(-1, N_EXPERTS)
    _, top_idx = lax.top_k(masked, TOP_K)
    top_s = jnp.take_along_axis(scores, top_idx, axis=-1)
    wts = top_s / jnp.sum(top_s, axis=-1, keepdims=True)
    gates = jnp.sum(jax.nn.one_hot(top_idx, N_EXPERTS, dtype=jnp.float32) * wts[..., None], axis=1)
    hid = jax.nn.silu(jnp.einsum('td,edf->tef', t, w_gate)) * jnp.einsum('td,edf->tef', t, w_up)
    hid = hid * gates.astype(hid.dtype)[..., None]
    routed = jnp.einsum('tef,efd->td', hid, w_down)
    shared = (jax.nn.silu(t @ ws_gate) * (t @ ws_up)) @ ws_down
    return (routed + shared).reshape(B, S, D)


def run_trunk(x, c, pos, ret_state, attend, P):
    c_silu = jax.nn.silu(c)
    new_ret = []
    ckv = kpe = None
    for l in range(DEPTH):
        mod = c_silu @ P['w_ada'][l] + P['b_ada'][l]
        sh1, sc1, g1, sh2, sc2, g2 = jnp.split(mod, 6, axis=-1)
        if l == N_A_LAYERS:
            ckv, kpe = mla_kv_stream(x, c_silu, pos, P['kv_w_ada'], P['kv_b_ada'], P['kv_norm_g'],
                                     P['mla_w_dkv'], P['mla_kv_norm_g'])
        h = modulate(rmsnorm(x, P['norm_mix_g'][l]), sh1, sc1)
        if l < N_A_LAYERS:
            st = None if ret_state is None else ret_state[l]
            y, s_new = retention_mixer(h, pos, st, P['ret_w_in'][l], P['ret_gn_g'][l], P['ret_w_out'][l])
            new_ret.append(s_new)
        else:
            j = l - N_A_LAYERS
            q_lat, q_pe = mla_query(h, pos, P['mla_w_dq'][j], P['mla_q_norm_g'][j], P['mla_w_uq'][j], P['mla_w_uk'])
            y = mla_out(attend(q_lat, q_pe, ckv, kpe), P['mla_w_uv'], P['mla_w_o'][j])
        x = x + g1[:, None, :] * y
        h = modulate(rmsnorm(x, P['norm_ffn_g'][l]), sh2, sc2)
        x = x + g2[:, None, :] * moe_ffn(h, P['router_w'], P['router_bias'], P['moe_w_gate'][l], P['moe_w_up'][l],
                                         P['moe_w_down'][l], P['shared_w_gate'][l], P['shared_w_up'][l], P['shared_w_down'][l])
    return rmsnorm(x, P['final_norm_g']), jnp.stack(new_ret), ckv, kpe


def setup_inputs(seed: int = 0) -> dict:
    key = jax.random.key(seed)
    ks = iter(jax.random.split(key, 48))
    f32 = jnp.float32

    def nrm(shape, scale):
        return jax.random.normal(next(ks), shape, f32) * scale

    def gain(shape):
        return 1.0 + nrm(shape, 0.01)

    n_pages = PAST_LEN // PAGE_SIZE
    n_used = DEC_BATCH * n_pages
    n_phys = n_used + n_used // 4
    D = D_MODEL
    ret_cols = RET_HEADS * (2 * RET_DK + 2 * RET_DV)
    perm = jax.random.permutation(next(ks), n_phys)
    return {
        'x_prompt': nrm((BATCH, SEQ, D), 1.0),
        'x_sample': nrm((DEC_BATCH, DEC_SEQ, D), 1.0),
        'c_prompt': nrm((BATCH, D), 1.0),
        'c_sample': nrm((DEC_BATCH, D), 1.0),
        'state_ret': nrm((N_A_LAYERS, DEC_BATCH, RET_HEADS, RET_DK, RET_DV), 4.0),
        'cache_ckv': nrm((n_phys, PAGE_SIZE, KV_LORA), 1.0),
        'cache_kpe': nrm((n_phys, PAGE_SIZE, QK_ROPE), 1.0),
        'page_table': perm[:n_used].reshape(DEC_BATCH, n_pages).astype(jnp.int32),
        'w_ada': nrm((DEPTH, D, 6 * D), 0.5 * D ** -0.5),
        'b_ada': nrm((DEPTH, 6 * D), 0.02),
        'norm_mix_g': gain((DEPTH, D)),
        'norm_ffn_g': gain((DEPTH, D)),
        'ret_w_in': nrm((N_A_LAYERS, D, ret_cols), D ** -0.5),
        'ret_gn_g': gain((N_A_LAYERS, RET_HEADS * RET_DV)),
        'ret_w_out': nrm((N_A_LAYERS, RET_HEADS * RET_DV, D), (RET_HEADS * RET_DV) ** -0.5),
        'kv_w_ada': nrm((D, 2 * D), 0.5 * D ** -0.5),
        'kv_b_ada': nrm((2 * D,), 0.02),
        'kv_norm_g': gain((D,)),
        'mla_w_dkv': nrm((D, KV_LORA + QK_ROPE), D ** -0.5),
        'mla_kv_norm_g': gain((KV_LORA,)),
        'mla_w_uk': nrm((MLA_HEADS, QK_NOPE, KV_LORA), KV_LORA ** -0.5),
        'mla_w_uv': nrm((MLA_HEADS, V_HEAD, KV_LORA), KV_LORA ** -0.5),
        'mla_w_dq': nrm((N_B_LAYERS, D, Q_LORA), D ** -0.5),
        'mla_q_norm_g': gain((N_B_LAYERS, Q_LORA)),
        'mla_w_uq': nrm((N_B_LAYERS, Q_LORA, MLA_HEADS * (QK_NOPE + QK_ROPE)), Q_LORA ** -0.5),
        'mla_w_o': nrm((N_B_LAYERS, MLA_HEADS * V_HEAD, D), (MLA_HEADS * V_HEAD) ** -0.5),
        'router_w': nrm((D, N_EXPERTS), D ** -0.5),
        'router_bias': nrm((N_EXPERTS,), 0.01),
        'moe_w_gate': nrm((DEPTH, N_EXPERTS, D, D_EXPERT), D ** -0.5),
        'moe_w_up': nrm((DEPTH, N_EXPERTS, D, D_EXPERT), D ** -0.5),
        'moe_w_down': nrm((DEPTH, N_EXPERTS, D_EXPERT, D), D_EXPERT ** -0.5),
        'shared_w_gate': nrm((DEPTH, D, D_EXPERT), D ** -0.5),
        'shared_w_up': nrm((DEPTH, D, D_EXPERT), D ** -0.5),
        'shared_w_down': nrm((DEPTH, D_EXPERT, D), D_EXPERT ** -0.5),
        'final_norm_g': gain((D,)),
    }


def reference(x_prompt, x_sample, c_prompt, c_sample, state_ret, cache_ckv, cache_kpe, page_table,
              w_ada, b_ada, norm_mix_g, norm_ffn_g, ret_w_in, ret_gn_g, ret_w_out,
              kv_w_ada, kv_b_ada, kv_norm_g, mla_w_dkv, mla_kv_norm_g, mla_w_uk, mla_w_uv,
              mla_w_dq, mla_q_norm_g, mla_w_uq, mla_w_o, router_w, router_bias,
              moe_w_gate, moe_w_up, moe_w_down, shared_w_gate, shared_w_up, shared_w_down, final_norm_g):
    P = {
        'w_ada': w_ada, 'b_ada': b_ada, 'norm_mix_g': norm_mix_g, 'norm_ffn_g': norm_ffn_g,
        'ret_w_in': ret_w_in, 'ret_gn_g': ret_gn_g, 'ret_w_out': ret_w_out,
        'kv_w_ada': kv_w_ada, 'kv_b_ada': kv_b_ada, 'kv_norm_g': kv_norm_g,
        'mla_w_dkv': mla_w_dkv, 'mla_kv_norm_g': mla_kv_norm_g, 'mla_w_uk': mla_w_uk, 'mla_w_uv': mla_w_uv,
        'mla_w_dq': mla_w_dq, 'mla_q_norm_g': mla_q_norm_g, 'mla_w_uq': mla_w_uq, 'mla_w_o': mla_w_o,
        'router_w': router_w, 'router_bias': router_bias,
        'moe_w_gate': moe_w_gate, 'moe_w_up': moe_w_up, 'moe_w_down': moe_w_down,
        'shared_w_gate': shared_w_gate, 'shared_w_up': shared_w_up, 'shared_w_down': shared_w_down,
        'final_norm_g': final_norm_g,
    }
    pos_p = jnp.arange(x_prompt.shape[1])
    y_prompt, state_ret_prompt, ckv_prompt, kpe_prompt = run_trunk(x_prompt, c_prompt, pos_p, None, mla_attend_prompt, P)

    n_seq, n_pages = page_table.shape
    past_len = n_pages * cache_ckv.shape[1]
    past_ckv = cache_ckv[page_table].reshape(n_seq, past_len, KV_LORA)
    past_kpe = cache_kpe[page_table].reshape(n_seq, past_len, QK_ROPE)
    pos_s = past_len + jnp.arange(x_sample.shape[1])

    def attend_sample(q_lat, q_pe, new_ckv, new_kpe):
        return mla_attend_sample(q_lat, q_pe, past_ckv, past_kpe, new_ckv, new_kpe)

    y_sample, state_ret_sample, ckv_sample, kpe_sample = run_trunk(x_sample, c_sample, pos_s, state_ret, attend_sample, P)
    return (y_prompt, y_sample, state_ret_prompt, state_ret_sample, ckv_prompt, kpe_prompt, ckv_sample, kpe_sample)
```

```python
import functools

import jax
import jax.numpy as jnp
from jax import lax
from jax.experimental import pallas as pl
from jax.experimental.pallas import tpu as pltpu

F32 = jnp.float32
BF16 = jnp.bfloat16

RET_HEADS = 4
MLA_HEADS = 8
QK_ROPE = 64
N_GROUPS = 4
EXPERTS_PER_GROUP = 4
ROPE_THETA = 10000.0
EPS = 1e-6

_PAIRS = ((0, 1), (0, 2), (0, 3), (1, 2), (1, 3), (2, 3))
N_CLASSES = N_GROUPS * len(_PAIRS)

LANE = 128
VMEM_LIMIT_BYTES = 56 * 2 ** 20
TM_PROMPT = 256
MOE_TILE = 256
RET_CHUNK = 256
ATT_TQ = 256
ATT_TK = 256
DEC_NB = 8
DEC_CHUNK_PAGES = 8


def _dot(a, b):
    return jnp.dot(a, b, preferred_element_type=F32)


def _dot_nt(a, b):
    return lax.dot_general(a, b, (((1,), (1,)), ((), ())), preferred_element_type=F32)


def _dot_tn(a, b):
    return lax.dot_general(a, b, (((0,), (0,)), ((), ())), preferred_element_type=F32)


def _silu(x):
    return x * jax.nn.sigmoid(x)


def _rms(x, g):
    return x * lax.rsqrt(jnp.mean(x * x, axis=-1, keepdims=True) + EPS) * g


def _cparams(n_axes=1):
    return pltpu.CompilerParams(dimension_semantics=("arbitrary",) * n_axes,
                                vmem_limit_bytes=VMEM_LIMIT_BYTES)


class _Group:
    def __init__(self, n_seq, seq_len, tm, per_token_mod):
        self.n_seq, self.seq_len, self.tm = n_seq, seq_len, tm
        self.T = n_seq * seq_len
        self.n_tiles = self.T // tm
        self.per_token_mod = per_token_mod
        self.tiles_per_seq = max(seq_len // tm, 1)

    def tok_spec(self, width, offset=0):
        return pl.BlockSpec((self.tm, width), lambda i: (i + offset, 0))

    def mod_spec(self, width):
        if self.per_token_mod:
            return pl.BlockSpec((self.tm, width), lambda i: (i, 0))
        tps = self.tiles_per_seq
        return pl.BlockSpec((None, 1, width), lambda i: (i // tps, 0, 0))

    def rope_spec(self, width):
        if self.per_token_mod:
            return pl.BlockSpec((1, width), lambda i: (0, 0))
        tps = self.tiles_per_seq
        return pl.BlockSpec((self.tm, width), lambda i: (i % tps, 0))

    def mod_array(self, m):
        return m if self.per_token_mod else m[:, None, :]


def _const_spec(a):
    nd = a.ndim
    return pl.BlockSpec(a.shape, lambda i: (0,) * nd)


def _tok_call(name, body, grp, tok_ins, mod_ins, rope_ins, const_ins, outs):
    in_specs, args = [], []
    for a, off in tok_ins:
        in_specs.append(grp.tok_spec(a.shape[-1], off)); args.append(a)
    for a in mod_ins:
        in_specs.append(grp.mod_spec(a.shape[-1])); args.append(a)
    for a in rope_ins:
        in_specs.append(grp.rope_spec(a.shape[-1])); args.append(a)
    for a in const_ins:
        in_specs.append(_const_spec(a)); args.append(a)
    out_specs = [grp.tok_spec(width) for width, _ in outs]
    out_shape = [jax.ShapeDtypeStruct((grp.T, width), dtype) for width, dtype in outs]
    return pl.pallas_call(
        body, grid=(grp.n_tiles,), in_specs=in_specs, out_specs=out_specs, out_shape=out_shape,
        name=name, compiler_params=_cparams(1),
    )(*args)


def _mod_kernel(c_ref, w_ref, b_ref, o_ref):
    c = c_ref[...]
    o_ref[...] = _dot(_silu(c).astype(BF16), w_ref[...].astype(BF16)) + b_ref[...]


def _ada_mod(c_all, w, b):
    L, D, N = w.shape
    Bc = c_all.shape[0]
    tn = min(N, 2048)
    return pl.pallas_call(
        _mod_kernel, grid=(L, N // tn),
        in_specs=[pl.BlockSpec((Bc, D), lambda l, j: (0, 0)),
                  pl.BlockSpec((None, D, tn), lambda l, j: (l, 0, j)),
                  pl.BlockSpec((None, 1, tn), lambda l, j: (l, 0, j))],
        out_specs=pl.BlockSpec((None, Bc, tn), lambda l, j: (l, 0, j)),
        out_shape=jax.ShapeDtypeStruct((L, Bc, N), F32),
        name="ada_mod", compiler_params=_cparams(2),
    )(c_all, w, b[:, None, :])


def _resid(xs_ref, y_ref, pmod_ref, D):
    x = xs_ref[...]
    if y_ref is not None:
        x = x + pmod_ref[:, 5 * D:6 * D] * y_ref[...]
    return x


def _ret_in_kernel(*refs, has_y, D, H):
    refs = list(refs)
    xs_ref = refs.pop(0)
    y_ref = refs.pop(0) if has_y else None
    pmod_ref = refs.pop(0) if has_y else None
    mod_ref, cos_ref, sin_ref, g_ref, w_ref = refs[:5]
    outs = refs[5:]
    if has_y:
        xo_ref, outs = outs[0], outs[1:]
    q_ref, k_ref, v_ref, gt_ref = outs
    x = _resid(xs_ref, y_ref, pmod_ref, D)
    if has_y:
        xo_ref[...] = x
    h = _rms(x, g_ref[...]) * (1.0 + mod_ref[:, D:2 * D]) + mod_ref[:, 0:D]
    hb = h.astype(BF16)
    dk = D // H
    half = dk // 2
    cos, sin = cos_ref[...], sin_ref[...]
    for idx, (o_ref, scale) in enumerate(((q_ref, float(dk) ** -0.5), (k_ref, 1.0))):
        t = _dot(hb, w_ref[:, idx * D:(idx + 1) * D])
        for hh in range(H):
            x1 = t[:, hh * dk:hh * dk + half]
            x2 = t[:, hh * dk + half:(hh + 1) * dk]
            o_ref[:, hh * dk:hh * dk + half] = ((x1 * cos - x2 * sin) * scale).astype(o_ref.dtype)
            o_ref[:, hh * dk + half:(hh + 1) * dk] = ((x1 * sin + x2 * cos) * scale).astype(o_ref.dtype)
    v_ref[...] = _dot(hb, w_ref[:, 2 * D:3 * D]).astype(v_ref.dtype)
    gt_ref[...] = _dot(hb, w_ref[:, 3 * D:4 * D])


def _ret_in(grp, xs, y, y_off, pmod, mod, cos, sin, norm_g, w_in, qkv_dtype):
    D = xs.shape[-1]
    has_y = y is not None
    tok = [(xs, 0)] + ([(y, y_off)] if has_y else [])
    mods = ([pmod] if has_y else []) + [mod]
    outs = ([(D, F32)] if has_y else []) + [(D, qkv_dtype)] * 3 + [(D, F32)]
    res = _tok_call("ret_in", functools.partial(_ret_in_kernel, has_y=has_y, D=D, H=RET_HEADS), grp,
                    tok, mods, [cos, sin], [norm_g, w_in], outs)
    if not has_y:
        res = [xs] + list(res)
    return res


def _gn_gate(o, g, gn):
    mu = jnp.mean(o, axis=-1, keepdims=True)
    c = o - mu
    var = jnp.mean(c * c, axis=-1, keepdims=True)
    return _silu(g) * (c * lax.rsqrt(var + EPS) * gn)


def _ret_chunk_kernel(x_ref, q_ref, k_ref, v_ref, gt_ref, mod_ref, dec_ref, qd_ref, kd_ref, gl_ref,
                      gn_ref, wo_ref, xo_ref, s_ref, gat_ref, *, D, H):
    @pl.when(pl.program_id(1) == 0)
    def _():
        s_ref[...] = jnp.zeros_like(s_ref)

    dk = D // H
    for h in range(H):
        hs = slice(h * dk, (h + 1) * dk)
        qh, kh, vh = q_ref[:, hs], k_ref[:, hs], v_ref[:, hs]
        S = s_ref[h]
        scores = _dot_nt(qh, kh) * dec_ref[h]
        inner = _dot(scores.astype(BF16), vh)
        cross = _dot((qh.astype(F32) * qd_ref[:, h:h + 1]).astype(BF16), S.astype(BF16))
        kdec = (kh.astype(F32) * kd_ref[:, h:h + 1]).astype(BF16)
        s_ref[h] = gl_ref[:, h:h + 1] * S + _dot_tn(kdec, vh)
        gat_ref[:, hs] = _gn_gate(inner + cross, gt_ref[:, hs], gn_ref[:, hs]).astype(BF16)
    y = _dot(gat_ref[...], wo_ref[...])
    xo_ref[...] = x_ref[...] + mod_ref[:, 2 * D:3 * D] * y


def _ret_chunk(B, S, x, q, k, v, gt, mod, tabs, gn_g, w_out):
    D = x.shape[-1]
    H = RET_HEADS
    L = RET_CHUNK
    nc = S // L
    dk = D // H
    dec, qd, kd, gl = tabs
    tok = lambda w: pl.BlockSpec((L, w), lambda b, c: (b * nc + c, 0))
    cst = lambda a: pl.BlockSpec(a.shape, lambda b, c: (0,) * a.ndim)
    return pl.pallas_call(
        functools.partial(_ret_chunk_kernel, D=D, H=H), grid=(B, nc),
        in_specs=[tok(D)] * 5 + [pl.BlockSpec((None, 1, mod.shape[-1]), lambda b, c: (b, 0, 0)),
                                 cst(dec), cst(qd), cst(kd), cst(gl), cst(gn_g), cst(w_out)],
        out_specs=[tok(D), pl.BlockSpec((None, H, dk, dk), lambda b, c: (b, 0, 0, 0))],
        out_shape=[jax.ShapeDtypeStruct(x.shape, F32), jax.ShapeDtypeStruct((B, H, dk, dk), F32)],
        scratch_shapes=[pltpu.VMEM((L, D), BF16)],
        name="ret_chunk", compiler_params=_cparams(2),
    )(x, q, k, v, gt, mod, dec, qd, kd, gl, gn_g, w_out)


def _ret_dec_kernel(st_ref, q_ref, k_ref, v_ref, gl_ref, so_ref, o_ref, *, D, H, nb):
    dk = D // H
    for h in range(H):
        hs = slice(h * dk, (h + 1) * dk)
        gh = gl_ref[:, h:h + 1]
        qh, kh, vh = q_ref[:, hs], k_ref[:, hs], v_ref[:, hs]
        inner = jnp.sum(qh * kh, axis=-1, keepdims=True) * vh
        qT = (qh * gh).T
        kT = kh.T
        rows = []
        for j in range(nb):
            S = st_ref[j, h]
            so_ref[j, h] = gh * S + kT[:, j:j + 1] * vh[j:j + 1, :]
            rows.append(jnp.sum(qT[:, j:j + 1] * S, axis=0, keepdims=True))
        o_ref[:, hs] = inner + jnp.concatenate(rows, axis=0)


def _ret_dec(state, layer, q, k, v, gl):
    _, B, H, dk, dv = state.shape
    D = q.shape[-1]
    nb = DEC_NB
    row = pl.BlockSpec((nb, D), lambda i: (i, 0))
    return pl.pallas_call(
        functools.partial(_ret_dec_kernel, D=D, H=H, nb=nb), grid=(B // nb,),
        in_specs=[pl.BlockSpec((None, nb, H, dk, dv), lambda i: (layer, i, 0, 0, 0)), row, row, row,
                  pl.BlockSpec(gl.shape, lambda i: (0, 0))],
        out_specs=[pl.BlockSpec((nb, H, dk, dv), lambda i: (i, 0, 0, 0)), row],
        out_shape=[jax.ShapeDtypeStruct((B, H, dk, dv), F32), jax.ShapeDtypeStruct((B, D), F32)],
        name="ret_dec", compiler_params=_cparams(1),
    )(state, q, k, v, gl)


def _ret_post_kernel(x_ref, o_ref, gt_ref, mod_ref, gn_ref, wo_ref, xo_ref, *, D, H):
    dk = D // H
    parts = []
    for h in range(H):
        hs = slice(h * dk, (h + 1) * dk)
        parts.append(_gn_gate(o_ref[:, hs], gt_ref[:, hs], gn_ref[:, hs]).astype(BF16))
    y = _dot(jnp.concatenate(parts, axis=-1), wo_ref[...])
    xo_ref[...] = x_ref[...] + mod_ref[:, 2 * D:3 * D] * y


def _route(logits_t, bias):
    scores = jax.nn.sigmoid(logits_t)
    sel = scores + bias
    n_e = EXPERTS_PER_GROUP
    row = lambda a, e: a[e:e + 1, :]
    gscore = []
    for g in range(N_GROUPS):
        best2 = None
        for a, b in _PAIRS:
            s = row(sel, n_e * g + a) + row(sel, n_e * g + b)
            best2 = s if best2 is None else jnp.maximum(best2, s)
        gscore.append(best2)
    best, bestv = jnp.zeros_like(gscore[0], dtype=jnp.int32), gscore[0]
    for g in range(1, N_GROUPS):
        upd = gscore[g] > bestv
        best = jnp.where(upd, g, best)
        bestv = jnp.where(upd, gscore[g], bestv)

    def pick(a, j):
        out = row(a, j)
        for g in range(1, N_GROUPS):
            out = jnp.where(best == g, row(a, n_e * g + j), out)
        return out

    v = [pick(sel, j) for j in range(n_e)]
    sc = [pick(scores, j) for j in range(n_e)]
    i1, v1 = jnp.zeros_like(best), v[0]
    for j in range(1, n_e):
        upd = v[j] > v1
        i1 = jnp.where(upd, j, i1)
        v1 = jnp.where(upd, v[j], v1)
    i2, v2 = None, None
    for j in range(n_e):
        ok = i1 != j
        if i2 is None:
            i2 = jnp.where(ok, j, n_e)
            v2 = jnp.where(ok, v[j], -jnp.inf)
        else:
            upd = ok & (v[j] > v2)
            i2 = jnp.where(upd, j, i2)
            v2 = jnp.where(upd, v[j], v2)
    s1 = sc[0]
    s2 = sc[0]
    for j in range(1, n_e):
        s1 = jnp.where(i1 == j, sc[j], s1)
        s2 = jnp.where(i2 == j, sc[j], s2)
    tot = s1 + s2
    w1, w2 = s1 / tot, s2 / tot
    lo = jnp.minimum(i1, i2)
    hi = jnp.maximum(i1, i2)
    base = jnp.where(lo == 0, 0, jnp.where(lo == 1, 3, 5))
    cls = best * len(_PAIRS) + base + hi - lo - 1
    first_is_lo = i1 < i2
    return cls, jnp.where(first_is_lo, w1, w2), jnp.where(first_is_lo, w2, w1)


def _moe_pre_kernel(x_ref, mod_ref, g_ref, rwt_ref, rb_ref, wsg_ref, wsu_ref, wsd_ref, *rest, D):
    hx_ref, xs_ref, cls_ref = rest[-3:]
    x = x_ref[...]
    h = _rms(x, g_ref[...]) * (1.0 + mod_ref[:, 4 * D:5 * D]) + mod_ref[:, 3 * D:4 * D]
    hb = h.astype(BF16)
    h_lo = (h - hb.astype(F32)).astype(BF16)
    rw = rwt_ref[...]
    rw_hi = rw.astype(BF16)
    rw_lo = (rw - rw_hi.astype(F32)).astype(BF16)
    logits_t = _dot_nt(rw_hi, hb) + (_dot_nt(rw_lo, hb) + _dot_nt(rw_hi, h_lo))
    cls, wa, wb = _route(logits_t, rb_ref[...])
    tm = x.shape[0]
    cls_ref[0] = cls
    rowi = lax.broadcasted_iota(jnp.int32, (LANE, tm), 0)
    extra = jnp.where(rowi == 0, wa, jnp.where(rowi == 1, wb, 0.0)).T
    hx_ref[:, 0:D] = h
    hx_ref[:, D:D + LANE] = extra
    hid = _silu(_dot(hb, wsg_ref[...])) * _dot(hb, wsu_ref[...])
    shared = _dot(hid.astype(BF16), wsd_ref[...])
    xs_ref[...] = x + mod_ref[:, 5 * D:6 * D] * shared


def _moe_pre(grp, x, mod, norm_g, rw_t, rb, wsg, wsu, wsd):
    D = x.shape[-1]
    in_specs = [grp.tok_spec(D), grp.mod_spec(mod.shape[-1])] + [_const_spec(a) for a in (norm_g, rw_t, rb, wsg, wsu, wsd)]
    hx, xs, cls = pl.pallas_call(
        functools.partial(_moe_pre_kernel, D=D), grid=(grp.n_tiles,), in_specs=in_specs,
        out_specs=[grp.tok_spec(D + LANE), grp.tok_spec(D), pl.BlockSpec((1, 1, grp.tm), lambda i: (i, 0, 0))],
        out_shape=[jax.ShapeDtypeStruct((grp.T, D + LANE), F32), jax.ShapeDtypeStruct((grp.T, D), F32),
                   jax.ShapeDtypeStruct((grp.n_tiles, 1, grp.tm), jnp.int32)],
        name="moe_pre", compiler_params=_cparams(1),
    )(x, mod, norm_g, rw_t, rb, wsg, wsu, wsd)
    return hx, xs, cls.reshape(-1)


def _moe_kernel(pos_ref, te1_ref, te2_ref, nv_ref, hxp_hbm, hxs_hbm, wg1, wu1, wd1, wg2, wu2, wd2, y_hbm,
                src_ref, hbuf, obuf, gsem, ssem, *, T_p, T_all, tm, D):
    i = pl.program_id(0)
    n = pl.num_programs(0)
    slot = i % 2

    def start_gather(tile, sl):
        base = tile * tm
        nv = nv_ref[tile]

        @pl.when(nv < tm)
        def _():
            hbuf[sl] = jnp.zeros(hbuf.shape[1:], F32)

        def body(r, c):
            t = src_ref[base + r]

            @pl.when(t < T_p)
            def _():
                pltpu.make_async_copy(hxp_hbm.at[pl.ds(t, 1)], hbuf.at[sl, pl.ds(r, 1)], gsem.at[sl]).start()

            @pl.when(t >= T_p)
            def _():
                pltpu.make_async_copy(hxs_hbm.at[pl.ds(t - T_p, 1)], hbuf.at[sl, pl.ds(r, 1)], gsem.at[sl]).start()
            return c
        lax.fori_loop(0, nv, body, 0)

    def wait_rows(src, dst, sem, nv):
        n8 = pl.multiple_of((nv // 8) * 8, 8)

        @pl.when(n8 > 0)
        def _():
            pltpu.make_async_copy(src.at[pl.ds(0, n8)], dst.at[pl.ds(0, n8)], sem).wait()

        def body(r, c):
            pltpu.make_async_copy(src.at[pl.ds(0, 1)], dst.at[pl.ds(0, 1)], sem).wait()
            return c
        lax.fori_loop(0, nv - n8, body, 0)

    def wait_gather(tile, sl):
        wait_rows(hxp_hbm, hbuf.at[sl], gsem.at[sl], nv_ref[tile])

    def start_scatter(tile, sl):
        base = tile * tm

        def body(r, c):
            t = src_ref[base + r]
            pltpu.make_async_copy(obuf.at[sl, pl.ds(r, 1)], y_hbm.at[pl.ds(t, 1)], ssem.at[sl]).start()
            return c
        lax.fori_loop(0, nv_ref[tile], body, 0)

    def wait_scatter(tile, sl):
        wait_rows(obuf.at[sl], y_hbm, ssem.at[sl], nv_ref[tile])

    @pl.when(i == 0)
    def _():
        def fill(t, c):
            src_ref[pos_ref[t]] = t
            return c
        lax.fori_loop(0, T_all, fill, 0, unroll=8)
        start_gather(0, 0)

    wait_gather(i, slot)

    @pl.when(i + 1 < n)
    def _():
        start_gather(i + 1, 1 - slot)

    @pl.when(i >= 2)
    def _():
        wait_scatter(i - 2, slot)

    @pl.when(nv_ref[i] > 0)
    def _():
        hx = hbuf[slot]
        hb = hx[:, 0:D].astype(BF16)
        wa = hx[:, D:D + 1]
        wb = hx[:, D + 1:D + 2]
        hid_a = (_silu(_dot(hb, wg1[...])) * _dot(hb, wu1[...]) * wa).astype(BF16)
        hid_b = (_silu(_dot(hb, wg2[...])) * _dot(hb, wu2[...]) * wb).astype(BF16)
        obuf[slot] = _dot(hid_a, wd1[...]) + _dot(hid_b, wd2[...])

    start_scatter(i, slot)

    @pl.when(i == n - 1)
    def _():
        wait_scatter(i, slot)

        @pl.when(i >= 1)
        def _():
            wait_scatter(i - 1, 1 - slot)


def _moe_routed(hx_p, hx_s, cls, wg, wu, wd):
    T_p, W = hx_p.shape
    T_all = T_p + hx_s.shape[0]
    D = W - LANE
    E, _, Fe = wg.shape
    tm = MOE_TILE
    n_tiles = (T_all + N_CLASSES * (tm - 1)) // tm
    n_rows = n_tiles * tm
    onehot = (cls[:, None] == jnp.arange(N_CLASSES, dtype=jnp.int32)[None, :]).astype(jnp.int32)
    counts = jnp.sum(onehot, axis=0)
    rank = jnp.sum((jnp.cumsum(onehot, axis=0) - onehot) * onehot, axis=1)
    ntile_c = (counts + tm - 1) // tm
    tile_end = jnp.cumsum(ntile_c)
    tile_off = tile_end - ntile_c
    pos = (jnp.sum(onehot * tile_off[None, :], axis=1) * tm + rank).astype(jnp.int32)
    nused = tile_end[-1]
    tiles = jnp.arange(n_tiles, dtype=jnp.int32)
    tid = jnp.minimum(tiles, nused - 1)
    tile_cls = jnp.sum((tid[:, None] >= tile_end[None, :]).astype(jnp.int32), axis=1)
    nvalid = jnp.clip(counts[tile_cls] - (tiles - tile_off[tile_cls]) * tm, 0, tm)
    nvalid = jnp.where(tiles < nused, nvalid, 0).astype(jnp.int32)
    pair_lo = jnp.array([a for a, _ in _PAIRS], jnp.int32)
    pair_hi = jnp.array([b for _, b in _PAIRS], jnp.int32)
    grp_id, pair_id = tile_cls // len(_PAIRS), tile_cls % len(_PAIRS)
    te1 = (grp_id * EXPERTS_PER_GROUP + pair_lo[pair_id]).astype(jnp.int32)
    te2 = (grp_id * EXPERTS_PER_GROUP + pair_hi[pair_id]).astype(jnp.int32)

    w1 = lambda shape: pl.BlockSpec((None,) + shape, lambda i, pos, te1, te2, nu: (te1[i], 0, 0))
    w2 = lambda shape: pl.BlockSpec((None,) + shape, lambda i, pos, te1, te2, nu: (te2[i], 0, 0))
    any_spec = pl.BlockSpec(memory_space=pl.ANY)
    grid_spec = pltpu.PrefetchScalarGridSpec(
        num_scalar_prefetch=4, grid=(n_tiles,),
        in_specs=[any_spec, any_spec, w1((D, Fe)), w1((D, Fe)), w1((Fe, D)), w2((D, Fe)), w2((D, Fe)), w2((Fe, D))],
        out_specs=any_spec,
        scratch_shapes=[pltpu.SMEM((n_rows,), jnp.int32),
                        pltpu.VMEM((2, tm, W), F32), pltpu.VMEM((2, tm, D), F32),
                        pltpu.SemaphoreType.DMA((2,)), pltpu.SemaphoreType.DMA((2,))])
    return pl.pallas_call(
        functools.partial(_moe_kernel, T_p=T_p, T_all=T_all, tm=tm, D=D),
        grid_spec=grid_spec, out_shape=jax.ShapeDtypeStruct((T_all, D), F32),
        name="moe_routed", compiler_params=_cparams(1),
    )(pos, te1, te2, nvalid, hx_p, hx_s, wg, wu, wd, wg, wu, wd)


def _kv_kernel(xs_ref, y_ref, pmod_ref, kvmod_ref, cos_ref, sin_ref, g_ref, wc_ref, wp_ref, wr_ref, lg_ref,
               xo_ref, ckv_ref, kpe_ref, kcat_ref, *, D, C, R):
    x = _resid(xs_ref, y_ref, pmod_ref, D)
    xo_ref[...] = x
    hn = (_rms(x, g_ref[...]) * (1.0 + kvmod_ref[:, D:2 * D]) + kvmod_ref[:, 0:D]).astype(BF16)
    ckv = _rms(_dot(hn, wc_ref[...]), lg_ref[...])
    kpe = _dot(hn, wp_ref[...]) * cos_ref[...] + _dot(hn, wr_ref[...]) * sin_ref[...]
    ckv_ref[...] = ckv
    kpe_ref[...] = kpe[:, 0:R]
    kcat_ref[:, 0:C] = ckv.astype(kcat_ref.dtype)
    kcat_ref[:, C:C + LANE] = kpe.astype(kcat_ref.dtype)


def _mla_q_kernel(*refs, has_y, D, C, H, scale):
    refs = list(refs)
    xs_ref = refs.pop(0)
    y_ref = refs.pop(0) if has_y else None
    pmod_ref = refs.pop(0) if has_y else None
    mod_ref, cos_ref, sin_ref, g_ref, wdq_ref, qg_ref, wn_ref, wp_ref, wr_ref, wuk_ref = refs[:10]
    outs = refs[10:]
    if has_y:
        xo_ref, outs = outs[0], outs[1:]
    qcat_ref, = outs
    x = _resid(xs_ref, y_ref, pmod_ref, D)
    if has_y:
        xo_ref[...] = x
    h = (_rms(x, g_ref[...]) * (1.0 + mod_ref[:, D:2 * D]) + mod_ref[:, 0:D]).astype(BF16)
    cq = _rms(_dot(h, wdq_ref[...]), qg_ref[...]).astype(BF16)
    q_nope = _dot(cq, wn_ref[...]).astype(BF16)
    cos, sin = cos_ref[...], sin_ref[...]
    dn = q_nope.shape[-1] // H
    for hh in range(H):
        ls = slice(hh * LANE, (hh + 1) * LANE)
        q_lat = _dot(q_nope[:, hh * dn:(hh + 1) * dn], wuk_ref[hh])
        q_pe = _dot(cq, wp_ref[:, ls]) * cos + _dot(cq, wr_ref[:, ls]) * sin
        qcat_ref[hh, :, 0:C] = (q_lat * scale).astype(qcat_ref.dtype)
        qcat_ref[hh, :, C:C + LANE] = (q_pe * scale).astype(qcat_ref.dtype)


def _mla_q(grp, xs, y, y_off, pmod, mod, cos, sin, norm_g, wq, out_dtype):
    D = xs.shape[-1]
    w_dq, qg, w_n, w_p, w_r, w_uk = wq
    C = w_uk.shape[-1]
    H = MLA_HEADS
    has_y = y is not None
    scale = float(w_uk.shape[1] + QK_ROPE) ** -0.5
    in_specs = [grp.tok_spec(D)] + ([grp.tok_spec(D, y_off), grp.mod_spec(pmod.shape[-1])] if has_y else [])
    args = [xs] + ([y, pmod] if has_y else [])
    in_specs += [grp.mod_spec(mod.shape[-1]), grp.rope_spec(LANE), grp.rope_spec(LANE)]
    args += [mod, cos, sin]
    for a in (norm_g, w_dq, qg, w_n, w_p, w_r, w_uk):
        in_specs.append(_const_spec(a)); args.append(a)
    out_specs = ([grp.tok_spec(D)] if has_y else []) + [pl.BlockSpec((H, grp.tm, C + LANE), lambda i: (0, i, 0))]
    out_shape = ([jax.ShapeDtypeStruct((grp.T, D), F32)] if has_y else []) + \
        [jax.ShapeDtypeStruct((H, grp.T, C + LANE), out_dtype)]
    res = pl.pallas_call(
        functools.partial(_mla_q_kernel, has_y=has_y, D=D, C=C, H=H, scale=scale), grid=(grp.n_tiles,),
        in_specs=in_specs, out_specs=out_specs, out_shape=out_shape, name="mla_q", compiler_params=_cparams(1),
    )(*args)
    return (res[0], res[1]) if has_y else (xs, res[0])


def _mla_out(o_lat_heads, x, g1, wuvt_ref, wo_ref):
    parts = [_dot(o.astype(BF16), wuvt_ref[hh]).astype(BF16) for hh, o in enumerate(o_lat_heads)]
    return x + g1 * _dot(jnp.concatenate(parts, axis=-1), wo_ref[...])


def _attn_kernel(x_ref, q_ref, k_ref, mod_ref, wuvt_ref, wo_ref, xo_ref, m_ref, l_ref, acc_ref,
                 *, D, C, H, tq, tk):
    qi = pl.program_id(1)
    rows = H * tq
    q = q_ref[...].reshape(rows, q_ref.shape[-1])
    m_ref[...] = jnp.full(m_ref.shape, -jnp.inf, F32)
    l_ref[...] = jnp.zeros(l_ref.shape, F32)
    acc_ref[...] = jnp.zeros(acc_ref.shape, F32)

    def block(j, masked):
        kb = k_ref[pl.ds(pl.multiple_of(j * tk, tk), tk), :]
        s = _dot_nt(q, kb)
        if masked:
            qpos = lax.broadcasted_iota(jnp.int32, (H, tq, tk), 1).reshape(rows, tk)
            kpos = lax.broadcasted_iota(jnp.int32, (rows, tk), 1)
            s = jnp.where(kpos <= qpos, s, -1e30)
        m_old = m_ref[...]
        m_new = jnp.maximum(m_old, jnp.max(s, axis=-1, keepdims=True))
        alpha = jnp.exp(m_old - m_new)
        p = jnp.exp(s - m_new)
        l_ref[...] = alpha * l_ref[...] + jnp.sum(p, axis=-1, keepdims=True)
        acc_ref[...] = alpha * acc_ref[...] + _dot(p.astype(BF16), kb[:, 0:C])
        m_ref[...] = m_new

    def body(j, c):
        block(j, False)
        return c
    lax.fori_loop(0, qi, body, 0)
    block(qi, True)
    o = acc_ref[...] / l_ref[...]
    heads = [o[hh * tq:(hh + 1) * tq] for hh in range(H)]
    xo_ref[...] = _mla_out(heads, x_ref[...], mod_ref[:, 2 * D:3 * D], wuvt_ref, wo_ref)


def _attn_prompt(B, S, x, qcat, kcat, mod, w_uvt, w_o):
    D = x.shape[-1]
    H, _, Wq = qcat.shape
    C = w_uvt.shape[1]
    tq, tk = ATT_TQ, ATT_TK
    assert tq == tk
    nq = S // tq
    cst = lambda a: pl.BlockSpec(a.shape, lambda b, i: (0,) * a.ndim)
    return pl.pallas_call(
        functools.partial(_attn_kernel, D=D, C=C, H=H, tq=tq, tk=tk), grid=(B, nq),
        in_specs=[pl.BlockSpec((tq, D), lambda b, i: (b * nq + i, 0)),
                  pl.BlockSpec((H, tq, Wq), lambda b, i: (0, b * nq + i, 0)),
                  pl.BlockSpec((S, Wq), lambda b, i: (b, 0)),
                  pl.BlockSpec((None, 1, mod.shape[-1]), lambda b, i: (b, 0, 0)), cst(w_uvt), cst(w_o)],
        out_specs=pl.BlockSpec((tq, D), lambda b, i: (b * nq + i, 0)),
        out_shape=jax.ShapeDtypeStruct(x.shape, F32),
        scratch_shapes=[pltpu.VMEM((H * tq, 1), F32), pltpu.VMEM((H * tq, 1), F32), pltpu.VMEM((H * tq, C), F32)],
        name="attn_prompt", compiler_params=_cparams(2),
    )(x, qcat, kcat, mod, w_uvt, w_o)


def _attn_dec_kernel(pt_ref, q_ref, kn_ref, ckv_hbm, kpe_hbm, o_ref, cbuf, pbuf, kb_ref, s_ref, csem, psem,
                     *, C, R, n_pages, page, cp):
    b = pl.program_id(0)
    nb = pl.num_programs(0)
    slot = b % 2

    def start(seq, sl):
        def body(p, c):
            pg = pt_ref[seq * n_pages + p]
            pltpu.make_async_copy(ckv_hbm.at[pl.ds(pg, 1)], cbuf.at[sl, pl.ds(p, 1)], csem.at[sl]).start()
            pltpu.make_async_copy(kpe_hbm.at[pl.ds(pg, 1)], pbuf.at[sl, pl.ds(p, 1)], psem.at[sl]).start()
            return c
        lax.fori_loop(0, n_pages, body, 0, unroll=4)

    @pl.when(b == 0)
    def _():
        start(0, 0)

    @pl.when(b + 1 < nb)
    def _():
        start(b + 1, 1 - slot)

    pltpu.make_async_copy(ckv_hbm.at[pl.ds(0, n_pages)], cbuf.at[slot], csem.at[slot]).wait()
    pltpu.make_async_copy(kpe_hbm.at[pl.ds(0, n_pages)], pbuf.at[slot], psem.at[slot]).wait()

    q = q_ref[...]
    ql = q[:, 0:C].astype(BF16)
    qp = q[:, C:C + R].astype(BF16)
    ck = cp * page
    n_chunks = n_pages // cp
    for c in range(n_chunks):
        kc = cbuf[slot, c * cp:(c + 1) * cp].reshape(ck, C).astype(BF16)
        pc = pbuf[slot, c * cp:(c + 1) * cp].reshape(ck, R).astype(BF16)
        kb_ref[c * ck:(c + 1) * ck, :] = kc
        s_ref[:, c * ck:(c + 1) * ck] = _dot_nt(ql, kc) + _dot_nt(qp, pc)
    kn = kn_ref[...]
    s_new = jnp.sum(q * kn, axis=-1, keepdims=True)
    s = s_ref[...]
    m = jnp.maximum(jnp.max(s, axis=-1, keepdims=True), s_new)
    p = jnp.exp(s - m)
    p_new = jnp.exp(s_new - m)
    l = jnp.sum(p, axis=-1, keepdims=True) + p_new
    acc = p_new * kn[:, 0:C]
    pb = p.astype(BF16)
    for c in range(n_chunks):
        acc = acc + _dot(pb[:, c * ck:(c + 1) * ck], kb_ref[c * ck:(c + 1) * ck, :])
    o_ref[...] = acc / l


def _attn_dec(q_s, kn, cache_ckv, cache_kpe, page_table):
    B, H, Wq = q_s.shape
    _, page, C = cache_ckv.shape
    R = cache_kpe.shape[-1]
    n_pages = page_table.shape[1]
    P = n_pages * page
    any_spec = pl.BlockSpec(memory_space=pl.ANY)
    grid_spec = pltpu.PrefetchScalarGridSpec(
        num_scalar_prefetch=1, grid=(B,),
        in_specs=[pl.BlockSpec((None, H, Wq), lambda b, pt: (b, 0, 0)),
                  pl.BlockSpec((None, 1, Wq), lambda b, pt: (b, 0, 0)), any_spec, any_spec],
        out_specs=pl.BlockSpec((None, H, C), lambda b, pt: (b, 0, 0)),
        scratch_shapes=[pltpu.VMEM((2, n_pages, page, C), F32), pltpu.VMEM((2, n_pages, page, R), F32),
                        pltpu.VMEM((P, C), BF16), pltpu.VMEM((H, P), F32),
                        pltpu.SemaphoreType.DMA((2,)), pltpu.SemaphoreType.DMA((2,))])
    return pl.pallas_call(
        functools.partial(_attn_dec_kernel, C=C, R=R, n_pages=n_pages, page=page, cp=DEC_CHUNK_PAGES),
        grid_spec=grid_spec, out_shape=jax.ShapeDtypeStruct((B, H, C), F32), name="attn_dec",
        compiler_params=_cparams(1),
    )(page_table.reshape(-1), q_s, kn, cache_ckv, cache_kpe)


def _mla_out_kernel(x_ref, o_ref, mod_ref, wuvt_ref, wo_ref, xo_ref, *, D, H):
    heads = [o_ref[hh] for hh in range(H)]
    xo_ref[...] = _mla_out(heads, x_ref[...], mod_ref[:, 2 * D:3 * D], wuvt_ref, wo_ref)


def _final_kernel(xs_ref, y_ref, pmod_ref, g_ref, o_ref, *, D):
    o_ref[...] = _rms(_resid(xs_ref, y_ref, pmod_ref, D), g_ref[...])


def _rope_table(pos, d, width):
    inv = ROPE_THETA ** (-jnp.arange(0, d, 2, dtype=F32) / d)
    ang = pos.astype(F32)[:, None] * inv[None, :]
    cos, sin = jnp.cos(ang), jnp.sin(ang)
    pad = jnp.zeros((pos.shape[0], width - d), F32)
    return jnp.concatenate([cos, cos, pad], axis=-1), jnp.concatenate([sin, sin, pad], axis=-1)


def _rot_half_cols(w):
    half = w.shape[-1] // 2
    return jnp.concatenate([-w[..., half:], w[..., :half]], axis=-1)


def _pad_lanes(w):
    pad = jnp.zeros(w.shape[:-1] + (LANE - w.shape[-1],), w.dtype)
    return jnp.concatenate([w, pad], axis=-1)


def _ret_tables(L):
    log_g = jnp.log1p(-jnp.exp2(-5.0 - jnp.arange(RET_HEADS, dtype=F32)))
    idx = jnp.arange(L, dtype=F32)
    diff = idx[:, None] - idx[None, :]
    dec = jnp.where(diff[None] >= 0, jnp.exp(jnp.maximum(diff, 0.0)[None] * log_g[:, None, None]), 0.0)
    qd = jnp.exp((idx + 1.0)[:, None] * log_g[None, :])
    kd = jnp.exp((L - 1.0 - idx)[:, None] * log_g[None, :])
    gl = jnp.exp(L * log_g)[None, :]
    return dec, qd, kd, gl


def kernel(x_prompt, x_sample, c_prompt, c_sample, state_ret, cache_ckv, cache_kpe, page_table, w_ada, b_ada, norm_mix_g, norm_ffn_g, ret_w_in, ret_gn_g, ret_w_out, kv_w_ada, kv_b_ada, kv_norm_g, mla_w_dkv, mla_kv_norm_g, mla_w_uk, mla_w_uv, mla_w_dq, mla_q_norm_g, mla_w_uq, mla_w_o, router_w, router_bias, moe_w_gate, moe_w_up, moe_w_down, shared_w_gate, shared_w_up, shared_w_down, final_norm_g):
    B, S, D = x_prompt.shape
    Bs, Ss, _ = x_sample.shape
    assert Ss == 1
    depth = w_ada.shape[0]
    n_a = ret_w_in.shape[0]
    H = MLA_HEADS
    C = mla_w_dkv.shape[1] - QK_ROPE
    R = QK_ROPE
    dn = mla_w_uk.shape[1]
    past_len = page_table.shape[1] * cache_ckv.shape[1]

    gp = _Group(B, S, TM_PROMPT, False)
    gs = _Group(Bs, 1, Bs, True)
    groups = (gp, gs)
    T_all = gp.T + gs.T
    ys_off = gp.T // gs.tm

    n_c = B + Bs
    n_c_pad = -(-n_c // 8) * 8
    c_all = jnp.concatenate([c_prompt, c_sample, jnp.zeros((n_c_pad - n_c, D), F32)], axis=0)
    mod_all = _ada_mod(c_all, w_ada, b_ada)
    kvmod_all = _ada_mod(c_all, kv_w_ada[None], kv_b_ada[None])[0]
    mods = [[g.mod_array(mod_all[l, lo:lo + g.n_seq]) for l in range(depth)]
            for g, lo in zip(groups, (0, B))]
    kvmods = [g.mod_array(kvmod_all[lo:lo + g.n_seq]) for g, lo in zip(groups, (0, B))]

    pos = (jnp.arange(S), past_len + jnp.arange(1))
    ret_rope = [_rope_table(p, D // RET_HEADS, D // RET_HEADS)[0:2] for p in pos]
    ret_rope = [(c[:, :c.shape[1] // 2], s[:, :s.shape[1] // 2]) for c, s in ret_rope]
    mla_rope = [_rope_table(p, R, LANE) for p in pos]
    tabs_p = _ret_tables(RET_CHUNK)
    gl_s = _ret_tables(1)[3]

    row = lambda g: g.reshape(1, -1)
    w_in = ret_w_in.astype(BF16)
    w_out = ret_w_out.astype(BF16)
    wsg, wsu, wsd = shared_w_gate.astype(BF16), shared_w_up.astype(BF16), shared_w_down.astype(BF16)
    wg, wu, wd = moe_w_gate.astype(BF16), moe_w_up.astype(BF16), moe_w_down.astype(BF16)
    rw_t = router_w.T
    rb = router_bias.reshape(-1, 1)
    w_dkv_c = mla_w_dkv[:, :C].astype(BF16)
    w_dkv_p = _pad_lanes(mla_w_dkv[:, C:]).astype(BF16)
    w_dkv_r = _pad_lanes(_rot_half_cols(mla_w_dkv[:, C:])).astype(BF16)
    w_uk = mla_w_uk.astype(BF16)
    w_uvt = jnp.swapaxes(mla_w_uv, 1, 2).astype(BF16)
    wq = []
    for j in range(depth - n_a):
        w3 = mla_w_uq[j].reshape(-1, H, dn + R)
        w_n = w3[:, :, :dn].reshape(-1, H * dn).astype(BF16)
        w_p = _pad_lanes(w3[:, :, dn:]).reshape(-1, H * LANE).astype(BF16)
        w_r = _pad_lanes(_rot_half_cols(w3[:, :, dn:])).reshape(-1, H * LANE).astype(BF16)
        wq.append((mla_w_dq[j].astype(BF16), row(mla_q_norm_g[j]), w_n, w_p, w_r, w_uk))
    w_o = mla_w_o.astype(BF16)

    xs = [x_prompt.reshape(gp.T, D), x_sample.reshape(gs.T, D)]
    y = None
    states = [[], []]
    ckv_out, kpe_out, kcat = [None, None], [None, None], [None, None]

    for l in range(depth):
        y_offs = (0, ys_off)
        xmid = [None, None]
        if l < n_a:
            for gi, g in enumerate(groups):
                pm = mods[gi][l - 1] if y is not None else None
                x, q, k, v, gt = _ret_in(g, xs[gi], y, y_offs[gi], pm, mods[gi][l], ret_rope[gi][0], ret_rope[gi][1],
                                         row(norm_mix_g[l]), w_in[l], BF16 if gi == 0 else F32)
                if gi == 0:
                    xmid[gi], st = _ret_chunk(B, S, x, q, k, v, gt, mods[gi][l], tabs_p, row(ret_gn_g[l]), w_out[l])
                else:
                    st, o = _ret_dec(state_ret, l, q, k, v, gl_s)
                    xmid[gi], = _tok_call("ret_post", functools.partial(_ret_post_kernel, D=D, H=RET_HEADS), g,
                                          [(x, 0), (o, 0), (gt, 0)], [mods[gi][l]], [],
                                          [row(ret_gn_g[l]), w_out[l]], [(D, F32)])
                states[gi].append(st)
        else:
            j = l - n_a
            for gi, g in enumerate(groups):
                xsg, yy, pm = xs[gi], y, mods[gi][l - 1]
                if l == n_a:
                    xsg, ckv_out[gi], kpe_out[gi], kcat[gi] = _tok_call(
                        "kv_stream", functools.partial(_kv_kernel, D=D, C=C, R=R), g, [(xsg, 0), (y, y_offs[gi])],
                        [pm, kvmods[gi]], list(mla_rope[gi]),
                        [row(kv_norm_g), w_dkv_c, w_dkv_p, w_dkv_r, row(mla_kv_norm_g)],
                        [(D, F32), (C, F32), (R, F32), (C + LANE, BF16 if gi == 0 else F32)])
                    yy, pm = None, None
                x, qcat = _mla_q(g, xsg, yy, y_offs[gi], pm, mods[gi][l], mla_rope[gi][0], mla_rope[gi][1],
                                 row(norm_mix_g[l]), wq[j], BF16 if gi == 0 else F32)
                if gi == 0:
                    xmid[gi] = _attn_prompt(B, S, x, qcat, kcat[gi], mods[gi][l], w_uvt, w_o[j])
                else:
                    o_lat = _attn_dec(jnp.swapaxes(qcat, 0, 1), kcat[gi][:, None, :], cache_ckv, cache_kpe, page_table)
                    xmid[gi] = pl.pallas_call(
                        functools.partial(_mla_out_kernel, D=D, H=H), grid=(1,),
                        in_specs=[_const_spec(x), pl.BlockSpec((H, Bs, C), lambda i: (0, 0, 0)),
                                  _const_spec(mods[gi][l]), _const_spec(w_uvt), _const_spec(w_o[j])],
                        out_specs=_const_spec(x), out_shape=jax.ShapeDtypeStruct(x.shape, F32),
                        name="mla_out", compiler_params=_cparams(1),
                    )(x, jnp.swapaxes(o_lat, 0, 1), mods[gi][l], w_uvt, w_o[j])
        hx, cls = [], []
        for gi, g in enumerate(groups):
            h, xs[gi], c = _moe_pre(g, xmid[gi], mods[gi][l], row(norm_ffn_g[l]), rw_t, rb, wsg[l], wsu[l], wsd[l])
            hx.append(h)
            cls.append(c)
        y = _moe_routed(hx[0], hx[1], jnp.concatenate(cls), wg[l], wu[l], wd[l])

    outs = []
    for gi, g in enumerate(groups):
        o, = _tok_call("final_norm", functools.partial(_final_kernel, D=D), g, [(xs[gi], 0), (y, (0, ys_off)[gi])],
                       [mods[gi][depth - 1]], [], [row(final_norm_g)], [(D, F32)])
        outs.append(o)
    return (outs[0].reshape(B, S, D), outs[1].reshape(Bs, 1, D),
            jnp.stack(states[0]), jnp.stack(states[1]),
            ckv_out[0].reshape(B, S, C), kpe_out[0].reshape(B, S, R),
            ckv_out[1].reshape(Bs, 1, C), kpe_out[1].reshape(Bs, 1, R))
```

```python
import functools

import jax
import jax.numpy as jnp
from jax import lax
from jax.experimental import pallas as pl
from jax.experimental.pallas import tpu as pltpu

F32 = jnp.float32
BF16 = jnp.bfloat16

RET_HEADS = 4
MLA_HEADS = 8
QK_ROPE = 64
N_GROUPS = 4
EXPERTS_PER_GROUP = 4
ROPE_THETA = 10000.0
EPS = 1e-6

_PAIRS = ((0, 1), (0, 2), (0, 3), (1, 2), (1, 3), (2, 3))
N_CLASSES = N_GROUPS * len(_PAIRS)

LANE = 128
VMEM_LIMIT_BYTES = 56 * 2 ** 20
TM_PROMPT = 256
MOE_TILE = 256
RET_CHUNK = 256
ATT_TQ = 256
ATT_TK = 256
ATT_CW = 512
DEC_NB = 8
DEC_CHUNK_PAGES = 8


def _dot(a, b):
    return jnp.dot(a, b, preferred_element_type=F32)


def _dot_nt(a, b):
    return lax.dot_general(a, b, (((1,), (1,)), ((), ())), preferred_element_type=F32)


def _dot_tn(a, b):
    return lax.dot_general(a, b, (((0,), (0,)), ((), ())), preferred_element_type=F32)


def _silu(x):
    return x * jax.nn.sigmoid(x)


def _rms(x, g):
    return x * lax.rsqrt(jnp.mean(x * x, axis=-1, keepdims=True) + EPS) * g


def _cparams(n_axes=1):
    return pltpu.CompilerParams(dimension_semantics=("arbitrary",) * n_axes,
                                vmem_limit_bytes=VMEM_LIMIT_BYTES)


class _Group:
    def __init__(self, n_seq, seq_len, tm, per_token_mod):
        self.n_seq, self.seq_len, self.tm = n_seq, seq_len, tm
        self.T = n_seq * seq_len
        self.n_tiles = self.T // tm
        self.per_token_mod = per_token_mod
        self.tiles_per_seq = max(seq_len // tm, 1)

    def tok_spec(self, width, offset=0):
        return pl.BlockSpec((self.tm, width), lambda i: (i + offset, 0))

    def mod_spec(self, width):
        if self.per_token_mod:
            return pl.BlockSpec((self.tm, width), lambda i: (i, 0))
        tps = self.tiles_per_seq
        return pl.BlockSpec((None, 1, width), lambda i: (i // tps, 0, 0))

    def rope_spec(self, width):
        if self.per_token_mod:
            return pl.BlockSpec((1, width), lambda i: (0, 0))
        tps = self.tiles_per_seq
        return pl.BlockSpec((self.tm, width), lambda i: (i % tps, 0))

    def mod_array(self, m):
        return m if self.per_token_mod else m[:, None, :]


def _const_spec(a):
    nd = a.ndim
    return pl.BlockSpec(a.shape, lambda i: (0,) * nd)


def _tok_call(name, body, grp, tok_ins, mod_ins, rope_ins, const_ins, outs):
    in_specs, args = [], []
    for a, off in tok_ins:
        in_specs.append(grp.tok_spec(a.shape[-1], off)); args.append(a)
    for a in mod_ins:
        in_specs.append(grp.mod_spec(a.shape[-1])); args.append(a)
    for a in rope_ins:
        in_specs.append(grp.rope_spec(a.shape[-1])); args.append(a)
    for a in const_ins:
        in_specs.append(_const_spec(a)); args.append(a)
    out_specs = [grp.tok_spec(width) for width, _ in outs]
    out_shape = [jax.ShapeDtypeStruct((grp.T, width), dtype) for width, dtype in outs]
    return pl.pallas_call(
        body, grid=(grp.n_tiles,), in_specs=in_specs, out_specs=out_specs, out_shape=out_shape,
        name=name, compiler_params=_cparams(1),
    )(*args)


def _mod_kernel(c_ref, w_ref, b_ref, o_ref):
    c = c_ref[...]
    o_ref[...] = _dot(_silu(c).astype(BF16), w_ref[...].astype(BF16)) + b_ref[...]


def _ada_mod(c_all, w, b):
    L, D, N = w.shape
    Bc = c_all.shape[0]
    tn = min(N, 2048)
    return pl.pallas_call(
        _mod_kernel, grid=(L, N // tn),
        in_specs=[pl.BlockSpec((Bc, D), lambda l, j: (0, 0)),
                  pl.BlockSpec((None, D, tn), lambda l, j: (l, 0, j)),
                  pl.BlockSpec((None, 1, tn), lambda l, j: (l, 0, j))],
        out_specs=pl.BlockSpec((None, Bc, tn), lambda l, j: (l, 0, j)),
        out_shape=jax.ShapeDtypeStruct((L, Bc, N), F32),
        name="ada_mod", compiler_params=_cparams(2),
    )(c_all, w, b[:, None, :])


def _resid(xs_ref, y_ref, pmod_ref, D):
    x = xs_ref[...]
    if y_ref is not None:
        x = x + pmod_ref[:, 5 * D:6 * D] * y_ref[...]
    return x


def _ret_in_kernel(*refs, has_y, D, H):
    refs = list(refs)
    xs_ref = refs.pop(0)
    y_ref = refs.pop(0) if has_y else None
    pmod_ref = refs.pop(0) if has_y else None
    mod_ref, cos_ref, sin_ref, g_ref, w_ref = refs[:5]
    outs = refs[5:]
    if has_y:
        xo_ref, outs = outs[0], outs[1:]
    q_ref, k_ref, v_ref, gt_ref = outs
    x = _resid(xs_ref, y_ref, pmod_ref, D)
    if has_y:
        xo_ref[...] = x
    h = _rms(x, g_ref[...]) * (1.0 + mod_ref[:, D:2 * D]) + mod_ref[:, 0:D]
    hb = h.astype(BF16)
    dk = D // H
    half = dk // 2
    cos, sin = cos_ref[...], sin_ref[...]
    for idx, (o_ref, scale) in enumerate(((q_ref, float(dk) ** -0.5), (k_ref, 1.0))):
        t = _dot(hb, w_ref[:, idx * D:(idx + 1) * D])
        for hh in range(H):
            x1 = t[:, hh * dk:hh * dk + half]
            x2 = t[:, hh * dk + half:(hh + 1) * dk]
            o_ref[:, hh * dk:hh * dk + half] = ((x1 * cos - x2 * sin) * scale).astype(o_ref.dtype)
            o_ref[:, hh * dk + half:(hh + 1) * dk] = ((x1 * sin + x2 * cos) * scale).astype(o_ref.dtype)
    v_ref[...] = _dot(hb, w_ref[:, 2 * D:3 * D]).astype(v_ref.dtype)
    gt_ref[...] = _dot(hb, w_ref[:, 3 * D:4 * D])


def _ret_in(grp, xs, y, y_off, pmod, mod, cos, sin, norm_g, w_in, qkv_dtype):
    D = xs.shape[-1]
    has_y = y is not None
    tok = [(xs, 0)] + ([(y, y_off)] if has_y else [])
    mods = ([pmod] if has_y else []) + [mod]
    outs = ([(D, F32)] if has_y else []) + [(D, qkv_dtype)] * 3 + [(D, F32)]
    res = _tok_call("ret_in", functools.partial(_ret_in_kernel, has_y=has_y, D=D, H=RET_HEADS), grp,
                    tok, mods, [cos, sin], [norm_g, w_in], outs)
    if not has_y:
        res = [xs] + list(res)
    return res


def _gn_gate(o, g, gn):
    mu = jnp.mean(o, axis=-1, keepdims=True)
    c = o - mu
    var = jnp.mean(c * c, axis=-1, keepdims=True)
    return _silu(g) * (c * lax.rsqrt(var + EPS) * gn)


def _ret_chunk_kernel(x_ref, q_ref, k_ref, v_ref, gt_ref, mod_ref, dec_ref, qd_ref, kd_ref, gl_ref,
                      gn_ref, wo_ref, xo_ref, s_ref, gat_ref, *, D, H):
    @pl.when(pl.program_id(1) == 0)
    def _():
        s_ref[...] = jnp.zeros_like(s_ref)

    dk = D // H
    for h in range(H):
        hs = slice(h * dk, (h + 1) * dk)
        qh, kh, vh = q_ref[:, hs], k_ref[:, hs], v_ref[:, hs]
        S = s_ref[h]
        scores = _dot_nt(qh, kh) * dec_ref[h]
        inner = _dot(scores.astype(BF16), vh)
        cross = _dot((qh.astype(F32) * qd_ref[:, h:h + 1]).astype(BF16), S.astype(BF16))
        kdec = (kh.astype(F32) * kd_ref[:, h:h + 1]).astype(BF16)
        s_ref[h] = gl_ref[:, h:h + 1] * S + _dot_tn(kdec, vh)
        gat_ref[:, hs] = _gn_gate(inner + cross, gt_ref[:, hs], gn_ref[:, hs]).astype(BF16)
    y = _dot(gat_ref[...], wo_ref[...])
    xo_ref[...] = x_ref[...] + mod_ref[:, 2 * D:3 * D] * y


def _ret_chunk(B, S, x, q, k, v, gt, mod, tabs, gn_g, w_out):
    D = x.shape[-1]
    H = RET_HEADS
    L = RET_CHUNK
    nc = S // L
    dk = D // H
    dec, qd, kd, gl = tabs
    tok = lambda w: pl.BlockSpec((L, w), lambda b, c: (b * nc + c, 0))
    cst = lambda a: pl.BlockSpec(a.shape, lambda b, c: (0,) * a.ndim)
    return pl.pallas_call(
        functools.partial(_ret_chunk_kernel, D=D, H=H), grid=(B, nc),
        in_specs=[tok(D)] * 5 + [pl.BlockSpec((None, 1, mod.shape[-1]), lambda b, c: (b, 0, 0)),
                                 cst(dec), cst(qd), cst(kd), cst(gl), cst(gn_g), cst(w_out)],
        out_specs=[tok(D), pl.BlockSpec((None, H, dk, dk), lambda b, c: (b, 0, 0, 0))],
        out_shape=[jax.ShapeDtypeStruct(x.shape, F32), jax.ShapeDtypeStruct((B, H, dk, dk), F32)],
        scratch_shapes=[pltpu.VMEM((L, D), BF16)],
        name="ret_chunk", compiler_params=_cparams(2),
    )(x, q, k, v, gt, mod, dec, qd, kd, gl, gn_g, w_out)


def _ret_dec_kernel(st_ref, q_ref, k_ref, v_ref, gl_ref, so_ref, o_ref, *, D, H, nb):
    dk = D // H
    for h in range(H):
        hs = slice(h * dk, (h + 1) * dk)
        gh = gl_ref[:, h:h + 1]
        qh, kh, vh = q_ref[:, hs], k_ref[:, hs], v_ref[:, hs]
        inner = jnp.sum(qh * kh, axis=-1, keepdims=True) * vh
        qT = (qh * gh).T
        kT = kh.T
        rows = []
        for j in range(nb):
            S = st_ref[j, h]
            so_ref[j, h] = gh * S + kT[:, j:j + 1] * vh[j:j + 1, :]
            rows.append(jnp.sum(qT[:, j:j + 1] * S, axis=0, keepdims=True))
        o_ref[:, hs] = inner + jnp.concatenate(rows, axis=0)


def _ret_dec(state, layer, q, k, v, gl):
    _, B, H, dk, dv = state.shape
    D = q.shape[-1]
    nb = DEC_NB
    row = pl.BlockSpec((nb, D), lambda i: (i, 0))
    return pl.pallas_call(
        functools.partial(_ret_dec_kernel, D=D, H=H, nb=nb), grid=(B // nb,),
        in_specs=[pl.BlockSpec((None, nb, H, dk, dv), lambda i: (layer, i, 0, 0, 0)), row, row, row,
                  pl.BlockSpec(gl.shape, lambda i: (0, 0))],
        out_specs=[pl.BlockSpec((nb, H, dk, dv), lambda i: (i, 0, 0, 0)), row],
        out_shape=[jax.ShapeDtypeStruct((B, H, dk, dv), F32), jax.ShapeDtypeStruct((B, D), F32)],
        name="ret_dec", compiler_params=_cparams(1),
    )(state, q, k, v, gl)


def _ret_post_kernel(x_ref, o_ref, gt_ref, mod_ref, gn_ref, wo_ref, xo_ref, *, D, H):
    dk = D // H
    parts = []
    for h in range(H):
        hs = slice(h * dk, (h + 1) * dk)
        parts.append(_gn_gate(o_ref[:, hs], gt_ref[:, hs], gn_ref[:, hs]).astype(BF16))
    y = _dot(jnp.concatenate(parts, axis=-1), wo_ref[...])
    xo_ref[...] = x_ref[...] + mod_ref[:, 2 * D:3 * D] * y


def _route(logits_t, bias):
    scores = jax.nn.sigmoid(logits_t)
    sel = scores + bias
    n_e = EXPERTS_PER_GROUP
    row = lambda a, e: a[e:e + 1, :]
    gscore = []
    for g in range(N_GROUPS):
        best2 = None
        for a, b in _PAIRS:
            s = row(sel, n_e * g + a) + row(sel, n_e * g + b)
            best2 = s if best2 is None else jnp.maximum(best2, s)
        gscore.append(best2)
    best, bestv = jnp.zeros_like(gscore[0], dtype=jnp.int32), gscore[0]
    for g in range(1, N_GROUPS):
        upd = gscore[g] > bestv
        best = jnp.where(upd, g, best)
        bestv = jnp.where(upd, gscore[g], bestv)

    def pick(a, j):
        out = row(a, j)
        for g in range(1, N_GROUPS):
            out = jnp.where(best == g, row(a, n_e * g + j), out)
        return out

    v = [pick(sel, j) for j in range(n_e)]
    sc = [pick(scores, j) for j in range(n_e)]
    i1, v1 = jnp.zeros_like(best), v[0]
    for j in range(1, n_e):
        upd = v[j] > v1
        i1 = jnp.where(upd, j, i1)
        v1 = jnp.where(upd, v[j], v1)
    i2, v2 = None, None
    for j in range(n_e):
        ok = i1 != j
        if i2 is None:
            i2 = jnp.where(ok, j, n_e)
            v2 = jnp.where(ok, v[j], -jnp.inf)
        else:
            upd = ok & (v[j] > v2)
            i2 = jnp.where(upd, j, i2)
            v2 = jnp.where(upd, v[j], v2)
    s1 = sc[0]
    s2 = sc[0]
    for j in range(1, n_e):
        s1 = jnp.where(i1 == j, sc[j], s1)
        s2 = jnp.where(i2 == j, sc[j], s2)
    tot = s1 + s2
    w1, w2 = s1 / tot, s2 / tot
    lo = jnp.minimum(i1, i2)
    hi = jnp.maximum(i1, i2)
    base = jnp.where(lo == 0, 0, jnp.where(lo == 1, 3, 5))
    cls = best * len(_PAIRS) + base + hi - lo - 1
    first_is_lo = i1 < i2
    return cls, jnp.where(first_is_lo, w1, w2), jnp.where(first_is_lo, w2, w1)


def _moe_pre_kernel(x_ref, mod_ref, g_ref, rwt_ref, rb_ref, wsg_ref, wsu_ref, wsd_ref, *rest, D):
    hx_ref, xs_ref, cls_ref = rest[-3:]
    x = x_ref[...]
    h = _rms(x, g_ref[...]) * (1.0 + mod_ref[:, 4 * D:5 * D]) + mod_ref[:, 3 * D:4 * D]
    hb = h.astype(BF16)
    h_lo = (h - hb.astype(F32)).astype(BF16)
    rw = rwt_ref[...]
    rw_hi = rw.astype(BF16)
    rw_lo = (rw - rw_hi.astype(F32)).astype(BF16)
    logits_t = _dot_nt(rw_hi, hb) + (_dot_nt(rw_lo, hb) + _dot_nt(rw_hi, h_lo))
    cls, wa, wb = _route(logits_t, rb_ref[...])
    tm = x.shape[0]
    cls_ref[0] = cls
    rowi = lax.broadcasted_iota(jnp.int32, (LANE, tm), 0)
    extra = jnp.where(rowi == 0, wa, jnp.where(rowi == 1, wb, 0.0)).T
    hx_ref[:, 0:D] = h
    hx_ref[:, D:D + LANE] = extra
    hid = _silu(_dot(hb, wsg_ref[...])) * _dot(hb, wsu_ref[...])
    shared = _dot(hid.astype(BF16), wsd_ref[...])
    xs_ref[...] = x + mod_ref[:, 5 * D:6 * D] * shared


def _moe_pre(grp, x, mod, norm_g, rw_t, rb, wsg, wsu, wsd):
    D = x.shape[-1]
    in_specs = [grp.tok_spec(D), grp.mod_spec(mod.shape[-1])] + [_const_spec(a) for a in (norm_g, rw_t, rb, wsg, wsu, wsd)]
    hx, xs, cls = pl.pallas_call(
        functools.partial(_moe_pre_kernel, D=D), grid=(grp.n_tiles,), in_specs=in_specs,
        out_specs=[grp.tok_spec(D + LANE), grp.tok_spec(D), pl.BlockSpec((1, 1, grp.tm), lambda i: (i, 0, 0))],
        out_shape=[jax.ShapeDtypeStruct((grp.T, D + LANE), F32), jax.ShapeDtypeStruct((grp.T, D), F32),
                   jax.ShapeDtypeStruct((grp.n_tiles, 1, grp.tm), jnp.int32)],
        name="moe_pre", compiler_params=_cparams(1),
    )(x, mod, norm_g, rw_t, rb, wsg, wsu, wsd)
    return hx, xs, cls.reshape(-1)


def _moe_kernel(pos_ref, te1_ref, te2_ref, nv_ref, dump_ref, nused_ref, hx_hbm, wg1, wu1, wd1, wg2, wu2, wd2,
                y_hbm, src_ref, hbuf, obuf, gsem, ssem, *, T_all, n_rows, tm, D):
    i = pl.program_id(0)
    nused = nused_ref[0]
    slot = i % 2

    def gather_row(tile, sl, r):
        t = src_ref[tile * tm + r]
        pltpu.make_async_copy(hx_hbm.at[pl.ds(t, 1)], hbuf.at[sl, pl.ds(r, 1)], gsem.at[sl]).start()

    def scatter_row(tile, sl, r):
        nv = nv_ref[tile]
        t = jnp.where(r < nv, src_ref[tile * tm + r], dump_ref[tile] + (r - nv))
        pltpu.make_async_copy(obuf.at[sl, pl.ds(r, 1)], y_hbm.at[pl.ds(t, 1)], ssem.at[sl]).start()

    def rolled(fn, tile, sl):
        def step(r, c):
            fn(tile, sl, r)
            return c
        lax.fori_loop(0, tm, step, 0, unroll=8)

    def unrolled(fn, tile, sl):
        for r in range(tm):
            fn(tile, sl, r)

    def wait_gather(sl):
        pltpu.make_async_copy(hx_hbm.at[pl.ds(0, tm)], hbuf.at[sl], gsem.at[sl]).wait()

    def wait_scatter(sl):
        pltpu.make_async_copy(obuf.at[sl], y_hbm.at[pl.ds(0, tm)], ssem.at[sl]).wait()

    def compute(sl):
        hx = hbuf[sl]
        hb = hx[:, 0:D].astype(BF16)
        wa = hx[:, D:D + 1]
        wb = hx[:, D + 1:D + 2]
        hid_a = (_silu(_dot(hb, wg1[...])) * _dot(hb, wu1[...]) * wa).astype(BF16)
        hid_b = (_silu(_dot(hb, wg2[...])) * _dot(hb, wu2[...]) * wb).astype(BF16)
        obuf[sl] = _dot(hid_a, wd1[...]) + _dot(hid_b, wd2[...])

    @pl.when(i == 0)
    def _():
        def init(r, c):
            src_ref[r] = 0
            return c
        lax.fori_loop(0, n_rows, init, 0, unroll=8)

        def fill(t, c):
            src_ref[pos_ref[t]] = t
            return c
        lax.fori_loop(0, T_all, fill, 0, unroll=8)
        rolled(gather_row, 0, 0)

    active = i < nused
    steady = jnp.logical_and(i >= 1, i + 1 < nused)

    @pl.when(active)
    def _():
        wait_gather(slot)

        @pl.when(i >= 2)
        def _():
            wait_scatter(slot)

    @pl.when(steady)
    def _():
        compute(slot)
        unrolled(scatter_row, i - 1, 1 - slot)
        unrolled(gather_row, i + 1, 1 - slot)

    @pl.when(jnp.logical_and(active, jnp.logical_not(steady)))
    def _():
        @pl.when(i + 1 < nused)
        def _():
            rolled(gather_row, i + 1, 1 - slot)
        compute(slot)

        @pl.when(i >= 1)
        def _():
            rolled(scatter_row, i - 1, 1 - slot)

        @pl.when(i == nused - 1)
        def _():
            rolled(scatter_row, i, slot)
            wait_scatter(slot)

            @pl.when(i >= 1)
            def _():
                wait_scatter(1 - slot)

    @pl.when(jnp.logical_not(active))
    def _():
        obuf[slot] = jnp.zeros(obuf.shape[1:], F32)
        rolled(scatter_row, i, slot)
        wait_scatter(slot)


def _moe_routed(hx, cls, layer, wg, wu, wd):
    T_all, W = hx.shape
    D = W - LANE
    Fe = wg.shape[-1]
    tm = MOE_TILE
    n_tiles = (T_all + N_CLASSES * (tm - 1)) // tm
    n_rows = n_tiles * tm
    onehot = (cls[:, None] == jnp.arange(N_CLASSES, dtype=jnp.int32)[None, :]).astype(jnp.int32)
    counts = jnp.sum(onehot, axis=0)
    rank = jnp.sum((jnp.cumsum(onehot, axis=0) - onehot) * onehot, axis=1)
    ntile_c = (counts + tm - 1) // tm
    tile_end = jnp.cumsum(ntile_c)
    tile_off = tile_end - ntile_c
    pos = (jnp.sum(onehot * tile_off[None, :], axis=1) * tm + rank).astype(jnp.int32)
    nused = tile_end[-1]
    tiles = jnp.arange(n_tiles, dtype=jnp.int32)
    tid = jnp.minimum(tiles, nused - 1)
    tile_cls = jnp.sum((tid[:, None] >= tile_end[None, :]).astype(jnp.int32), axis=1)
    nvalid = jnp.clip(counts[tile_cls] - (tiles - tile_off[tile_cls]) * tm, 0, tm)
    nvalid = jnp.where(tiles < nused, nvalid, 0).astype(jnp.int32)
    dump = (T_all + tiles * tm - (jnp.cumsum(nvalid) - nvalid)).astype(jnp.int32)
    pair_lo = jnp.array([a for a, _ in _PAIRS], jnp.int32)
    pair_hi = jnp.array([b for _, b in _PAIRS], jnp.int32)
    grp_id, pair_id = tile_cls // len(_PAIRS), tile_cls % len(_PAIRS)
    te1 = (grp_id * EXPERTS_PER_GROUP + pair_lo[pair_id]).astype(jnp.int32)
    te2 = (grp_id * EXPERTS_PER_GROUP + pair_hi[pair_id]).astype(jnp.int32)

    w1 = lambda shape: pl.BlockSpec((None, None) + shape, lambda i, pos, te1, *_: (layer, te1[i], 0, 0))
    w2 = lambda shape: pl.BlockSpec((None, None) + shape, lambda i, pos, te1, te2, *_: (layer, te2[i], 0, 0))
    any_spec = pl.BlockSpec(memory_space=pl.ANY)
    grid_spec = pltpu.PrefetchScalarGridSpec(
        num_scalar_prefetch=6, grid=(n_tiles,),
        in_specs=[any_spec, w1((D, Fe)), w1((D, Fe)), w1((Fe, D)), w2((D, Fe)), w2((D, Fe)), w2((Fe, D))],
        out_specs=any_spec,
        scratch_shapes=[pltpu.SMEM((n_rows,), jnp.int32),
                        pltpu.VMEM((2, tm, W), F32), pltpu.VMEM((2, tm, D), F32),
                        pltpu.SemaphoreType.DMA((2,)), pltpu.SemaphoreType.DMA((2,))])
    return pl.pallas_call(
        functools.partial(_moe_kernel, T_all=T_all, n_rows=n_rows, tm=tm, D=D),
        grid_spec=grid_spec, out_shape=jax.ShapeDtypeStruct((n_rows, D), F32),
        name="moe_routed", compiler_params=_cparams(1),
    )(pos, te1, te2, nvalid, dump, nused.reshape(1).astype(jnp.int32), hx, wg, wu, wd, wg, wu, wd)


def _kv_kernel(xs_ref, y_ref, pmod_ref, kvmod_ref, cos_ref, sin_ref, g_ref, wc_ref, wp_ref, wr_ref, lg_ref,
               xo_ref, ckv_ref, kpe_ref, kcat_ref, *maybe_vt_ref, D, C, R):
    x = _resid(xs_ref, y_ref, pmod_ref, D)
    xo_ref[...] = x
    hn = (_rms(x, g_ref[...]) * (1.0 + kvmod_ref[:, D:2 * D]) + kvmod_ref[:, 0:D]).astype(BF16)
    ckv = _rms(_dot(hn, wc_ref[...]), lg_ref[...])
    kpe = _dot(hn, wp_ref[...]) * cos_ref[...] + _dot(hn, wr_ref[...]) * sin_ref[...]
    ckv_ref[...] = ckv
    kpe_ref[...] = kpe[:, 0:R]
    kcat_ref[:, 0:C] = ckv.astype(kcat_ref.dtype)
    kcat_ref[:, C:C + LANE] = kpe.astype(kcat_ref.dtype)
    if maybe_vt_ref:
        maybe_vt_ref[0][...] = ckv.T.astype(maybe_vt_ref[0].dtype)


def _kv_stream(grp, xs, y, y_off, pmod, kvmod, cos, sin, consts, C, R, kcat_dtype, with_vt):
    D = xs.shape[-1]
    tok_ins = [xs, y]
    in_specs = [grp.tok_spec(D), grp.tok_spec(D, y_off), grp.mod_spec(pmod.shape[-1]),
                grp.mod_spec(kvmod.shape[-1]), grp.rope_spec(LANE), grp.rope_spec(LANE)]
    in_specs += [_const_spec(a) for a in consts]
    widths = [(D, F32), (C, F32), (R, F32), (C + LANE, kcat_dtype)]
    out_specs = [grp.tok_spec(w) for w, _ in widths]
    out_shape = [jax.ShapeDtypeStruct((grp.T, w), dt) for w, dt in widths]
    if with_vt:
        tps = grp.tiles_per_seq
        out_specs.append(pl.BlockSpec((None, C, grp.tm), lambda i: (i // tps, 0, i % tps)))
        out_shape.append(jax.ShapeDtypeStruct((grp.n_seq, C, grp.seq_len), BF16))
    return pl.pallas_call(
        functools.partial(_kv_kernel, D=D, C=C, R=R), grid=(grp.n_tiles,), in_specs=in_specs,
        out_specs=out_specs, out_shape=out_shape, name="kv_stream", compiler_params=_cparams(1),
    )(*tok_ins, pmod, kvmod, cos, sin, *consts)


def _mla_q_kernel(*refs, has_y, transposed, D, C, H, scale):
    refs = list(refs)
    xs_ref = refs.pop(0)
    y_ref = refs.pop(0) if has_y else None
    pmod_ref = refs.pop(0) if has_y else None
    mod_ref, cos_ref, sin_ref, g_ref, wdq_ref, qg_ref, wn_ref, wp_ref, wr_ref, wuk_ref = refs[:10]
    outs = refs[10:]
    if has_y:
        xo_ref, outs = outs[0], outs[1:]
    qcat_ref, = outs
    x = _resid(xs_ref, y_ref, pmod_ref, D)
    if has_y:
        xo_ref[...] = x
    h = (_rms(x, g_ref[...]) * (1.0 + mod_ref[:, D:2 * D]) + mod_ref[:, 0:D]).astype(BF16)
    cq = _rms(_dot(h, wdq_ref[...]), qg_ref[...]).astype(BF16)
    q_nope = _dot(cq, wn_ref[...]).astype(BF16)
    cos, sin = cos_ref[...], sin_ref[...]
    dn = q_nope.shape[-1] // H
    for hh in range(H):
        ls = slice(hh * LANE, (hh + 1) * LANE)
        q_lat = _dot(q_nope[:, hh * dn:(hh + 1) * dn], wuk_ref[hh])
        q_pe = _dot(cq, wp_ref[:, ls]) * cos + _dot(cq, wr_ref[:, ls]) * sin
        if transposed:
            tm = q_lat.shape[0]
            qcat_ref[0:C, hh * tm:(hh + 1) * tm] = (q_lat * scale).T.astype(qcat_ref.dtype)
            qcat_ref[C:C + LANE, hh * tm:(hh + 1) * tm] = (q_pe * scale).T.astype(qcat_ref.dtype)
        else:
            qcat_ref[hh, :, 0:C] = (q_lat * scale).astype(qcat_ref.dtype)
            qcat_ref[hh, :, C:C + LANE] = (q_pe * scale).astype(qcat_ref.dtype)


def _mla_q(grp, xs, y, y_off, pmod, mod, cos, sin, norm_g, wq, out_dtype, transposed):
    D = xs.shape[-1]
    w_dq, qg, w_n, w_p, w_r, w_uk = wq
    C = w_uk.shape[-1]
    H = MLA_HEADS
    has_y = y is not None
    scale = float(w_uk.shape[1] + QK_ROPE) ** -0.5
    in_specs = [grp.tok_spec(D)] + ([grp.tok_spec(D, y_off), grp.mod_spec(pmod.shape[-1])] if has_y else [])
    args = [xs] + ([y, pmod] if has_y else [])
    in_specs += [grp.mod_spec(mod.shape[-1]), grp.rope_spec(LANE), grp.rope_spec(LANE)]
    args += [mod, cos, sin]
    for a in (norm_g, w_dq, qg, w_n, w_p, w_r, w_uk):
        in_specs.append(_const_spec(a)); args.append(a)
    if transposed:
        q_spec = pl.BlockSpec((None, C + LANE, H * grp.tm), lambda i: (i, 0, 0))
        q_shape = jax.ShapeDtypeStruct((grp.n_tiles, C + LANE, H * grp.tm), out_dtype)
    else:
        q_spec = pl.BlockSpec((H, grp.tm, C + LANE), lambda i: (0, i, 0))
        q_shape = jax.ShapeDtypeStruct((H, grp.T, C + LANE), out_dtype)
    out_specs = ([grp.tok_spec(D)] if has_y else []) + [q_spec]
    out_shape = ([jax.ShapeDtypeStruct((grp.T, D), F32)] if has_y else []) + [q_shape]
    res = pl.pallas_call(
        functools.partial(_mla_q_kernel, has_y=has_y, transposed=transposed, D=D, C=C, H=H, scale=scale),
        grid=(grp.n_tiles,),
        in_specs=in_specs, out_specs=out_specs, out_shape=out_shape, name="mla_q", compiler_params=_cparams(1),
    )(*args)
    return (res[0], res[1]) if has_y else (xs, res[0])


def _mla_out(o_lat_heads, x, g1, wuvt_ref, wo_ref):
    parts = [_dot(o.astype(BF16), wuvt_ref[hh]).astype(BF16) for hh, o in enumerate(o_lat_heads)]
    return x + g1 * _dot(jnp.concatenate(parts, axis=-1), wo_ref[...])


def _attn_kernel(x_ref, qt_ref, k_ref, vt_ref, mod_ref, wuvt_ref, wo_ref, xo_ref, m_ref, l_ref, acc_ref,
                 *, D, C, H, tq, tk, cw):
    qi = pl.program_id(1)
    m_ref[...] = jnp.full(m_ref.shape, -jnp.inf, F32)
    l_ref[...] = jnp.zeros(l_ref.shape, F32)
    acc_ref[...] = jnp.zeros(acc_ref.shape, F32)
    n_col = H * tq

    def block(j, masked):
        k0 = pl.multiple_of(j * tk, tk)
        kb = k_ref[pl.ds(k0, tk), :]
        vt = vt_ref[:, pl.ds(k0, tk)]
        if masked:
            kpos = lax.broadcasted_iota(jnp.int32, (tk, cw), 0)
            qpos = lax.broadcasted_iota(jnp.int32, (tk, cw), 1) & (tq - 1)
            keep = kpos <= qpos
        s_next = _dot(kb, qt_ref[:, 0:cw])
        for c0 in range(0, n_col, cw):
            cs = slice(c0, c0 + cw)
            s = s_next
            if c0 + cw < n_col:
                s_next = _dot(kb, qt_ref[:, c0 + cw:c0 + 2 * cw])
            if masked:
                s = jnp.where(keep, s, -1e30)
            m_old = m_ref[:, cs]
            m_new = jnp.maximum(m_old, jnp.max(s, axis=0, keepdims=True))
            alpha = jnp.exp(m_old - m_new)
            p = jnp.exp(s - m_new)
            l_ref[:, cs] = alpha * l_ref[:, cs] + jnp.sum(p, axis=0, keepdims=True)
            acc_ref[:, cs] = alpha * acc_ref[:, cs] + _dot(vt, p.astype(BF16))
            m_ref[:, cs] = m_new

    def body(j, c):
        block(j, False)
        return c
    lax.fori_loop(0, qi, body, 0)
    block(qi, True)
    parts = []
    for hh in range(H):
        cs = slice(hh * tq, (hh + 1) * tq)
        o_t = (acc_ref[:, cs] / l_ref[:, cs]).astype(BF16)
        parts.append(_dot_tn(o_t, wuvt_ref[hh]).astype(BF16))
    y = _dot(jnp.concatenate(parts, axis=-1), wo_ref[...])
    xo_ref[...] = x_ref[...] + mod_ref[:, 2 * D:3 * D] * y


def _attn_prompt(B, S, x, qt, kcat, vt, mod, w_uvt, w_o):
    D = x.shape[-1]
    H = MLA_HEADS
    _, Wq, n_col = qt.shape
    C = w_uvt.shape[1]
    tq, tk = ATT_TQ, ATT_TK
    assert tq == tk and tq & (tq - 1) == 0 and n_col == H * tq
    nq = S // tq
    cst = lambda a: pl.BlockSpec(a.shape, lambda b, i: (0,) * a.ndim)
    return pl.pallas_call(
        functools.partial(_attn_kernel, D=D, C=C, H=H, tq=tq, tk=tk, cw=ATT_CW), grid=(B, nq),
        in_specs=[pl.BlockSpec((tq, D), lambda b, i: (b * nq + i, 0)),
                  pl.BlockSpec((None, Wq, n_col), lambda b, i: (b * nq + i, 0, 0)),
                  pl.BlockSpec((S, Wq), lambda b, i: (b, 0)),
                  pl.BlockSpec((None, C, S), lambda b, i: (b, 0, 0)),
                  pl.BlockSpec((None, 1, mod.shape[-1]), lambda b, i: (b, 0, 0)), cst(w_uvt), cst(w_o)],
        out_specs=pl.BlockSpec((tq, D), lambda b, i: (b * nq + i, 0)),
        out_shape=jax.ShapeDtypeStruct(x.shape, F32),
        scratch_shapes=[pltpu.VMEM((1, n_col), F32), pltpu.VMEM((1, n_col), F32), pltpu.VMEM((C, n_col), F32)],
        name="attn_prompt", compiler_params=_cparams(2),
    )(x, qt, kcat, vt, mod, w_uvt, w_o)


def _attn_dec_kernel(pt_ref, q_ref, kn_ref, ckv_hbm, kpe_hbm, o_ref, cbuf, pbuf, kb_ref, s_ref, csem, psem,
                     *, C, R, n_pages, page, cp):
    b = pl.program_id(0)
    nb = pl.num_programs(0)
    slot = b % 2

    def start(seq, sl):
        def body(p, c):
            pg = pt_ref[seq * n_pages + p]
            pltpu.make_async_copy(ckv_hbm.at[pl.ds(pg, 1)], cbuf.at[sl, pl.ds(p, 1)], csem.at[sl]).start()
            pltpu.make_async_copy(kpe_hbm.at[pl.ds(pg, 1)], pbuf.at[sl, pl.ds(p, 1)], psem.at[sl]).start()
            return c
        lax.fori_loop(0, n_pages, body, 0, unroll=4)

    @pl.when(b == 0)
    def _():
        start(0, 0)

    @pl.when(b + 1 < nb)
    def _():
        start(b + 1, 1 - slot)

    pltpu.make_async_copy(ckv_hbm.at[pl.ds(0, n_pages)], cbuf.at[slot], csem.at[slot]).wait()
    pltpu.make_async_copy(kpe_hbm.at[pl.ds(0, n_pages)], pbuf.at[slot], psem.at[slot]).wait()

    q = q_ref[...]
    ql = q[:, 0:C].astype(BF16)
    qp = q[:, C:C + R].astype(BF16)
    ck = cp * page
    n_chunks = n_pages // cp
    for c in range(n_chunks):
        kc = cbuf[slot, c * cp:(c + 1) * cp].reshape(ck, C).astype(BF16)
        pct = jnp.concatenate([pbuf[slot, c * cp + i] for i in range(cp)], axis=-1).astype(BF16)
        kb_ref[c * ck:(c + 1) * ck, :] = kc
        s_ref[:, c * ck:(c + 1) * ck] = _dot_nt(ql, kc) + _dot(qp, pct)
    kn = kn_ref[...]
    s_new = jnp.sum(q * kn, axis=-1, keepdims=True)
    s = s_ref[...]
    m = jnp.maximum(jnp.max(s, axis=-1, keepdims=True), s_new)
    p = jnp.exp(s - m)
    p_new = jnp.exp(s_new - m)
    l = jnp.sum(p, axis=-1, keepdims=True) + p_new
    acc = p_new * kn[:, 0:C]
    pb = p.astype(BF16)
    for c in range(n_chunks):
        acc = acc + _dot(pb[:, c * ck:(c + 1) * ck], kb_ref[c * ck:(c + 1) * ck, :])
    o_ref[...] = acc / l


def _attn_dec(q_s, kn, cache_ckv, kpe_pages, page_table):
    B, H, Wq = q_s.shape
    _, page, C = cache_ckv.shape
    R = kpe_pages.shape[1]
    n_pages = page_table.shape[1]
    P = n_pages * page
    any_spec = pl.BlockSpec(memory_space=pl.ANY)
    grid_spec = pltpu.PrefetchScalarGridSpec(
        num_scalar_prefetch=1, grid=(B,),
        in_specs=[pl.BlockSpec((None, H, Wq), lambda b, pt: (b, 0, 0)),
                  pl.BlockSpec((None, 1, Wq), lambda b, pt: (b, 0, 0)), any_spec, any_spec],
        out_specs=pl.BlockSpec((None, H, C), lambda b, pt: (b, 0, 0)),
        scratch_shapes=[pltpu.VMEM((2, n_pages, page, C), F32), pltpu.VMEM((2, n_pages, R, page), F32),
                        pltpu.VMEM((P, C), BF16), pltpu.VMEM((H, P), F32),
                        pltpu.SemaphoreType.DMA((2,)), pltpu.SemaphoreType.DMA((2,))])
    return pl.pallas_call(
        functools.partial(_attn_dec_kernel, C=C, R=R, n_pages=n_pages, page=page, cp=DEC_CHUNK_PAGES),
        grid_spec=grid_spec, out_shape=jax.ShapeDtypeStruct((B, H, C), F32), name="attn_dec",
        compiler_params=_cparams(1),
    )(page_table.reshape(-1), q_s, kn, cache_ckv, kpe_pages)


def _mla_out_kernel(x_ref, o_ref, mod_ref, wuvt_ref, wo_ref, xo_ref, *, D, H):
    heads = [o_ref[hh] for hh in range(H)]
    xo_ref[...] = _mla_out(heads, x_ref[...], mod_ref[:, 2 * D:3 * D], wuvt_ref, wo_ref)


def _final_kernel(xs_ref, y_ref, pmod_ref, g_ref, o_ref, *, D):
    o_ref[...] = _rms(_resid(xs_ref, y_ref, pmod_ref, D), g_ref[...])


def _rope_table(pos, d, width):
    inv = ROPE_THETA ** (-jnp.arange(0, d, 2, dtype=F32) / d)
    ang = pos.astype(F32)[:, None] * inv[None, :]
    cos, sin = jnp.cos(ang), jnp.sin(ang)
    pad = jnp.zeros((pos.shape[0], width - d), F32)
    return jnp.concatenate([cos, cos, pad], axis=-1), jnp.concatenate([sin, sin, pad], axis=-1)


def _rot_half_cols(w):
    half = w.shape[-1] // 2
    return jnp.concatenate([-w[..., half:], w[..., :half]], axis=-1)


def _pad_lanes(w):
    pad = jnp.zeros(w.shape[:-1] + (LANE - w.shape[-1],), w.dtype)
    return jnp.concatenate([w, pad], axis=-1)


def _ret_tables(L):
    log_g = jnp.log1p(-jnp.exp2(-5.0 - jnp.arange(RET_HEADS, dtype=F32)))
    idx = jnp.arange(L, dtype=F32)
    diff = idx[:, None] - idx[None, :]
    dec = jnp.where(diff[None] >= 0, jnp.exp(jnp.maximum(diff, 0.0)[None] * log_g[:, None, None]), 0.0)
    qd = jnp.exp((idx + 1.0)[:, None] * log_g[None, :])
    kd = jnp.exp((L - 1.0 - idx)[:, None] * log_g[None, :])
    gl = jnp.exp(L * log_g)[None, :]
    return dec, qd, kd, gl


def kernel(x_prompt, x_sample, c_prompt, c_sample, state_ret, cache_ckv, cache_kpe, page_table, w_ada, b_ada, norm_mix_g, norm_ffn_g, ret_w_in, ret_gn_g, ret_w_out, kv_w_ada, kv_b_ada, kv_norm_g, mla_w_dkv, mla_kv_norm_g, mla_w_uk, mla_w_uv, mla_w_dq, mla_q_norm_g, mla_w_uq, mla_w_o, router_w, router_bias, moe_w_gate, moe_w_up, moe_w_down, shared_w_gate, shared_w_up, shared_w_down, final_norm_g):
    B, S, D = x_prompt.shape
    Bs, Ss, _ = x_sample.shape
    assert Ss == 1
    depth = w_ada.shape[0]
    n_a = ret_w_in.shape[0]
    H = MLA_HEADS
    C = mla_w_dkv.shape[1] - QK_ROPE
    R = QK_ROPE
    dn = mla_w_uk.shape[1]
    past_len = page_table.shape[1] * cache_ckv.shape[1]

    gp = _Group(B, S, TM_PROMPT, False)
    gs = _Group(Bs, 1, Bs, True)
    groups = (gp, gs)
    T_all = gp.T + gs.T
    ys_off = gp.T // gs.tm

    n_c = B + Bs
    n_c_pad = -(-n_c // 8) * 8
    c_all = jnp.concatenate([c_prompt, c_sample, jnp.zeros((n_c_pad - n_c, D), F32)], axis=0)
    mod_all = _ada_mod(c_all, w_ada, b_ada)
    kvmod_all = _ada_mod(c_all, kv_w_ada[None], kv_b_ada[None])[0]
    mods = [[g.mod_array(mod_all[l, lo:lo + g.n_seq]) for l in range(depth)]
            for g, lo in zip(groups, (0, B))]
    kvmods = [g.mod_array(kvmod_all[lo:lo + g.n_seq]) for g, lo in zip(groups, (0, B))]

    pos = (jnp.arange(S), past_len + jnp.arange(1))
    ret_rope = [_rope_table(p, D // RET_HEADS, D // RET_HEADS)[0:2] for p in pos]
    ret_rope = [(c[:, :c.shape[1] // 2], s[:, :s.shape[1] // 2]) for c, s in ret_rope]
    mla_rope = [_rope_table(p, R, LANE) for p in pos]
    tabs_p = _ret_tables(RET_CHUNK)
    gl_s = _ret_tables(1)[3]

    row = lambda g: g.reshape(1, -1)
    w_in = ret_w_in.astype(BF16)
    w_out = ret_w_out.astype(BF16)
    wsg, wsu, wsd = shared_w_gate.astype(BF16), shared_w_up.astype(BF16), shared_w_down.astype(BF16)
    wg, wu, wd = moe_w_gate.astype(BF16), moe_w_up.astype(BF16), moe_w_down.astype(BF16)
    rw_t = router_w.T
    rb = router_bias.reshape(-1, 1)
    w_dkv_c = mla_w_dkv[:, :C].astype(BF16)
    w_dkv_p = _pad_lanes(mla_w_dkv[:, C:]).astype(BF16)
    w_dkv_r = _pad_lanes(_rot_half_cols(mla_w_dkv[:, C:])).astype(BF16)
    w_uk = mla_w_uk.astype(BF16)
    w_uvt = jnp.swapaxes(mla_w_uv, 1, 2).astype(BF16)
    wq = []
    for j in range(depth - n_a):
        w3 = mla_w_uq[j].reshape(-1, H, dn + R)
        w_n = w3[:, :, :dn].reshape(-1, H * dn).astype(BF16)
        w_p = _pad_lanes(w3[:, :, dn:]).reshape(-1, H * LANE).astype(BF16)
        w_r = _pad_lanes(_rot_half_cols(w3[:, :, dn:])).reshape(-1, H * LANE).astype(BF16)
        wq.append((mla_w_dq[j].astype(BF16), row(mla_q_norm_g[j]), w_n, w_p, w_r, w_uk))
    w_o = mla_w_o.astype(BF16)

    kpe_pages = jnp.swapaxes(cache_kpe, 1, 2)

    xs = [x_prompt.reshape(gp.T, D), x_sample.reshape(gs.T, D)]
    y = None
    states = [[], []]
    ckv_out, kpe_out, kcat = [None, None], [None, None], [None, None]

    for l in range(depth):
        y_offs = (0, ys_off)
        xmid = [None, None]
        if l < n_a:
            for gi, g in enumerate(groups):
                pm = mods[gi][l - 1] if y is not None else None
                x, q, k, v, gt = _ret_in(g, xs[gi], y, y_offs[gi], pm, mods[gi][l], ret_rope[gi][0], ret_rope[gi][1],
                                         row(norm_mix_g[l]), w_in[l], BF16 if gi == 0 else F32)
                if gi == 0:
                    xmid[gi], st = _ret_chunk(B, S, x, q, k, v, gt, mods[gi][l], tabs_p, row(ret_gn_g[l]), w_out[l])
                else:
                    st, o = _ret_dec(state_ret, l, q, k, v, gl_s)
                    xmid[gi], = _tok_call("ret_post", functools.partial(_ret_post_kernel, D=D, H=RET_HEADS), g,
                                          [(x, 0), (o, 0), (gt, 0)], [mods[gi][l]], [],
                                          [row(ret_gn_g[l]), w_out[l]], [(D, F32)])
                states[gi].append(st)
        else:
            j = l - n_a
            for gi, g in enumerate(groups):
                xsg, yy, pm = xs[gi], y, mods[gi][l - 1]
                if l == n_a:
                    res = _kv_stream(g, xsg, y, y_offs[gi], pm, kvmods[gi], mla_rope[gi][0], mla_rope[gi][1],
                                     [row(kv_norm_g), w_dkv_c, w_dkv_p, w_dkv_r, row(mla_kv_norm_g)], C, R,
                                     BF16 if gi == 0 else F32, gi == 0)
                    xsg, ckv_out[gi], kpe_out[gi], kcat[gi] = res[:4]
                    if gi == 0:
                        vt_p = res[4]
                    yy, pm = None, None
                x, qcat = _mla_q(g, xsg, yy, y_offs[gi], pm, mods[gi][l], mla_rope[gi][0], mla_rope[gi][1],
                                 row(norm_mix_g[l]), wq[j], BF16 if gi == 0 else F32, gi == 0)
                if gi == 0:
                    xmid[gi] = _attn_prompt(B, S, x, qcat, kcat[gi], vt_p, mods[gi][l], w_uvt, w_o[j])
                else:
                    o_lat = _attn_dec(jnp.swapaxes(qcat, 0, 1), kcat[gi][:, None, :], cache_ckv, kpe_pages, page_table)
                    xmid[gi] = pl.pallas_call(
                        functools.partial(_mla_out_kernel, D=D, H=H), grid=(1,),
                        in_specs=[_const_spec(x), pl.BlockSpec((H, Bs, C), lambda i: (0, 0, 0)),
                                  _const_spec(mods[gi][l]), _const_spec(w_uvt), _const_spec(w_o[j])],
                        out_specs=_const_spec(x), out_shape=jax.ShapeDtypeStruct(x.shape, F32),
                        name="mla_out", compiler_params=_cparams(1),
                    )(x, jnp.swapaxes(o_lat, 0, 1), mods[gi][l], w_uvt, w_o[j])
        hx, cls = [], []
        for gi, g in enumerate(groups):
            h, xs[gi], c = _moe_pre(g, xmid[gi], mods[gi][l], row(norm_ffn_g[l]), rw_t, rb, wsg[l], wsu[l], wsd[l])
            hx.append(h)
            cls.append(c)
        y = _moe_routed(jnp.concatenate(hx), jnp.concatenate(cls), l, wg, wu, wd)

    outs = []
    for gi, g in enumerate(groups):
        o, = _tok_call("final_norm", functools.partial(_final_kernel, D=D), g, [(xs[gi], 0), (y, (0, ys_off)[gi])],
                       [mods[gi][depth - 1]], [], [row(final_norm_g)], [(D, F32)])
        outs.append(o)
    return (outs[0].reshape(B, S, D), outs[1].reshape(Bs, 1, D),
            jnp.stack(states[0]), jnp.stack(states[1]),
            ckv_out[0].reshape(B, S, C), kpe_out[0].reshape(B, S, R),
            ckv_out[1].reshape(Bs, 1, C), kpe_out[1].reshape(Bs, 1, R))
```

```python
import functools

import jax
import jax.numpy as jnp
from jax import lax
from jax.experimental import pallas as pl
from jax.experimental.pallas import tpu as pltpu

F32 = jnp.float32
BF16 = jnp.bfloat16

RET_HEADS = 4
MLA_HEADS = 8
QK_ROPE = 64
N_GROUPS = 4
EXPERTS_PER_GROUP = 4
ROPE_THETA = 10000.0
EPS = 1e-6
LOG2_E = 1.4426950408889634

_PAIRS = ((0, 1), (0, 2), (0, 3), (1, 2), (1, 3), (2, 3))
N_CLASSES = N_GROUPS * len(_PAIRS)

LANE = 128
VMEM_LIMIT_BYTES = 56 * 2 ** 20
TM_PROMPT = 256
MOE_TILE = 256
HX_CHUNKS = 16
RET_CHUNK = 256
ATT_TQ = 256
ATT_TK = 256
ATT_CW = 512
DEC_NB = 8
DEC_CHUNK_PAGES = 8


def _dot(a, b):
    return jnp.dot(a, b, preferred_element_type=F32)


def _dot_nt(a, b):
    return lax.dot_general(a, b, (((1,), (1,)), ((), ())), preferred_element_type=F32)


def _dot_tn(a, b):
    return lax.dot_general(a, b, (((0,), (0,)), ((), ())), preferred_element_type=F32)


def _silu(x):
    return x * jax.nn.sigmoid(x)


def _rms(x, g):
    return x * lax.rsqrt(jnp.mean(x * x, axis=-1, keepdims=True) + EPS) * g


def _cparams(n_axes=1):
    return pltpu.CompilerParams(dimension_semantics=("arbitrary",) * n_axes,
                                vmem_limit_bytes=VMEM_LIMIT_BYTES)


class _Group:
    def __init__(self, n_seq, seq_len, tm, per_token_mod):
        self.n_seq, self.seq_len, self.tm = n_seq, seq_len, tm
        self.T = n_seq * seq_len
        self.n_tiles = self.T // tm
        self.per_token_mod = per_token_mod
        self.tiles_per_seq = max(seq_len // tm, 1)

    def tok_spec(self, width, offset=0):
        return pl.BlockSpec((self.tm, width), lambda i: (i + offset, 0))

    def slab_spec(self, chunks, offset=0):
        return pl.BlockSpec((self.tm * chunks, LANE), lambda i: (i + offset, 0))

    def mod_spec(self, width):
        if self.per_token_mod:
            return pl.BlockSpec((self.tm, width), lambda i: (i, 0))
        tps = self.tiles_per_seq
        return pl.BlockSpec((None, 1, width), lambda i: (i // tps, 0, 0))

    def rope_spec(self, width):
        if self.per_token_mod:
            return pl.BlockSpec((1, width), lambda i: (0, 0))
        tps = self.tiles_per_seq
        return pl.BlockSpec((self.tm, width), lambda i: (i % tps, 0))

    def mod_array(self, m):
        return m if self.per_token_mod else m[:, None, :]


def _const_spec(a):
    nd = a.ndim
    return pl.BlockSpec(a.shape, lambda i: (0,) * nd)


def _tok_call(name, body, grp, tok_ins, mod_ins, rope_ins, const_ins, outs):
    in_specs, args = [], []
    for a, off, *slab_chunks in tok_ins:
        spec = grp.slab_spec(slab_chunks[0], off) if slab_chunks else grp.tok_spec(a.shape[-1], off)
        in_specs.append(spec); args.append(a)
    for a in mod_ins:
        in_specs.append(grp.mod_spec(a.shape[-1])); args.append(a)
    for a in rope_ins:
        in_specs.append(grp.rope_spec(a.shape[-1])); args.append(a)
    for a in const_ins:
        in_specs.append(_const_spec(a)); args.append(a)
    out_specs = [grp.tok_spec(width) for width, _ in outs]
    out_shape = [jax.ShapeDtypeStruct((grp.T, width), dtype) for width, dtype in outs]
    return pl.pallas_call(
        body, grid=(grp.n_tiles,), in_specs=in_specs, out_specs=out_specs, out_shape=out_shape,
        name=name, compiler_params=_cparams(1),
    )(*args)


def _mod_kernel(c_ref, w_ref, b_ref, o_ref):
    c = c_ref[...]
    o_ref[...] = _dot(_silu(c).astype(BF16), w_ref[...].astype(BF16)) + b_ref[...]


def _ada_mod(c_all, w, b):
    L, D, N = w.shape
    Bc = c_all.shape[0]
    tn = min(N, 2048)
    return pl.pallas_call(
        _mod_kernel, grid=(L, N // tn),
        in_specs=[pl.BlockSpec((Bc, D), lambda l, j: (0, 0)),
                  pl.BlockSpec((None, D, tn), lambda l, j: (l, 0, j)),
                  pl.BlockSpec((None, 1, tn), lambda l, j: (l, 0, j))],
        out_specs=pl.BlockSpec((None, Bc, tn), lambda l, j: (l, 0, j)),
        out_shape=jax.ShapeDtypeStruct((L, Bc, N), F32),
        name="ada_mod", compiler_params=_cparams(2),
    )(c_all, w, b[:, None, :])


def _slab_load(ref, n_tok, chunks, stride):
    return jnp.concatenate([ref[pl.ds(j, n_tok, stride=stride), :] for j in range(chunks)], axis=-1)


def _slab_store(ref, val, stride):
    n_tok = val.shape[0]
    for j in range(val.shape[1] // LANE):
        ref[pl.ds(j, n_tok, stride=stride), :] = val[:, j * LANE:(j + 1) * LANE]


def _resid(xs_ref, y_ref, pmod_ref, D):
    x = xs_ref[...]
    if y_ref is not None:
        x = x + pmod_ref[:, 5 * D:6 * D] * _slab_load(y_ref, x.shape[0], D // LANE, D // LANE)
    return x


def _ret_in_kernel(*refs, has_y, D, H):
    refs = list(refs)
    xs_ref = refs.pop(0)
    y_ref = refs.pop(0) if has_y else None
    pmod_ref = refs.pop(0) if has_y else None
    mod_ref, cos_ref, sin_ref, g_ref, w_ref = refs[:5]
    outs = refs[5:]
    if has_y:
        xo_ref, outs = outs[0], outs[1:]
    q_ref, k_ref, v_ref, gt_ref = outs
    x = _resid(xs_ref, y_ref, pmod_ref, D)
    if has_y:
        xo_ref[...] = x
    h = _rms(x, g_ref[...]) * (1.0 + mod_ref[:, D:2 * D]) + mod_ref[:, 0:D]
    hb = h.astype(BF16)
    dk = D // H
    half = dk // 2
    cos, sin = cos_ref[...], sin_ref[...]
    for idx, (o_ref, scale) in enumerate(((q_ref, float(dk) ** -0.5), (k_ref, 1.0))):
        t = _dot(hb, w_ref[:, idx * D:(idx + 1) * D])
        for hh in range(H):
            x1 = t[:, hh * dk:hh * dk + half]
            x2 = t[:, hh * dk + half:(hh + 1) * dk]
            o_ref[:, hh * dk:hh * dk + half] = ((x1 * cos - x2 * sin) * scale).astype(o_ref.dtype)
            o_ref[:, hh * dk + half:(hh + 1) * dk] = ((x1 * sin + x2 * cos) * scale).astype(o_ref.dtype)
    v_ref[...] = _dot(hb, w_ref[:, 2 * D:3 * D]).astype(v_ref.dtype)
    gt_ref[...] = _dot(hb, w_ref[:, 3 * D:4 * D])


def _ret_in(grp, xs, y, y_off, pmod, mod, cos, sin, norm_g, w_in, qkv_dtype):
    D = xs.shape[-1]
    has_y = y is not None
    tok = [(xs, 0)] + ([(y, y_off, D // LANE)] if has_y else [])
    mods = ([pmod] if has_y else []) + [mod]
    outs = ([(D, F32)] if has_y else []) + [(D, qkv_dtype)] * 3 + [(D, F32)]
    res = _tok_call("ret_in", functools.partial(_ret_in_kernel, has_y=has_y, D=D, H=RET_HEADS), grp,
                    tok, mods, [cos, sin], [norm_g, w_in], outs)
    if not has_y:
        res = [xs] + list(res)
    return res


def _gn_gate(o, g, gn):
    mu = jnp.mean(o, axis=-1, keepdims=True)
    c = o - mu
    var = jnp.mean(c * c, axis=-1, keepdims=True)
    return _silu(g) * (c * lax.rsqrt(var + EPS) * gn)


def _ret_chunk_kernel(x_ref, q_ref, k_ref, v_ref, gt_ref, mod_ref, dec_ref, qd_ref, kd_ref, gl_ref,
                      gn_ref, wo_ref, xo_ref, s_ref, gat_ref, *, D, H):
    @pl.when(pl.program_id(1) == 0)
    def _():
        s_ref[...] = jnp.zeros_like(s_ref)

    dk = D // H
    for h in range(H):
        hs = slice(h * dk, (h + 1) * dk)
        qh, kh, vh = q_ref[:, hs], k_ref[:, hs], v_ref[:, hs]
        S = s_ref[h]
        scores = _dot_nt(qh, kh) * dec_ref[h]
        inner = _dot(scores.astype(BF16), vh)
        cross = _dot((qh.astype(F32) * qd_ref[:, h:h + 1]).astype(BF16), S.astype(BF16))
        kdec = (kh.astype(F32) * kd_ref[:, h:h + 1]).astype(BF16)
        s_ref[h] = gl_ref[:, h:h + 1] * S + _dot_tn(kdec, vh)
        gat_ref[:, hs] = _gn_gate(inner + cross, gt_ref[:, hs], gn_ref[:, hs]).astype(BF16)
    y = _dot(gat_ref[...], wo_ref[...])
    xo_ref[...] = x_ref[...] + mod_ref[:, 2 * D:3 * D] * y


def _ret_chunk(B, S, x, q, k, v, gt, mod, tabs, gn_g, w_out):
    D = x.shape[-1]
    H = RET_HEADS
    L = RET_CHUNK
    nc = S // L
    dk = D // H
    dec, qd, kd, gl = tabs
    tok = lambda w: pl.BlockSpec((L, w), lambda b, c: (b * nc + c, 0))
    cst = lambda a: pl.BlockSpec(a.shape, lambda b, c: (0,) * a.ndim)
    return pl.pallas_call(
        functools.partial(_ret_chunk_kernel, D=D, H=H), grid=(B, nc),
        in_specs=[tok(D)] * 5 + [pl.BlockSpec((None, 1, mod.shape[-1]), lambda b, c: (b, 0, 0)),
                                 cst(dec), cst(qd), cst(kd), cst(gl), cst(gn_g), cst(w_out)],
        out_specs=[tok(D), pl.BlockSpec((None, H, dk, dk), lambda b, c: (b, 0, 0, 0))],
        out_shape=[jax.ShapeDtypeStruct(x.shape, F32), jax.ShapeDtypeStruct((B, H, dk, dk), F32)],
        scratch_shapes=[pltpu.VMEM((L, D), BF16)],
        name="ret_chunk", compiler_params=_cparams(2),
    )(x, q, k, v, gt, mod, dec, qd, kd, gl, gn_g, w_out)


def _ret_dec_kernel(st_ref, q_ref, k_ref, v_ref, gl_ref, so_ref, o_ref, *, D, H, nb):
    dk = D // H
    for h in range(H):
        hs = slice(h * dk, (h + 1) * dk)
        gh = gl_ref[:, h:h + 1]
        qh, kh, vh = q_ref[:, hs], k_ref[:, hs], v_ref[:, hs]
        inner = jnp.sum(qh * kh, axis=-1, keepdims=True) * vh
        qT = (qh * gh).T
        kT = kh.T
        rows = []
        for j in range(nb):
            S = st_ref[j, h]
            so_ref[j, h] = gh * S + kT[:, j:j + 1] * vh[j:j + 1, :]
            rows.append(jnp.sum(qT[:, j:j + 1] * S, axis=0, keepdims=True))
        o_ref[:, hs] = inner + jnp.concatenate(rows, axis=0)


def _ret_dec(state, layer, q, k, v, gl):
    _, B, H, dk, dv = state.shape
    D = q.shape[-1]
    nb = DEC_NB
    row = pl.BlockSpec((nb, D), lambda i: (i, 0))
    return pl.pallas_call(
        functools.partial(_ret_dec_kernel, D=D, H=H, nb=nb), grid=(B // nb,),
        in_specs=[pl.BlockSpec((None, nb, H, dk, dv), lambda i: (layer, i, 0, 0, 0)), row, row, row,
                  pl.BlockSpec(gl.shape, lambda i: (0, 0))],
        out_specs=[pl.BlockSpec((nb, H, dk, dv), lambda i: (i, 0, 0, 0)), row],
        out_shape=[jax.ShapeDtypeStruct((B, H, dk, dv), F32), jax.ShapeDtypeStruct((B, D), F32)],
        name="ret_dec", compiler_params=_cparams(1),
    )(state, q, k, v, gl)


def _ret_post_kernel(x_ref, o_ref, gt_ref, mod_ref, gn_ref, wo_ref, xo_ref, *, D, H):
    dk = D // H
    parts = []
    for h in range(H):
        hs = slice(h * dk, (h + 1) * dk)
        parts.append(_gn_gate(o_ref[:, hs], gt_ref[:, hs], gn_ref[:, hs]).astype(BF16))
    y = _dot(jnp.concatenate(parts, axis=-1), wo_ref[...])
    xo_ref[...] = x_ref[...] + mod_ref[:, 2 * D:3 * D] * y


def _route(logits_t, bias):
    scores = jax.nn.sigmoid(logits_t)
    sel = scores + bias
    n_e = EXPERTS_PER_GROUP
    row = lambda a, e: a[e:e + 1, :]
    gscore = []
    for g in range(N_GROUPS):
        best2 = None
        for a, b in _PAIRS:
            s = row(sel, n_e * g + a) + row(sel, n_e * g + b)
            best2 = s if best2 is None else jnp.maximum(best2, s)
        gscore.append(best2)
    best, bestv = jnp.zeros_like(gscore[0], dtype=jnp.int32), gscore[0]
    for g in range(1, N_GROUPS):
        upd = gscore[g] > bestv
        best = jnp.where(upd, g, best)
        bestv = jnp.where(upd, gscore[g], bestv)

    def pick(a, j):
        out = row(a, j)
        for g in range(1, N_GROUPS):
            out = jnp.where(best == g, row(a, n_e * g + j), out)
        return out

    v = [pick(sel, j) for j in range(n_e)]
    sc = [pick(scores, j) for j in range(n_e)]
    i1, v1 = jnp.zeros_like(best), v[0]
    for j in range(1, n_e):
        upd = v[j] > v1
        i1 = jnp.where(upd, j, i1)
        v1 = jnp.where(upd, v[j], v1)
    i2, v2 = None, None
    for j in range(n_e):
        ok = i1 != j
        if i2 is None:
            i2 = jnp.where(ok, j, n_e)
            v2 = jnp.where(ok, v[j], -jnp.inf)
        else:
            upd = ok & (v[j] > v2)
            i2 = jnp.where(upd, j, i2)
            v2 = jnp.where(upd, v[j], v2)
    s1 = sc[0]
    s2 = sc[0]
    for j in range(1, n_e):
        s1 = jnp.where(i1 == j, sc[j], s1)
        s2 = jnp.where(i2 == j, sc[j], s2)
    tot = s1 + s2
    w1, w2 = s1 / tot, s2 / tot
    lo = jnp.minimum(i1, i2)
    hi = jnp.maximum(i1, i2)
    base = jnp.where(lo == 0, 0, jnp.where(lo == 1, 3, 5))
    cls = best * len(_PAIRS) + base + hi - lo - 1
    first_is_lo = i1 < i2
    return cls, jnp.where(first_is_lo, w1, w2), jnp.where(first_is_lo, w2, w1)


def _moe_pre_kernel(x_ref, mod_ref, g_ref, rwt_ref, rb_ref, wsg_ref, wsu_ref, wsd_ref, *rest, D):
    hx_ref, xs_ref, cls_ref = rest[-3:]
    x = x_ref[...]
    h = _rms(x, g_ref[...]) * (1.0 + mod_ref[:, 4 * D:5 * D]) + mod_ref[:, 3 * D:4 * D]
    hb = h.astype(BF16)
    h_lo = (h - hb.astype(F32)).astype(BF16)
    rw = rwt_ref[...]
    rw_hi = rw.astype(BF16)
    rw_lo = (rw - rw_hi.astype(F32)).astype(BF16)
    logits_t = _dot_nt(rw_hi, hb) + (_dot_nt(rw_lo, hb) + _dot_nt(rw_hi, h_lo))
    cls, wa, wb = _route(logits_t, rb_ref[...])
    tm = x.shape[0]
    cls_ref[0] = cls
    rowi = lax.broadcasted_iota(jnp.int32, (LANE, tm), 0)
    extra = jnp.where(rowi == 0, wa, jnp.where(rowi == 1, wb, 0.0)).T
    _slab_store(hx_ref, h, HX_CHUNKS)
    hx_ref[pl.ds(D // LANE, tm, stride=HX_CHUNKS), :] = extra
    for j in range(D // LANE + 1, HX_CHUNKS):
        hx_ref[pl.ds(j, tm, stride=HX_CHUNKS), :] = jnp.zeros((tm, LANE), F32)
    hid =_silu(_dot(hb, wsg_ref[...])) * _dot(hb, wsu_ref[...])
    shared = _dot(hid.astype(BF16), wsd_ref[...])
    xs_ref[...] = x + mod_ref[:, 5 * D:6 * D] * shared


def _moe_pre(grp, x, mod, norm_g, rw_t, rb, wsg, wsu, wsd):
    D = x.shape[-1]
    in_specs = [grp.tok_spec(D), grp.mod_spec(mod.shape[-1])] + [_const_spec(a) for a in (norm_g, rw_t, rb, wsg, wsu, wsd)]
    hx, xs, cls = pl.pallas_call(
        functools.partial(_moe_pre_kernel, D=D), grid=(grp.n_tiles,), in_specs=in_specs,
        out_specs=[grp.slab_spec(HX_CHUNKS), grp.tok_spec(D), pl.BlockSpec((1, 1, grp.tm), lambda i: (i, 0, 0))],
        out_shape=[jax.ShapeDtypeStruct((grp.T * HX_CHUNKS, LANE), F32), jax.ShapeDtypeStruct((grp.T, D), F32),
                   jax.ShapeDtypeStruct((grp.n_tiles, 1, grp.tm), jnp.int32)],
        name="moe_pre", compiler_params=_cparams(1),
    )(x, mod, norm_g, rw_t, rb, wsg, wsu, wsd)
    return hx, xs, cls.reshape(-1)


def _moe_kernel(pos_ref, te1_ref, te2_ref, nv_ref, dump_ref, nused_ref, hx_hbm, wg1, wu1, wd1, wg2, wu2, wd2,
                y_hbm, src_ref, hbuf, obuf, gsem, ssem, *, T_all, n_rows, tm, D):
    i = pl.program_id(0)
    nused = nused_ref[0]
    slot = i % 2

    ci, co = HX_CHUNKS, D // LANE

    def gather_row(tile, sl, r):
        t = src_ref[tile * tm + r]
        pltpu.make_async_copy(hx_hbm.at[pl.ds(pl.multiple_of(t * ci, ci), ci)],
                              hbuf.at[sl, pl.ds(pl.multiple_of(r * ci, ci), ci)], gsem.at[sl]).start()

    def scatter_row(tile, sl, r):
        nv = nv_ref[tile]
        t = jnp.where(r < nv, src_ref[tile * tm + r], dump_ref[tile] + (r - nv))
        pltpu.make_async_copy(obuf.at[sl, pl.ds(pl.multiple_of(r * co, co), co)],
                              y_hbm.at[pl.ds(pl.multiple_of(t * co, co), co)], ssem.at[sl]).start()

    def rolled(fn, tile, sl):
        def step(r, c):
            fn(tile, sl, r)
            return c
        lax.fori_loop(0, tm, step, 0, unroll=8)

    def unrolled(fn, tile, sl):
        for r in range(tm):
            fn(tile, sl, r)

    def wait_gather(sl):
        pltpu.make_async_copy(hx_hbm.at[pl.ds(0, tm * ci)], hbuf.at[sl], gsem.at[sl]).wait()

    def wait_scatter(sl):
        pltpu.make_async_copy(obuf.at[sl], y_hbm.at[pl.ds(0, tm * co)], ssem.at[sl]).wait()

    def compute(sl):
        hb = _slab_load(hbuf.at[sl], tm, co, ci).astype(BF16)
        gates = hbuf[sl, pl.ds(co, tm, stride=ci), :]
        wa, wb = gates[:, 0:1], gates[:, 1:2]
        hid_a = (_silu(_dot(hb, wg1[...])) * _dot(hb, wu1[...]) * wa).astype(BF16)
        hid_b = (_silu(_dot(hb, wg2[...])) * _dot(hb, wu2[...]) * wb).astype(BF16)
        _slab_store(obuf.at[sl], _dot(hid_a, wd1[...]) + _dot(hid_b, wd2[...]), co)

    @pl.when(i == 0)
    def _():
        def init(r, c):
            src_ref[r] = 0
            return c
        lax.fori_loop(0, n_rows, init, 0, unroll=8)

        def fill(t, c):
            src_ref[pos_ref[t]] = t
            return c
        lax.fori_loop(0, T_all, fill, 0, unroll=8)
        rolled(gather_row, 0, 0)

    active = i < nused
    steady = jnp.logical_and(i >= 1, i + 1 < nused)

    @pl.when(active)
    def _():
        wait_gather(slot)

        @pl.when(i >= 2)
        def _():
            wait_scatter(slot)

    @pl.when(steady)
    def _():
        compute(slot)
        unrolled(scatter_row, i - 1, 1 - slot)
        unrolled(gather_row, i + 1, 1 - slot)

    @pl.when(jnp.logical_and(active, jnp.logical_not(steady)))
    def _():
        @pl.when(i + 1 < nused)
        def _():
            rolled(gather_row, i + 1, 1 - slot)
        compute(slot)

        @pl.when(i >= 1)
        def _():
            rolled(scatter_row, i - 1, 1 - slot)

        @pl.when(i == nused - 1)
        def _():
            rolled(scatter_row, i, slot)
            wait_scatter(slot)

            @pl.when(i >= 1)
            def _():
                wait_scatter(1 - slot)

    @pl.when(jnp.logical_not(active))
    def _():
        obuf[slot] = jnp.zeros(obuf.shape[1:], F32)
        rolled(scatter_row, i, slot)
        wait_scatter(slot)


def _moe_routed(hx, cls, layer, wg, wu, wd):
    T_all = hx.shape[0] // HX_CHUNKS
    D, Fe = wg.shape[-2:]
    tm = MOE_TILE
    n_tiles = (T_all + N_CLASSES * (tm - 1)) // tm
    n_rows = n_tiles * tm
    onehot = (cls[:, None] == jnp.arange(N_CLASSES, dtype=jnp.int32)[None, :]).astype(jnp.int32)
    counts = jnp.sum(onehot, axis=0)
    rank = jnp.sum((jnp.cumsum(onehot, axis=0) - onehot) * onehot, axis=1)
    ntile_c = (counts + tm - 1) // tm
    tile_end = jnp.cumsum(ntile_c)
    tile_off = tile_end - ntile_c
    pos = (jnp.sum(onehot * tile_off[None, :], axis=1) * tm + rank).astype(jnp.int32)
    nused = tile_end[-1]
    tiles = jnp.arange(n_tiles, dtype=jnp.int32)
    tid = jnp.minimum(tiles, nused - 1)
    tile_cls = jnp.sum((tid[:, None] >= tile_end[None, :]).astype(jnp.int32), axis=1)
    nvalid = jnp.clip(counts[tile_cls] - (tiles - tile_off[tile_cls]) * tm, 0, tm)
    nvalid = jnp.where(tiles < nused, nvalid, 0).astype(jnp.int32)
    dump = (T_all + tiles * tm - (jnp.cumsum(nvalid) - nvalid)).astype(jnp.int32)
    pair_lo = jnp.array([a for a, _ in _PAIRS], jnp.int32)
    pair_hi = jnp.array([b for _, b in _PAIRS], jnp.int32)
    grp_id, pair_id = tile_cls // len(_PAIRS), tile_cls % len(_PAIRS)
    te1 = (grp_id * EXPERTS_PER_GROUP + pair_lo[pair_id]).astype(jnp.int32)
    te2 = (grp_id * EXPERTS_PER_GROUP + pair_hi[pair_id]).astype(jnp.int32)

    w1 = lambda shape: pl.BlockSpec((None, None) + shape, lambda i, pos, te1, *_: (layer, te1[i], 0, 0))
    w2 = lambda shape: pl.BlockSpec((None, None) + shape, lambda i, pos, te1, te2, *_: (layer, te2[i], 0, 0))
    any_spec = pl.BlockSpec(memory_space=pl.ANY)
    grid_spec = pltpu.PrefetchScalarGridSpec(
        num_scalar_prefetch=6, grid=(n_tiles,),
        in_specs=[any_spec, w1((D, Fe)), w1((D, Fe)), w1((Fe, D)), w2((D, Fe)), w2((D, Fe)), w2((Fe, D))],
        out_specs=any_spec,
        scratch_shapes=[pltpu.SMEM((n_rows,), jnp.int32),
                        pltpu.VMEM((2, tm * HX_CHUNKS, LANE), F32), pltpu.VMEM((2, tm * (D // LANE), LANE), F32),
                        pltpu.SemaphoreType.DMA((2,)), pltpu.SemaphoreType.DMA((2,))])
    return pl.pallas_call(
        functools.partial(_moe_kernel, T_all=T_all, n_rows=n_rows, tm=tm, D=D),
        grid_spec=grid_spec, out_shape=jax.ShapeDtypeStruct((n_rows * (D // LANE), LANE), F32),
        name="moe_routed", compiler_params=_cparams(1),
    )(pos, te1, te2, nvalid, dump, nused.reshape(1).astype(jnp.int32), hx, wg, wu, wd, wg, wu, wd)


def _kv_kernel(xs_ref, y_ref, pmod_ref, kvmod_ref, cos_ref, sin_ref, g_ref, wc_ref, wp_ref, wr_ref, lg_ref,
               xo_ref, ckv_ref, kpe_ref, kcat_ref, *maybe_vt_ref, D, C, R):
    x = _resid(xs_ref, y_ref, pmod_ref, D)
    xo_ref[...] = x
    hn = (_rms(x, g_ref[...]) * (1.0 + kvmod_ref[:, D:2 * D]) + kvmod_ref[:, 0:D]).astype(BF16)
    ckv = _rms(_dot(hn, wc_ref[...]), lg_ref[...])
    kpe = _dot(hn, wp_ref[...]) * cos_ref[...] + _dot(hn, wr_ref[...]) * sin_ref[...]
    ckv_ref[...] = ckv
    kpe_ref[...] = kpe[:, 0:R]
    kcat_ref[:, 0:C] = ckv.astype(kcat_ref.dtype)
    kcat_ref[:, C:C + LANE] = kpe.astype(kcat_ref.dtype)
    if maybe_vt_ref:
        maybe_vt_ref[0][...] = ckv.T.astype(maybe_vt_ref[0].dtype)


def _kv_stream(grp, xs, y, y_off, pmod, kvmod, cos, sin, consts, C, R, kcat_dtype, with_vt):
    D = xs.shape[-1]
    tok_ins = [xs, y]
    in_specs = [grp.tok_spec(D), grp.slab_spec(D // LANE, y_off), grp.mod_spec(pmod.shape[-1]),
                grp.mod_spec(kvmod.shape[-1]), grp.rope_spec(LANE), grp.rope_spec(LANE)]
    in_specs += [_const_spec(a) for a in consts]
    widths = [(D, F32), (C, F32), (R, F32), (C + LANE, kcat_dtype)]
    out_specs = [grp.tok_spec(w) for w, _ in widths]
    out_shape = [jax.ShapeDtypeStruct((grp.T, w), dt) for w, dt in widths]
    if with_vt:
        tps = grp.tiles_per_seq
        out_specs.append(pl.BlockSpec((None, C, grp.tm), lambda i: (i // tps, 0, i % tps)))
        out_shape.append(jax.ShapeDtypeStruct((grp.n_seq, C, grp.seq_len), BF16))
    return pl.pallas_call(
        functools.partial(_kv_kernel, D=D, C=C, R=R), grid=(grp.n_tiles,), in_specs=in_specs,
        out_specs=out_specs, out_shape=out_shape, name="kv_stream", compiler_params=_cparams(1),
    )(*tok_ins, pmod, kvmod, cos, sin, *consts)


def _mla_q_kernel(*refs, has_y, transposed, D, C, H, scale):
    refs = list(refs)
    xs_ref = refs.pop(0)
    y_ref = refs.pop(0) if has_y else None
    pmod_ref = refs.pop(0) if has_y else None
    mod_ref, cos_ref, sin_ref, g_ref, wdq_ref, qg_ref, wn_ref, wp_ref, wr_ref, wuk_ref = refs[:10]
    outs = refs[10:]
    if has_y:
        xo_ref, outs = outs[0], outs[1:]
    qcat_ref, = outs
    x = _resid(xs_ref, y_ref, pmod_ref, D)
    if has_y:
        xo_ref[...] = x
    h = (_rms(x, g_ref[...]) * (1.0 + mod_ref[:, D:2 * D]) + mod_ref[:, 0:D]).astype(BF16)
    cq = _rms(_dot(h, wdq_ref[...]), qg_ref[...]).astype(BF16)
    q_nope = _dot(cq, wn_ref[...]).astype(BF16)
    cos, sin = cos_ref[...], sin_ref[...]
    dn = q_nope.shape[-1] // H
    for hh in range(H):
        ls = slice(hh * LANE, (hh + 1) * LANE)
        q_lat = _dot(q_nope[:, hh * dn:(hh + 1) * dn], wuk_ref[hh])
        q_pe = _dot(cq, wp_ref[:, ls]) * cos + _dot(cq, wr_ref[:, ls]) * sin
        if transposed:
            tm = q_lat.shape[0]
            qcat_ref[0:C, hh * tm:(hh + 1) * tm] = (q_lat * scale).T.astype(qcat_ref.dtype)
            qcat_ref[C:C + LANE, hh * tm:(hh + 1) * tm] = (q_pe * scale).T.astype(qcat_ref.dtype)
        else:
            qcat_ref[hh, :, 0:C] = (q_lat * scale).astype(qcat_ref.dtype)
            qcat_ref[hh, :, C:C + LANE] = (q_pe * scale).astype(qcat_ref.dtype)


def _mla_q(grp, xs, y, y_off, pmod, mod, cos, sin, norm_g, wq, out_dtype, transposed):
    D = xs.shape[-1]
    w_dq, qg, w_n, w_p, w_r, w_uk = wq
    C = w_uk.shape[-1]
    H = MLA_HEADS
    has_y = y is not None
    scale = float(w_uk.shape[1] + QK_ROPE) ** -0.5
    if transposed:
        scale *= LOG2_E
    in_specs = [grp.tok_spec(D)] + ([grp.slab_spec(D // LANE, y_off), grp.mod_spec(pmod.shape[-1])] if has_y else [])
    args = [xs] + ([y, pmod] if has_y else [])
    in_specs += [grp.mod_spec(mod.shape[-1]), grp.rope_spec(LANE), grp.rope_spec(LANE)]
    args += [mod, cos, sin]
    for a in (norm_g, w_dq, qg, w_n, w_p, w_r, w_uk):
        in_specs.append(_const_spec(a)); args.append(a)
    if transposed:
        q_spec = pl.BlockSpec((None, C + LANE, H * grp.tm), lambda i: (i, 0, 0))
        q_shape = jax.ShapeDtypeStruct((grp.n_tiles, C + LANE, H * grp.tm), out_dtype)
    else:
        q_spec = pl.BlockSpec((H, grp.tm, C + LANE), lambda i: (0, i, 0))
        q_shape = jax.ShapeDtypeStruct((H, grp.T, C + LANE), out_dtype)
    out_specs = ([grp.tok_spec(D)] if has_y else []) + [q_spec]
    out_shape = ([jax.ShapeDtypeStruct((grp.T, D), F32)] if has_y else []) + [q_shape]
    res = pl.pallas_call(
        functools.partial(_mla_q_kernel, has_y=has_y, transposed=transposed, D=D, C=C, H=H, scale=scale),
        grid=(grp.n_tiles,),
        in_specs=in_specs, out_specs=out_specs, out_shape=out_shape, name="mla_q", compiler_params=_cparams(1),
    )(*args)
    return (res[0], res[1]) if has_y else (xs, res[0])


def _mla_out(o_lat_heads, x, g1, wuvt_ref, wo_ref):
    parts = [_dot(o.astype(BF16), wuvt_ref[hh]).astype(BF16) for hh, o in enumerate(o_lat_heads)]
    return x + g1 * _dot(jnp.concatenate(parts, axis=-1), wo_ref[...])


def _attn_kernel(x_ref, qt_ref, k_ref, vt_ref, mod_ref, wuvt_ref, wo_ref, xo_ref, m_ref, l_ref, acc_ref, s0_ref,
                 *, D, C, H, tq, tk, cw):
    qi = pl.program_id(1)
    m_ref[...] = jnp.full(m_ref.shape, -jnp.inf, F32)
    l_ref[...] = jnp.zeros(l_ref.shape, F32)
    acc_ref[...] = jnp.zeros(acc_ref.shape, F32)
    n_col = H * tq

    def keys(j):
        return k_ref[pl.ds(pl.multiple_of(j * tk, tk), tk), :]

    s0_ref[...] = _dot(keys(0), qt_ref[:, 0:cw])

    def block(j, masked):
        kb = keys(j)
        vt = vt_ref[:, pl.ds(pl.multiple_of(j * tk, tk), tk)]
        if masked:
            kpos = lax.broadcasted_iota(jnp.int32, (tk, cw), 0)
            qpos = lax.broadcasted_iota(jnp.int32, (tk, cw), 1) & (tq - 1)
            keep = kpos <= qpos
        s_next = s0_ref[...]
        for c0 in range(0, n_col, cw):
            cs = slice(c0, c0 + cw)
            s = s_next
            if c0 + cw < n_col:
                s_next = _dot(kb, qt_ref[:, c0 + cw:c0 + 2 * cw])
            elif not masked:
                s0_ref[...] = _dot(keys(j + 1), qt_ref[:, 0:cw])
            if masked:
                s = jnp.where(keep, s, -1e30)
            m_old = m_ref[:, cs]
            m_new = jnp.maximum(m_old, jnp.max(s, axis=0, keepdims=True))
            alpha = jnp.exp2(m_old - m_new)
            p = jnp.exp2(s - m_new)
            l_ref[:, cs] = alpha * l_ref[:, cs] + jnp.sum(p, axis=0, keepdims=True)
            acc_ref[:, cs] = alpha * acc_ref[:, cs] + _dot(vt, p.astype(BF16))
            m_ref[:, cs] = m_new

    def body(j, c):
        block(j, False)
        return c
    lax.fori_loop(0, qi, body, 0)
    block(qi, True)
    parts = []
    for hh in range(H):
        cs = slice(hh * tq, (hh + 1) * tq)
        o_t = (acc_ref[:, cs] / l_ref[:, cs]).astype(BF16)
        parts.append(_dot_tn(o_t, wuvt_ref[hh]).astype(BF16))
    y = _dot(jnp.concatenate(parts, axis=-1), wo_ref[...])
    xo_ref[...] = x_ref[...] + mod_ref[:, 2 * D:3 * D] * y


def _attn_prompt(B, S, x, qt, kcat, vt, mod, w_uvt, w_o):
    D = x.shape[-1]
    H = MLA_HEADS
    _, Wq, n_col = qt.shape
    C = w_uvt.shape[1]
    tq, tk = ATT_TQ, ATT_TK
    assert tq == tk and tq & (tq - 1) == 0 and n_col == H * tq
    nq = S // tq
    cst = lambda a: pl.BlockSpec(a.shape, lambda b, i: (0,) * a.ndim)
    return pl.pallas_call(
        functools.partial(_attn_kernel, D=D, C=C, H=H, tq=tq, tk=tk, cw=ATT_CW), grid=(B, nq),
        in_specs=[pl.BlockSpec((tq, D), lambda b, i: (b * nq + i, 0)),
                  pl.BlockSpec((None, Wq, n_col), lambda b, i: (b * nq + i, 0, 0)),
                  pl.BlockSpec((S, Wq), lambda b, i: (b, 0)),
                  pl.BlockSpec((None, C, S), lambda b, i: (b, 0, 0)),
                  pl.BlockSpec((None, 1, mod.shape[-1]), lambda b, i: (b, 0, 0)), cst(w_uvt), cst(w_o)],
        out_specs=pl.BlockSpec((tq, D), lambda b, i: (b * nq + i, 0)),
        out_shape=jax.ShapeDtypeStruct(x.shape, F32),
        scratch_shapes=[pltpu.VMEM((1, n_col), F32), pltpu.VMEM((1, n_col), F32), pltpu.VMEM((C, n_col), F32),
                        pltpu.VMEM((tk, ATT_CW), F32)],
        name="attn_prompt", compiler_params=_cparams(2),
    )(x, qt, kcat, vt, mod, w_uvt, w_o)


def _attn_dec_kernel(pt_ref, q_ref, kn_ref, ckv_hbm, kpe_hbm, o_ref, cbuf, pbuf, kb_ref, s_ref, csem, psem,
                     *, C, R, n_pages, page, cp):
    b = pl.program_id(0)
    nb = pl.num_programs(0)
    slot = b % 2

    def start(seq, sl):
        def body(p, c):
            pg = pt_ref[seq * n_pages + p]
            pltpu.make_async_copy(ckv_hbm.at[pl.ds(pg, 1)], cbuf.at[sl, pl.ds(p, 1)], csem.at[sl]).start()
            pltpu.make_async_copy(kpe_hbm.at[pl.ds(pg, 1)], pbuf.at[sl, pl.ds(p, 1)], psem.at[sl]).start()
            return c
        lax.fori_loop(0, n_pages, body, 0, unroll=4)

    @pl.when(b == 0)
    def _():
        start(0, 0)

    @pl.when(b + 1 < nb)
    def _():
        start(b + 1, 1 - slot)

    pltpu.make_async_copy(ckv_hbm.at[pl.ds(0, n_pages)], cbuf.at[slot], csem.at[slot]).wait()
    pltpu.make_async_copy(kpe_hbm.at[pl.ds(0, n_pages)], pbuf.at[slot], psem.at[slot]).wait()

    q = q_ref[...]
    ql = q[:, 0:C].astype(BF16)
    qp = q[:, C:C + R].astype(BF16)
    ck = cp * page
    n_chunks = n_pages // cp
    for c in range(n_chunks):
        kc = cbuf[slot, c * cp:(c + 1) * cp].reshape(ck, C).astype(BF16)
        pct = jnp.concatenate([pbuf[slot, c * cp + i] for i in range(cp)], axis=-1).astype(BF16)
        kb_ref[c * ck:(c + 1) * ck, :] = kc
        s_ref[:, c * ck:(c + 1) * ck] = _dot_nt(ql, kc) + _dot(qp, pct)
    kn = kn_ref[...]
    s_new = jnp.sum(q * kn, axis=-1, keepdims=True)
    s = s_ref[...]
    m = jnp.maximum(jnp.max(s, axis=-1, keepdims=True), s_new)
    p = jnp.exp(s - m)
    p_new = jnp.exp(s_new - m)
    l = jnp.sum(p, axis=-1, keepdims=True) + p_new
    acc = p_new * kn[:, 0:C]
    pb = p.astype(BF16)
    for c in range(n_chunks):
        acc = acc + _dot(pb[:, c * ck:(c + 1) * ck], kb_ref[c * ck:(c + 1) * ck, :])
    o_ref[...] = acc / l


def _attn_dec(q_s, kn, cache_ckv, kpe_pages, page_table):
    B, H, Wq = q_s.shape
    _, page, C = cache_ckv.shape
    R = kpe_pages.shape[1]
    n_pages = page_table.shape[1]
    P = n_pages * page
    any_spec = pl.BlockSpec(memory_space=pl.ANY)
    grid_spec = pltpu.PrefetchScalarGridSpec(
        num_scalar_prefetch=1, grid=(B,),
        in_specs=[pl.BlockSpec((None, H, Wq), lambda b, pt: (b, 0, 0)),
                  pl.BlockSpec((None, 1, Wq), lambda b, pt: (b, 0, 0)), any_spec, any_spec],
        out_specs=pl.BlockSpec((None, H, C), lambda b, pt: (b, 0, 0)),
        scratch_shapes=[pltpu.VMEM((2, n_pages, page, C), F32), pltpu.VMEM((2, n_pages, R, page), F32),
                        pltpu.VMEM((P, C), BF16), pltpu.VMEM((H, P), F32),
                        pltpu.SemaphoreType.DMA((2,)), pltpu.SemaphoreType.DMA((2,))])
    return pl.pallas_call(
        functools.partial(_attn_dec_kernel, C=C, R=R, n_pages=n_pages, page=page, cp=DEC_CHUNK_PAGES),
        grid_spec=grid_spec, out_shape=jax.ShapeDtypeStruct((B, H, C), F32), name="attn_dec",
        compiler_params=_cparams(1),
    )(page_table.reshape(-1), q_s, kn, cache_ckv, kpe_pages)


def _mla_out_kernel(x_ref, o_ref, mod_ref, wuvt_ref, wo_ref, xo_ref, *, D, H):
    heads = [o_ref[hh] for hh in range(H)]
    xo_ref[...] = _mla_out(heads, x_ref[...], mod_ref[:, 2 * D:3 * D], wuvt_ref, wo_ref)


def _final_kernel(xs_ref, y_ref, pmod_ref, g_ref, o_ref, *, D):
    o_ref[...] = _rms(_resid(xs_ref, y_ref, pmod_ref, D), g_ref[...])


def _rope_table(pos, d, width):
    inv = ROPE_THETA ** (-jnp.arange(0, d, 2, dtype=F32) / d)
    ang = pos.astype(F32)[:, None] * inv[None, :]
    cos, sin = jnp.cos(ang), jnp.sin(ang)
    pad = jnp.zeros((pos.shape[0], width - d), F32)
    return jnp.concatenate([cos, cos, pad], axis=-1), jnp.concatenate([sin, sin, pad], axis=-1)


def _rot_half_cols(w):
    half = w.shape[-1] // 2
    return jnp.concatenate([-w[..., half:], w[..., :half]], axis=-1)


def _pad_lanes(w):
    pad = jnp.zeros(w.shape[:-1] + (LANE - w.shape[-1],), w.dtype)
    return jnp.concatenate([w, pad], axis=-1)


def _ret_tables(L):
    log_g = jnp.log1p(-jnp.exp2(-5.0 - jnp.arange(RET_HEADS, dtype=F32)))
    idx = jnp.arange(L, dtype=F32)
    diff = idx[:, None] - idx[None, :]
    dec = jnp.where(diff[None] >= 0, jnp.exp(jnp.maximum(diff, 0.0)[None] * log_g[:, None, None]), 0.0)
    qd = jnp.exp((idx + 1.0)[:, None] * log_g[None, :])
    kd = jnp.exp((L - 1.0 - idx)[:, None] * log_g[None, :])
    gl = jnp.exp(L * log_g)[None, :]
    return dec, qd, kd, gl


def kernel(x_prompt, x_sample, c_prompt, c_sample, state_ret, cache_ckv, cache_kpe, page_table, w_ada, b_ada, norm_mix_g, norm_ffn_g, ret_w_in, ret_gn_g, ret_w_out, kv_w_ada, kv_b_ada, kv_norm_g, mla_w_dkv, mla_kv_norm_g, mla_w_uk, mla_w_uv, mla_w_dq, mla_q_norm_g, mla_w_uq, mla_w_o, router_w, router_bias, moe_w_gate, moe_w_up, moe_w_down, shared_w_gate, shared_w_up, shared_w_down, final_norm_g):
    B, S, D = x_prompt.shape
    Bs, Ss, _ = x_sample.shape
    assert Ss == 1
    depth = w_ada.shape[0]
    n_a = ret_w_in.shape[0]
    H = MLA_HEADS
    C = mla_w_dkv.shape[1] - QK_ROPE
    R = QK_ROPE
    dn = mla_w_uk.shape[1]
    past_len = page_table.shape[1] * cache_ckv.shape[1]

    gp = _Group(B, S, TM_PROMPT, False)
    gs = _Group(Bs, 1, Bs, True)
    groups = (gp, gs)
    T_all = gp.T + gs.T
    ys_off = gp.T // gs.tm

    n_c = B + Bs
    n_c_pad = -(-n_c // 8) * 8
    c_all = jnp.concatenate([c_prompt, c_sample, jnp.zeros((n_c_pad - n_c, D), F32)], axis=0)
    mod_all = _ada_mod(c_all, w_ada, b_ada)
    kvmod_all = _ada_mod(c_all, kv_w_ada[None], kv_b_ada[None])[0]
    mods = [[g.mod_array(mod_all[l, lo:lo + g.n_seq]) for l in range(depth)]
            for g, lo in zip(groups, (0, B))]
    kvmods = [g.mod_array(kvmod_all[lo:lo + g.n_seq]) for g, lo in zip(groups, (0, B))]

    pos = (jnp.arange(S), past_len + jnp.arange(1))
    ret_rope = [_rope_table(p, D // RET_HEADS, D // RET_HEADS)[0:2] for p in pos]
    ret_rope = [(c[:, :c.shape[1] // 2], s[:, :s.shape[1] // 2]) for c, s in ret_rope]
    mla_rope = [_rope_table(p, R, LANE) for p in pos]
    tabs_p = _ret_tables(RET_CHUNK)
    gl_s = _ret_tables(1)[3]

    row = lambda g: g.reshape(1, -1)
    w_in = ret_w_in.astype(BF16)
    w_out = ret_w_out.astype(BF16)
    wsg, wsu, wsd = shared_w_gate.astype(BF16), shared_w_up.astype(BF16), shared_w_down.astype(BF16)
    wg, wu, wd = moe_w_gate.astype(BF16), moe_w_up.astype(BF16), moe_w_down.astype(BF16)
    rw_t = router_w.T
    rb = router_bias.reshape(-1, 1)
    w_dkv_c = mla_w_dkv[:, :C].astype(BF16)
    w_dkv_p = _pad_lanes(mla_w_dkv[:, C:]).astype(BF16)
    w_dkv_r = _pad_lanes(_rot_half_cols(mla_w_dkv[:, C:])).astype(BF16)
    w_uk = mla_w_uk.astype(BF16)
    w_uvt = jnp.swapaxes(mla_w_uv, 1, 2).astype(BF16)
    wq = []
    for j in range(depth - n_a):
        w3 = mla_w_uq[j].reshape(-1, H, dn + R)
        w_n = w3[:, :, :dn].reshape(-1, H * dn).astype(BF16)
        w_p = _pad_lanes(w3[:, :, dn:]).reshape(-1, H * LANE).astype(BF16)
        w_r = _pad_lanes(_rot_half_cols(w3[:, :, dn:])).reshape(-1, H * LANE).astype(BF16)
        wq.append((mla_w_dq[j].astype(BF16), row(mla_q_norm_g[j]), w_n, w_p, w_r, w_uk))
    w_o = mla_w_o.astype(BF16)

    kpe_pages = jnp.swapaxes(cache_kpe, 1, 2)

    xs = [x_prompt.reshape(gp.T, D), x_sample.reshape(gs.T, D)]
    y = None
    states = [[], []]
    ckv_out, kpe_out, kcat = [None, None], [None, None], [None, None]

    for l in range(depth):
        y_offs = (0, ys_off)
        xmid = [None, None]
        if l < n_a:
            for gi, g in enumerate(groups):
                pm = mods[gi][l - 1] if y is not None else None
                x, q, k, v, gt = _ret_in(g, xs[gi], y, y_offs[gi], pm, mods[gi][l], ret_rope[gi][0], ret_rope[gi][1],
                                         row(norm_mix_g[l]), w_in[l], BF16 if gi == 0 else F32)
                if gi == 0:
                    xmid[gi], st = _ret_chunk(B, S, x, q, k, v, gt, mods[gi][l], tabs_p, row(ret_gn_g[l]), w_out[l])
                else:
                    st, o = _ret_dec(state_ret, l, q, k, v, gl_s)
                    xmid[gi], = _tok_call("ret_post", functools.partial(_ret_post_kernel, D=D, H=RET_HEADS), g,
                                          [(x, 0), (o, 0), (gt, 0)], [mods[gi][l]], [],
                                          [row(ret_gn_g[l]), w_out[l]], [(D, F32)])
                states[gi].append(st)
        else:
            j = l - n_a
            for gi, g in enumerate(groups):
                xsg, yy, pm = xs[gi], y, mods[gi][l - 1]
                if l == n_a:
                    res = _kv_stream(g, xsg, y, y_offs[gi], pm, kvmods[gi], mla_rope[gi][0], mla_rope[gi][1],
                                     [row(kv_norm_g), w_dkv_c, w_dkv_p, w_dkv_r, row(mla_kv_norm_g)], C, R,
                                     BF16 if gi == 0 else F32, gi == 0)
                    xsg, ckv_out[gi], kpe_out[gi], kcat[gi] = res[:4]
                    if gi == 0:
                        vt_p = res[4]
                    yy, pm = None, None
                x, qcat = _mla_q(g, xsg, yy, y_offs[gi], pm, mods[gi][l], mla_rope[gi][0], mla_rope[gi][1],
                                 row(norm_mix_g[l]), wq[j], BF16 if gi == 0 else F32, gi == 0)
                if gi == 0:
                    xmid[gi] = _attn_prompt(B, S, x, qcat, kcat[gi], vt_p, mods[gi][l], w_uvt, w_o[j])
                else:
                    o_lat = _attn_dec(jnp.swapaxes(qcat, 0, 1), kcat[gi][:, None, :], cache_ckv, kpe_pages, page_table)
                    xmid[gi] = pl.pallas_call(
                        functools.partial(_mla_out_kernel, D=D, H=H), grid=(1,),
                        in_specs=[_const_spec(x), pl.BlockSpec((H, Bs, C), lambda i: (0, 0, 0)),
                                  _const_spec(mods[gi][l]), _const_spec(w_uvt), _const_spec(w_o[j])],
                        out_specs=_const_spec(x), out_shape=jax.ShapeDtypeStruct(x.shape, F32),
                        name="mla_out", compiler_params=_cparams(1),
                    )(x, jnp.swapaxes(o_lat, 0, 1), mods[gi][l], w_uvt, w_o[j])
        hx, cls = [], []
        for gi, g in enumerate(groups):
            h, xs[gi], c = _moe_pre(g, xmid[gi], mods[gi][l], row(norm_ffn_g[l]), rw_t, rb, wsg[l], wsu[l], wsd[l])
            hx.append(h)
            cls.append(c)
        y = _moe_routed(jnp.concatenate(hx), jnp.concatenate(cls), l, wg, wu, wd)

    outs = []
    for gi, g in enumerate(groups):
        o, = _tok_call("final_norm", functools.partial(_final_kernel, D=D), g, [(xs[gi], 0), (y, (0, ys_off)[gi], D // LANE)],
                       [mods[gi][depth - 1]], [], [row(final_norm_g)], [(D, F32)])
        outs.append(o)
    return (outs[0].reshape(B, S, D), outs[1].reshape(Bs, 1, D),
            jnp.stack(states[0]), jnp.stack(states[1]),
            ckv_out[0].reshape(B, S, C), kpe_out[0].reshape(B, S, R),
            ckv_out[1].reshape(Bs, 1, C), kpe_out[1].reshape(Bs, 1, R))
```

```python
import functools

import jax
import jax.numpy as jnp
from jax import lax
from jax.experimental import pallas as pl
from jax.experimental.pallas import tpu as pltpu

F32 = jnp.float32
BF16 = jnp.bfloat16

RET_HEADS = 4
MLA_HEADS = 8
QK_ROPE = 64
N_GROUPS = 4
EXPERTS_PER_GROUP = 4
ROPE_THETA = 10000.0
EPS = 1e-6
LOG2_E = 1.4426950408889634

_PAIRS = ((0, 1), (0, 2), (0, 3), (1, 2), (1, 3), (2, 3))
N_CLASSES = N_GROUPS * len(_PAIRS)

LANE = 128
VMEM_LIMIT_BYTES = 56 * 2 ** 20
TM_PROMPT = 256
MOE_TILE = 256
RET_CHUNK = 256
ATT_TQ = 256
ATT_TK = 256
ATT_CW = 512
DEC_NB = 8
DEC_CHUNK_PAGES = 8


def _dot(a, b):
    return jnp.dot(a, b, preferred_element_type=F32)


def _dot_nt(a, b):
    return lax.dot_general(a, b, (((1,), (1,)), ((), ())), preferred_element_type=F32)


def _dot_tn(a, b):
    return lax.dot_general(a, b, (((0,), (0,)), ((), ())), preferred_element_type=F32)


def _silu(x):
    return x * jax.nn.sigmoid(x)


def _rms(x, g):
    return x * lax.rsqrt(jnp.mean(x * x, axis=-1, keepdims=True) + EPS) * g


def _cparams(n_axes=1):
    return pltpu.CompilerParams(dimension_semantics=("arbitrary",) * n_axes,
                                vmem_limit_bytes=VMEM_LIMIT_BYTES)


class _Group:
    def __init__(self, n_seq, seq_len, tm, per_token_mod):
        self.n_seq, self.seq_len, self.tm = n_seq, seq_len, tm
        self.T = n_seq * seq_len
        self.n_tiles = self.T // tm
        self.per_token_mod = per_token_mod
        self.tiles_per_seq = max(seq_len // tm, 1)

    def tok_spec(self, width, offset=0):
        return pl.BlockSpec((self.tm, width), lambda i: (i + offset, 0))

    def mod_spec(self, width):
        if self.per_token_mod:
            return pl.BlockSpec((self.tm, width), lambda i: (i, 0))
        tps = self.tiles_per_seq
        return pl.BlockSpec((None, 1, width), lambda i: (i // tps, 0, 0))

    def rope_spec(self, width):
        if self.per_token_mod:
            return pl.BlockSpec((1, width), lambda i: (0, 0))
        tps = self.tiles_per_seq
        return pl.BlockSpec((self.tm, width), lambda i: (i % tps, 0))

    def mod_array(self, m):
        return m if self.per_token_mod else m[:, None, :]


def _const_spec(a):
    nd = a.ndim
    return pl.BlockSpec(a.shape, lambda i: (0,) * nd)


def _tok_call(name, body, grp, tok_ins, mod_ins, rope_ins, const_ins, outs):
    in_specs, args = [], []
    for a, off in tok_ins:
        in_specs.append(grp.tok_spec(a.shape[-1], off)); args.append(a)
    for a in mod_ins:
        in_specs.append(grp.mod_spec(a.shape[-1])); args.append(a)
    for a in rope_ins:
        in_specs.append(grp.rope_spec(a.shape[-1])); args.append(a)
    for a in const_ins:
        in_specs.append(_const_spec(a)); args.append(a)
    out_specs = [grp.tok_spec(width) for width, _ in outs]
    out_shape = [jax.ShapeDtypeStruct((grp.T, width), dtype) for width, dtype in outs]
    return pl.pallas_call(
        body, grid=(grp.n_tiles,), in_specs=in_specs, out_specs=out_specs, out_shape=out_shape,
        name=name, compiler_params=_cparams(1),
    )(*args)


def _mod_kernel(c_ref, w_ref, b_ref, o_ref):
    c = c_ref[...]
    o_ref[...] = _dot(_silu(c).astype(BF16), w_ref[...].astype(BF16)) + b_ref[...]


def _ada_mod(c_all, w, b):
    L, D, N = w.shape
    Bc = c_all.shape[0]
    tn = min(N, 2048)
    return pl.pallas_call(
        _mod_kernel, grid=(L, N // tn),
        in_specs=[pl.BlockSpec((Bc, D), lambda l, j: (0, 0)),
                  pl.BlockSpec((None, D, tn), lambda l, j: (l, 0, j)),
                  pl.BlockSpec((None, 1, tn), lambda l, j: (l, 0, j))],
        out_specs=pl.BlockSpec((None, Bc, tn), lambda l, j: (l, 0, j)),
        out_shape=jax.ShapeDtypeStruct((L, Bc, N), F32),
        name="ada_mod", compiler_params=_cparams(2),
    )(c_all, w, b[:, None, :])


def _resid(xs_ref, y_ref, pmod_ref, D):
    x = xs_ref[...]
    if y_ref is not None:
        x = x + pmod_ref[:, 5 * D:6 * D] * y_ref[...]
    return x


def _ret_in_kernel(*refs, has_y, D, H):
    refs = list(refs)
    xs_ref = refs.pop(0)
    y_ref = refs.pop(0) if has_y else None
    pmod_ref = refs.pop(0) if has_y else None
    mod_ref, cos_ref, sin_ref, g_ref, w_ref = refs[:5]
    outs = refs[5:]
    if has_y:
        xo_ref, outs = outs[0], outs[1:]
    q_ref, k_ref, v_ref, gt_ref = outs
    x = _resid(xs_ref, y_ref, pmod_ref, D)
    if has_y:
        xo_ref[...] = x
    h = _rms(x, g_ref[...]) * (1.0 + mod_ref[:, D:2 * D]) + mod_ref[:, 0:D]
    hb = h.astype(BF16)
    dk = D // H
    half = dk // 2
    cos, sin = cos_ref[...], sin_ref[...]
    for idx, (o_ref, scale) in enumerate(((q_ref, float(dk) ** -0.5), (k_ref, 1.0))):
        t = _dot(hb, w_ref[:, idx * D:(idx + 1) * D])
        for hh in range(H):
            x1 = t[:, hh * dk:hh * dk + half]
            x2 = t[:, hh * dk + half:(hh + 1) * dk]
            o_ref[:, hh * dk:hh * dk + half] = ((x1 * cos - x2 * sin) * scale).astype(o_ref.dtype)
            o_ref[:, hh * dk + half:(hh + 1) * dk] = ((x1 * sin + x2 * cos) * scale).astype(o_ref.dtype)
    v_ref[...] = _dot(hb, w_ref[:, 2 * D:3 * D]).astype(v_ref.dtype)
    gt_ref[...] = _dot(hb, w_ref[:, 3 * D:4 * D])


def _ret_in(grp, xs, y, y_off, pmod, mod, cos, sin, norm_g, w_in, qkv_dtype):
    D = xs.shape[-1]
    has_y = y is not None
    tok = [(xs, y_off if has_y else 0)] + ([(y, y_off)] if has_y else [])
    mods = ([pmod] if has_y else []) + [mod]
    outs = ([(D, F32)] if has_y else []) + [(D, qkv_dtype)] * 3 + [(D, F32)]
    res = _tok_call("ret_in", functools.partial(_ret_in_kernel, has_y=has_y, D=D, H=RET_HEADS), grp,
                    tok, mods, [cos, sin], [norm_g, w_in], outs)
    if not has_y:
        res = [xs] + list(res)
    return res


def _gn_gate(o, g, gn):
    mu = jnp.mean(o, axis=-1, keepdims=True)
    c = o - mu
    var = jnp.mean(c * c, axis=-1, keepdims=True)
    return _silu(g) * (c * lax.rsqrt(var + EPS) * gn)


def _ret_chunk_kernel(x_ref, q_ref, k_ref, v_ref, gt_ref, mod_ref, dec_ref, qd_ref, kd_ref, gl_ref,
                      gn_ref, wo_ref, xo_ref, s_ref, gat_ref, *, D, H):
    @pl.when(pl.program_id(1) == 0)
    def _():
        s_ref[...] = jnp.zeros_like(s_ref)

    dk = D // H
    for h in range(H):
        hs = slice(h * dk, (h + 1) * dk)
        qh, kh, vh = q_ref[:, hs], k_ref[:, hs], v_ref[:, hs]
        S = s_ref[h]
        scores = _dot_nt(qh, kh) * dec_ref[h]
        inner = _dot(scores.astype(BF16), vh)
        cross = _dot((qh.astype(F32) * qd_ref[:, h:h + 1]).astype(BF16), S.astype(BF16))
        kdec = (kh.astype(F32) * kd_ref[:, h:h + 1]).astype(BF16)
        s_ref[h] = gl_ref[:, h:h + 1] * S + _dot_tn(kdec, vh)
        gat_ref[:, hs] = _gn_gate(inner + cross, gt_ref[:, hs], gn_ref[:, hs]).astype(BF16)
    y = _dot(gat_ref[...], wo_ref[...])
    xo_ref[...] = x_ref[...] + mod_ref[:, 2 * D:3 * D] * y


def _ret_chunk(B, S, x, q, k, v, gt, mod, tabs, gn_g, w_out):
    D = x.shape[-1]
    H = RET_HEADS
    L = RET_CHUNK
    nc = S // L
    dk = D // H
    dec, qd, kd, gl = tabs
    tok = lambda w: pl.BlockSpec((L, w), lambda b, c: (b * nc + c, 0))
    cst = lambda a: pl.BlockSpec(a.shape, lambda b, c: (0,) * a.ndim)
    return pl.pallas_call(
        functools.partial(_ret_chunk_kernel, D=D, H=H), grid=(B, nc),
        in_specs=[tok(D)] * 5 + [pl.BlockSpec((None, 1, mod.shape[-1]), lambda b, c: (b, 0, 0)),
                                 cst(dec), cst(qd), cst(kd), cst(gl), cst(gn_g), cst(w_out)],
        out_specs=[tok(D), pl.BlockSpec((None, H, dk, dk), lambda b, c: (b, 0, 0, 0))],
        out_shape=[jax.ShapeDtypeStruct(x.shape, F32), jax.ShapeDtypeStruct((B, H, dk, dk), F32)],
        scratch_shapes=[pltpu.VMEM((L, D), BF16)],
        name="ret_chunk", compiler_params=_cparams(2),
    )(x, q, k, v, gt, mod, dec, qd, kd, gl, gn_g, w_out)


def _ret_dec_kernel(st_ref, q_ref, k_ref, v_ref, gl_ref, so_ref, o_ref, *, D, H, nb):
    is_update = pl.program_id(0) == pl.num_programs(0) - 1

    @pl.when(jnp.logical_not(is_update))
    def _():
        so_ref[...] = st_ref[...]
        o_ref[...] = jnp.zeros(o_ref.shape, F32)

    @pl.when(is_update)
    def _():
        _ret_dec_update(st_ref, q_ref, k_ref, v_ref, gl_ref, so_ref, o_ref, D=D, H=H, nb=nb)


def _ret_dec_update(st_ref, q_ref, k_ref, v_ref, gl_ref, so_ref, o_ref, *, D, H, nb):
    dk = D // H
    for h in range(H):
        hs = slice(h * dk, (h + 1) * dk)
        gh = gl_ref[:, h:h + 1]
        qh, kh, vh = q_ref[:, hs], k_ref[:, hs], v_ref[:, hs]
        inner = jnp.sum(qh * kh, axis=-1, keepdims=True) * vh
        qT = (qh * gh).T
        kT = kh.T
        rows = []
        for j in range(nb):
            S = st_ref[j, h]
            so_ref[j, h] = gh * S + kT[:, j:j + 1] * vh[j:j + 1, :]
            rows.append(jnp.sum(qT[:, j:j + 1] * S, axis=0, keepdims=True))
        o_ref[:, hs] = inner + jnp.concatenate(rows, axis=0)


def _ret_dec(state, layer, q, k, v, gl, in_place):
    L, B, H, dk, dv = state.shape
    D = q.shape[-1]
    nb = DEC_NB
    phases = 1 if in_place else L
    which = lambda p: (layer + 1 + p) % L if not in_place else layer
    row = pl.BlockSpec((nb, D), lambda p, i: (i, 0))
    st_spec = pl.BlockSpec((None, nb, H, dk, dv), lambda p, i: (which(p), i, 0, 0, 0))
    new_state, o = pl.pallas_call(
        functools.partial(_ret_dec_kernel, D=D, H=H, nb=nb), grid=(phases, B // nb),
        in_specs=[st_spec, row, row, row, pl.BlockSpec(gl.shape, lambda p, i: (0, 0))],
        out_specs=[st_spec, pl.BlockSpec((None, nb, D), lambda p, i: (p, i, 0))],
        out_shape=[jax.ShapeDtypeStruct(state.shape, F32), jax.ShapeDtypeStruct((phases, B, D), F32)],
        input_output_aliases={0: 0} if in_place else {},
        name="ret_dec", compiler_params=_cparams(2),
    )(state, q, k, v, gl)
    return new_state, o[phases - 1]


def _ret_post_kernel(x_ref, o_ref, gt_ref, mod_ref, gn_ref, wo_ref, xo_ref, *, D, H):
    dk = D // H
    parts = []
    for h in range(H):
        hs = slice(h * dk, (h + 1) * dk)
        parts.append(_gn_gate(o_ref[:, hs], gt_ref[:, hs], gn_ref[:, hs]).astype(BF16))
    y = _dot(jnp.concatenate(parts, axis=-1), wo_ref[...])
    xo_ref[...] = x_ref[...] + mod_ref[:, 2 * D:3 * D] * y


def _route(logits_t, bias):
    scores = jax.nn.sigmoid(logits_t)
    sel = scores + bias
    n_e = EXPERTS_PER_GROUP
    row = lambda a, e: a[e:e + 1, :]
    gscore = []
    for g in range(N_GROUPS):
        best2 = None
        for a, b in _PAIRS:
            s = row(sel, n_e * g + a) + row(sel, n_e * g + b)
            best2 = s if best2 is None else jnp.maximum(best2, s)
        gscore.append(best2)
    best, bestv = jnp.zeros_like(gscore[0], dtype=jnp.int32), gscore[0]
    for g in range(1, N_GROUPS):
        upd = gscore[g] > bestv
        best = jnp.where(upd, g, best)
        bestv = jnp.where(upd, gscore[g], bestv)

    def pick(a, j):
        out = row(a, j)
        for g in range(1, N_GROUPS):
            out = jnp.where(best == g, row(a, n_e * g + j), out)
        return out

    v = [pick(sel, j) for j in range(n_e)]
    sc = [pick(scores, j) for j in range(n_e)]
    i1, v1 = jnp.zeros_like(best), v[0]
    for j in range(1, n_e):
        upd = v[j] > v1
        i1 = jnp.where(upd, j, i1)
        v1 = jnp.where(upd, v[j], v1)
    i2, v2 = None, None
    for j in range(n_e):
        ok = i1 != j
        if i2 is None:
            i2 = jnp.where(ok, j, n_e)
            v2 = jnp.where(ok, v[j], -jnp.inf)
        else:
            upd = ok & (v[j] > v2)
            i2 = jnp.where(upd, j, i2)
            v2 = jnp.where(upd, v[j], v2)
    s1 = sc[0]
    s2 = sc[0]
    for j in range(1, n_e):
        s1 = jnp.where(i1 == j, sc[j], s1)
        s2 = jnp.where(i2 == j, sc[j], s2)
    tot = s1 + s2
    w1, w2 = s1 / tot, s2 / tot
    lo = jnp.minimum(i1, i2)
    hi = jnp.maximum(i1, i2)
    base = jnp.where(lo == 0, 0, jnp.where(lo == 1, 3, 5))
    cls = best * len(_PAIRS) + base + hi - lo - 1
    first_is_lo = i1 < i2
    return cls, jnp.where(first_is_lo, w1, w2), jnp.where(first_is_lo, w2, w1)


def _moe_pre_kernel(xp_ref, xsm_ref, modp_ref, mods_ref, g_ref, rwt_ref, rb_ref, wsg_ref, wsu_ref, wsd_ref,
                    hx_ref, xs_ref, cls_ref, *, D, n_p):
    is_s = pl.program_id(0) == n_p
    tm, ts = xp_ref.shape[0], xsm_ref.shape[0]
    pad_rows = lambda a: jnp.concatenate([a, jnp.zeros((tm - ts, a.shape[1]), a.dtype)], axis=0)
    x = jnp.where(is_s, pad_rows(xsm_ref[...]), xp_ref[...])
    mod = jnp.where(is_s, pad_rows(mods_ref[:, 3 * D:6 * D]), modp_ref[:, 3 * D:6 * D])
    h = _rms(x, g_ref[...]) * (1.0 + mod[:, D:2 * D]) + mod[:, 0:D]
    hb = h.astype(BF16)
    h_lo = (h - hb.astype(F32)).astype(BF16)
    rw = rwt_ref[...]
    rw_hi = rw.astype(BF16)
    rw_lo = (rw - rw_hi.astype(F32)).astype(BF16)
    logits_t = _dot_nt(rw_hi, hb) + (_dot_nt(rw_lo, hb) + _dot_nt(rw_hi, h_lo))
    cls, wa, wb = _route(logits_t, rb_ref[...])
    cls_ref[0] = cls
    rowi = lax.broadcasted_iota(jnp.int32, (LANE, tm), 0)
    extra = jnp.where(rowi == 0, wa, jnp.where(rowi == 1, wb, 0.0)).T
    hx_ref[:, 0:D] = h
    hx_ref[:, D:D + LANE] = extra
    hid = _silu(_dot(hb, wsg_ref[...])) * _dot(hb, wsu_ref[...])
    shared = _dot(hid.astype(BF16), wsd_ref[...])
    xs_ref[...] = x + mod[:, 2 * D:3 * D] * shared


def _moe_pre(gp, gs, x_p, x_s, mod_p, mod_s, norm_g, rw_t, rb, wsg, wsu, wsd):
    D = x_p.shape[-1]
    n_p, tm, tps = gp.n_tiles, gp.tm, gp.tiles_per_seq
    assert gs.T <= tm and gp.T % gs.T == 0
    T_all = gp.T + gs.T
    clamp = lambda i: jnp.minimum(i, n_p - 1)
    consts = (norm_g, rw_t, rb, wsg, wsu, wsd)
    in_specs = [pl.BlockSpec((tm, D), lambda i: (clamp(i), 0)), _const_spec(x_s),
                pl.BlockSpec((None, 1, mod_p.shape[-1]), lambda i: (clamp(i) // tps, 0, 0)), _const_spec(mod_s)]
    in_specs += [_const_spec(a) for a in consts]
    hx, xs, cls = pl.pallas_call(
        functools.partial(_moe_pre_kernel, D=D, n_p=n_p), grid=(n_p + 1,), in_specs=in_specs,
        out_specs=[gp.tok_spec(D + LANE), gp.tok_spec(D), pl.BlockSpec((1, 1, tm), lambda i: (i, 0, 0))],
        out_shape=[jax.ShapeDtypeStruct((T_all, D + LANE), F32), jax.ShapeDtypeStruct((T_all, D), F32),
                   jax.ShapeDtypeStruct((n_p + 1, 1, tm), jnp.int32)],
        name="moe_pre", compiler_params=_cparams(1),
    )(x_p, x_s, mod_p, mod_s, *consts)
    return hx, xs, cls.reshape(-1)[:T_all]


def _moe_kernel(pos_ref, te1_ref, te2_ref, nv_ref, dump_ref, nused_ref, hx_hbm, wg1, wu1, wd1, wg2, wu2, wd2,
                y_hbm, src_ref, hbuf, obuf, gsem, ssem, *, T_all, n_rows, tm, D):
    i = pl.program_id(0)
    nused = nused_ref[0]
    slot = i % 2

    def gather_row(tile, sl, r):
        t = src_ref[tile * tm + r]
        pltpu.make_async_copy(hx_hbm.at[pl.ds(t, 1)], hbuf.at[sl, pl.ds(r, 1)], gsem.at[sl]).start()

    def scatter_row(tile, sl, r):
        nv = nv_ref[tile]
        t = jnp.where(r < nv, src_ref[tile * tm + r], dump_ref[tile] + (r - nv))
        pltpu.make_async_copy(obuf.at[sl, pl.ds(r, 1)], y_hbm.at[pl.ds(t, 1)], ssem.at[sl]).start(priority=1)

    def rolled(fn, tile, sl):
        def step(r, c):
            fn(tile, sl, r)
            return c
        lax.fori_loop(0, tm, step, 0, unroll=8)

    def unrolled(fn, tile, sl):
        for r in range(tm):
            fn(tile, sl, r)

    def wait_gather(sl):
        pltpu.make_async_copy(hx_hbm.at[pl.ds(0, tm)], hbuf.at[sl], gsem.at[sl]).wait()

    def wait_scatter(sl):
        pltpu.make_async_copy(obuf.at[sl], y_hbm.at[pl.ds(0, tm)], ssem.at[sl]).wait()

    def compute(sl):
        hx = hbuf[sl]
        hb = hx[:, 0:D].astype(BF16)
        wa = hx[:, D:D + 1]
        wb = hx[:, D + 1:D + 2]
        hid_a = (_silu(_dot(hb, wg1[...])) * _dot(hb, wu1[...]) * wa).astype(BF16)
        hid_b = (_silu(_dot(hb, wg2[...])) * _dot(hb, wu2[...]) * wb).astype(BF16)
        obuf[sl] = _dot(hid_a, wd1[...]) + _dot(hid_b, wd2[...])

    @pl.when(i == 0)
    def _():
        def init(r, c):
            src_ref[r] = 0
            return c
        lax.fori_loop(0, n_rows, init, 0, unroll=8)

        def fill(t, c):
            src_ref[pos_ref[t]] = t
            return c
        lax.fori_loop(0, T_all, fill, 0, unroll=8)
        rolled(gather_row, 0, 0)

    active = i < nused
    steady = jnp.logical_and(i >= 1, i + 1 < nused)

    @pl.when(active)
    def _():
        wait_gather(slot)

        @pl.when(i >= 2)
        def _():
            wait_scatter(slot)

    @pl.when(steady)
    def _():
        compute(slot)
        unrolled(scatter_row, i - 1, 1 - slot)
        unrolled(gather_row, i + 1, 1 - slot)

    @pl.when(jnp.logical_and(active, jnp.logical_not(steady)))
    def _():
        @pl.when(i + 1 < nused)
        def _():
            rolled(gather_row, i + 1, 1 - slot)
        compute(slot)

        @pl.when(i >= 1)
        def _():
            rolled(scatter_row, i - 1, 1 - slot)

        @pl.when(i == nused - 1)
        def _():
            rolled(scatter_row, i, slot)
            wait_scatter(slot)

            @pl.when(i >= 1)
            def _():
                wait_scatter(1 - slot)

    @pl.when(jnp.logical_not(active))
    def _():
        obuf[slot] = jnp.zeros(obuf.shape[1:], F32)
        rolled(scatter_row, i, slot)
        wait_scatter(slot)


def _moe_routed(hx, cls, layer, wg, wu, wd):
    T_all = hx.shape[0]
    D, Fe = wg.shape[-2:]
    tm = MOE_TILE
    n_tiles = (T_all + N_CLASSES * (tm - 1)) // tm
    n_rows = n_tiles * tm
    onehot = (cls[:, None] == jnp.arange(N_CLASSES, dtype=jnp.int32)[None, :]).astype(jnp.int32)
    counts = jnp.sum(onehot, axis=0)
    rank = jnp.sum((jnp.cumsum(onehot, axis=0) - onehot) * onehot, axis=1)
    ntile_c = (counts + tm - 1) // tm
    tile_end = jnp.cumsum(ntile_c)
    tile_off = tile_end - ntile_c
    pos = (jnp.sum(onehot * tile_off[None, :], axis=1) * tm + rank).astype(jnp.int32)
    nused = tile_end[-1]
    tiles = jnp.arange(n_tiles, dtype=jnp.int32)
    tid = jnp.minimum(tiles, nused - 1)
    tile_cls = jnp.sum((tid[:, None] >= tile_end[None, :]).astype(jnp.int32), axis=1)
    nvalid = jnp.clip(counts[tile_cls] - (tiles - tile_off[tile_cls]) * tm, 0, tm)
    nvalid = jnp.where(tiles < nused, nvalid, 0).astype(jnp.int32)
    dump = (T_all + tiles * tm - (jnp.cumsum(nvalid) - nvalid)).astype(jnp.int32)
    pair_lo = jnp.array([a for a, _ in _PAIRS], jnp.int32)
    pair_hi = jnp.array([b for _, b in _PAIRS], jnp.int32)
    grp_id, pair_id = tile_cls // len(_PAIRS), tile_cls % len(_PAIRS)
    te1 = (grp_id * EXPERTS_PER_GROUP + pair_lo[pair_id]).astype(jnp.int32)
    te2 = (grp_id * EXPERTS_PER_GROUP + pair_hi[pair_id]).astype(jnp.int32)

    w1 = lambda shape: pl.BlockSpec((None, None) + shape, lambda i, pos, te1, *_: (layer, te1[i], 0, 0))
    w2 = lambda shape: pl.BlockSpec((None, None) + shape, lambda i, pos, te1, te2, *_: (layer, te2[i], 0, 0))
    any_spec = pl.BlockSpec(memory_space=pl.ANY)
    grid_spec = pltpu.PrefetchScalarGridSpec(
        num_scalar_prefetch=6, grid=(n_tiles,),
        in_specs=[any_spec, w1((D, Fe)), w1((D, Fe)), w1((Fe, D)), w2((D, Fe)), w2((D, Fe)), w2((Fe, D))],
        out_specs=any_spec,
        scratch_shapes=[pltpu.SMEM((n_rows,), jnp.int32),
                        pltpu.VMEM((2, tm, D + LANE), F32), pltpu.VMEM((2, tm, D), F32),
                        pltpu.SemaphoreType.DMA((2,)), pltpu.SemaphoreType.DMA((2,))])
    return pl.pallas_call(
        functools.partial(_moe_kernel, T_all=T_all, n_rows=n_rows, tm=tm, D=D),
        grid_spec=grid_spec, out_shape=jax.ShapeDtypeStruct((n_rows, D), F32),
        name="moe_routed", compiler_params=_cparams(1),
    )(pos, te1, te2, nvalid, dump, nused.reshape(1).astype(jnp.int32), hx, wg, wu, wd, wg, wu, wd)


def _kv_kernel(xs_ref, y_ref, pmod_ref, kvmod_ref, cos_ref, sin_ref, g_ref, wc_ref, wp_ref, wr_ref, lg_ref,
               xo_ref, ckv_ref, kpe_ref, kcat_ref, *maybe_vt_ref, D, C, R):
    x = _resid(xs_ref, y_ref, pmod_ref, D)
    xo_ref[...] = x
    hn = (_rms(x, g_ref[...]) * (1.0 + kvmod_ref[:, D:2 * D]) + kvmod_ref[:, 0:D]).astype(BF16)
    ckv = _rms(_dot(hn, wc_ref[...]), lg_ref[...])
    kpe = _dot(hn, wp_ref[...]) * cos_ref[...] + _dot(hn, wr_ref[...]) * sin_ref[...]
    ckv_ref[...] = ckv
    kpe_ref[...] = kpe[:, 0:R]
    kcat_ref[:, 0:C] = ckv.astype(kcat_ref.dtype)
    kcat_ref[:, C:C + LANE] = kpe.astype(kcat_ref.dtype)
    if maybe_vt_ref:
        maybe_vt_ref[0][...] = ckv.T.astype(maybe_vt_ref[0].dtype)


def _kv_stream(grp, xs, y, y_off, pmod, kvmod, cos, sin, consts, C, R, kcat_dtype, with_vt):
    D = xs.shape[-1]
    tok_ins = [xs, y]
    in_specs = [grp.tok_spec(D, y_off), grp.tok_spec(D, y_off), grp.mod_spec(pmod.shape[-1]),
                grp.mod_spec(kvmod.shape[-1]), grp.rope_spec(LANE), grp.rope_spec(LANE)]
    in_specs += [_const_spec(a) for a in consts]
    widths = [(D, F32), (C, F32), (R, F32), (C + LANE, kcat_dtype)]
    out_specs = [grp.tok_spec(w) for w, _ in widths]
    out_shape = [jax.ShapeDtypeStruct((grp.T, w), dt) for w, dt in widths]
    if with_vt:
        tps = grp.tiles_per_seq
        out_specs.append(pl.BlockSpec((None, C, grp.tm), lambda i: (i // tps, 0, i % tps)))
        out_shape.append(jax.ShapeDtypeStruct((grp.n_seq, C, grp.seq_len), BF16))
    return pl.pallas_call(
        functools.partial(_kv_kernel, D=D, C=C, R=R), grid=(grp.n_tiles,), in_specs=in_specs,
        out_specs=out_specs, out_shape=out_shape, name="kv_stream", compiler_params=_cparams(1),
    )(*tok_ins, pmod, kvmod, cos, sin, *consts)


def _mla_q_kernel(*refs, has_y, transposed, D, C, H, scale):
    refs = list(refs)
    xs_ref = refs.pop(0)
    y_ref = refs.pop(0) if has_y else None
    pmod_ref = refs.pop(0) if has_y else None
    mod_ref, cos_ref, sin_ref, g_ref, wdq_ref, qg_ref, wn_ref, wp_ref, wr_ref, wuk_ref = refs[:10]
    outs = refs[10:]
    if has_y:
        xo_ref, outs = outs[0], outs[1:]
    qcat_ref, = outs
    x = _resid(xs_ref, y_ref, pmod_ref, D)
    if has_y:
        xo_ref[...] = x
    h = (_rms(x, g_ref[...]) * (1.0 + mod_ref[:, D:2 * D]) + mod_ref[:, 0:D]).astype(BF16)
    cq = _rms(_dot(h, wdq_ref[...]), qg_ref[...]).astype(BF16)
    q_nope = _dot(cq, wn_ref[...]).astype(BF16)
    cos, sin = cos_ref[...], sin_ref[...]
    dn = q_nope.shape[-1] // H
    for hh in range(H):
        ls = slice(hh * LANE, (hh + 1) * LANE)
        q_lat = _dot(q_nope[:, hh * dn:(hh + 1) * dn], wuk_ref[hh])
        q_pe = _dot(cq, wp_ref[:, ls]) * cos + _dot(cq, wr_ref[:, ls]) * sin
        if transposed:
            tm = q_lat.shape[0]
            qcat_ref[0:C, hh * tm:(hh + 1) * tm] = (q_lat * scale).T.astype(qcat_ref.dtype)
            qcat_ref[C:C + LANE, hh * tm:(hh + 1) * tm] = (q_pe * scale).T.astype(qcat_ref.dtype)
        else:
            qcat_ref[hh, :, 0:C] = (q_lat * scale).astype(qcat_ref.dtype)
            qcat_ref[hh, :, C:C + LANE] = (q_pe * scale).astype(qcat_ref.dtype)


def _mla_q(grp, xs, y, y_off, pmod, mod, cos, sin, norm_g, wq, out_dtype, transposed):
    D = xs.shape[-1]
    w_dq, qg, w_n, w_p, w_r, w_uk = wq
    C = w_uk.shape[-1]
    H = MLA_HEADS
    has_y = y is not None
    scale = float(w_uk.shape[1] + QK_ROPE) ** -0.5
    if transposed:
        scale *= LOG2_E
    in_specs = [grp.tok_spec(D, y_off if has_y else 0)]
    in_specs += [grp.tok_spec(D, y_off), grp.mod_spec(pmod.shape[-1])] if has_y else []
    args = [xs] + ([y, pmod] if has_y else [])
    in_specs += [grp.mod_spec(mod.shape[-1]), grp.rope_spec(LANE), grp.rope_spec(LANE)]
    args += [mod, cos, sin]
    for a in (norm_g, w_dq, qg, w_n, w_p, w_r, w_uk):
        in_specs.append(_const_spec(a)); args.append(a)
    if transposed:
        q_spec = pl.BlockSpec((None, C + LANE, H * grp.tm), lambda i: (i, 0, 0))
        q_shape = jax.ShapeDtypeStruct((grp.n_tiles, C + LANE, H * grp.tm), out_dtype)
    else:
        q_spec = pl.BlockSpec((H, grp.tm, C + LANE), lambda i: (0, i, 0))
        q_shape = jax.ShapeDtypeStruct((H, grp.T, C + LANE), out_dtype)
    out_specs = ([grp.tok_spec(D)] if has_y else []) + [q_spec]
    out_shape = ([jax.ShapeDtypeStruct((grp.T, D), F32)] if has_y else []) + [q_shape]
    res = pl.pallas_call(
        functools.partial(_mla_q_kernel, has_y=has_y, transposed=transposed, D=D, C=C, H=H, scale=scale),
        grid=(grp.n_tiles,),
        in_specs=in_specs, out_specs=out_specs, out_shape=out_shape, name="mla_q", compiler_params=_cparams(1),
    )(*args)
    return (res[0], res[1]) if has_y else (xs, res[0])


def _mla_out(o_lat_heads, x, g1, wuvt_ref, wo_ref):
    parts = [_dot(o.astype(BF16), wuvt_ref[hh]).astype(BF16) for hh, o in enumerate(o_lat_heads)]
    return x + g1 * _dot(jnp.concatenate(parts, axis=-1), wo_ref[...])


def _attn_kernel(x_ref, qt_ref, k_ref, vt_ref, mod_ref, wuvt_ref, wo_ref, xo_ref, m_ref, l_ref, acc_ref, s0_ref,
                 *, D, C, H, tq, tk, cw):
    qi = pl.program_id(1)
    m_ref[...] = jnp.full(m_ref.shape, -jnp.inf, F32)
    l_ref[...] = jnp.zeros(l_ref.shape, F32)
    acc_ref[...] = jnp.zeros(acc_ref.shape, F32)
    n_col = H * tq

    def keys(j):
        return k_ref[pl.ds(pl.multiple_of(j * tk, tk), tk), :]

    s0_ref[...] = _dot(keys(0), qt_ref[:, 0:cw])

    def block(j, masked):
        kb = keys(j)
        vt = vt_ref[:, pl.ds(pl.multiple_of(j * tk, tk), tk)]
        if masked:
            kpos = lax.broadcasted_iota(jnp.int32, (tk, cw), 0)
            qpos = lax.broadcasted_iota(jnp.int32, (tk, cw), 1) & (tq - 1)
            keep = kpos <= qpos
        s_next = s0_ref[...]
        for c0 in range(0, n_col, cw):
            cs = slice(c0, c0 + cw)
            s = s_next
            if c0 + cw < n_col:
                s_next = _dot(kb, qt_ref[:, c0 + cw:c0 + 2 * cw])
            elif not masked:
                s0_ref[...] = _dot(keys(j + 1), qt_ref[:, 0:cw])
            if masked:
                s = jnp.where(keep, s, -1e30)
            m_old = m_ref[:, cs]
            m_new = jnp.maximum(m_old, jnp.max(s, axis=0, keepdims=True))
            alpha = jnp.exp2(m_old - m_new)
            p = jnp.exp2(s - m_new)
            l_ref[:, cs] = alpha * l_ref[:, cs] + jnp.sum(p, axis=0, keepdims=True)
            acc_ref[:, cs] = alpha * acc_ref[:, cs] + _dot(vt, p.astype(BF16))
            m_ref[:, cs] = m_new

    def body(j, c):
        block(j, False)
        return c
    lax.fori_loop(0, qi, body, 0)
    block(qi, True)
    parts = []
    for hh in range(H):
        cs = slice(hh * tq, (hh + 1) * tq)
        o_t = (acc_ref[:, cs] / l_ref[:, cs]).astype(BF16)
        parts.append(_dot_tn(o_t, wuvt_ref[hh]).astype(BF16))
    y = _dot(jnp.concatenate(parts, axis=-1), wo_ref[...])
    xo_ref[...] = x_ref[...] + mod_ref[:, 2 * D:3 * D] * y


def _attn_prompt(B, S, x, qt, kcat, vt, mod, w_uvt, w_o):
    D = x.shape[-1]
    H = MLA_HEADS
    _, Wq, n_col = qt.shape
    C = w_uvt.shape[1]
    tq, tk = ATT_TQ, ATT_TK
    assert tq == tk and tq & (tq - 1) == 0 and n_col == H * tq
    nq = S // tq
    cst = lambda a: pl.BlockSpec(a.shape, lambda b, i: (0,) * a.ndim)
    return pl.pallas_call(
        functools.partial(_attn_kernel, D=D, C=C, H=H, tq=tq, tk=tk, cw=ATT_CW), grid=(B, nq),
        in_specs=[pl.BlockSpec((tq, D), lambda b, i: (b * nq + i, 0)),
                  pl.BlockSpec((None, Wq, n_col), lambda b, i: (b * nq + i, 0, 0)),
                  pl.BlockSpec((S, Wq), lambda b, i: (b, 0)),
                  pl.BlockSpec((None, C, S), lambda b, i: (b, 0, 0)),
                  pl.BlockSpec((None, 1, mod.shape[-1]), lambda b, i: (b, 0, 0)), cst(w_uvt), cst(w_o)],
        out_specs=pl.BlockSpec((tq, D), lambda b, i: (b * nq + i, 0)),
        out_shape=jax.ShapeDtypeStruct(x.shape, F32),
        scratch_shapes=[pltpu.VMEM((1, n_col), F32), pltpu.VMEM((1, n_col), F32), pltpu.VMEM((C, n_col), F32),
                        pltpu.VMEM((tk, ATT_CW), F32)],
        name="attn_prompt", compiler_params=_cparams(2),
    )(x, qt, kcat, vt, mod, w_uvt, w_o)


def _attn_dec_kernel(pt_ref, q_ref, kn_ref, ckv_hbm, kpe_hbm, o_ref, cbuf, pbuf, kb_ref, s_ref, csem, psem,
                     *, C, R, n_pages, page, cp):
    b = pl.program_id(0)
    nb = pl.num_programs(0)
    slot = b % 2

    def start(seq, sl):
        def body(p, c):
            pg = pt_ref[seq * n_pages + p]
            pltpu.make_async_copy(ckv_hbm.at[pl.ds(pg, 1)], cbuf.at[sl, pl.ds(p, 1)], csem.at[sl]).start()
            pltpu.make_async_copy(kpe_hbm.at[pl.ds(pg, 1)], pbuf.at[sl, pl.ds(p, 1)], psem.at[sl]).start()
            return c
        lax.fori_loop(0, n_pages, body, 0, unroll=4)

    @pl.when(b == 0)
    def _():
        start(0, 0)

    @pl.when(b + 1 < nb)
    def _():
        start(b + 1, 1 - slot)

    pltpu.make_async_copy(ckv_hbm.at[pl.ds(0, n_pages)], cbuf.at[slot], csem.at[slot]).wait()
    pltpu.make_async_copy(kpe_hbm.at[pl.ds(0, n_pages)], pbuf.at[slot], psem.at[slot]).wait()

    q = q_ref[...]
    ql = q[:, 0:C].astype(BF16)
    qp = q[:, C:C + R].astype(BF16)
    ck = cp * page
    n_chunks = n_pages // cp
    for c in range(n_chunks):
        kc = cbuf[slot, c * cp:(c + 1) * cp].reshape(ck, C).astype(BF16)
        pct = jnp.concatenate([pbuf[slot, c * cp + i] for i in range(cp)], axis=-1).astype(BF16)
        kb_ref[c * ck:(c + 1) * ck, :] = kc
        s_ref[:, c * ck:(c + 1) * ck] = _dot_nt(ql, kc) + _dot(qp, pct)
    kn = kn_ref[...]
    s_new = jnp.sum(q * kn, axis=-1, keepdims=True)
    s = s_ref[...]
    m = jnp.maximum(jnp.max(s, axis=-1, keepdims=True), s_new)
    p = jnp.exp(s - m)
    p_new = jnp.exp(s_new - m)
    l = jnp.sum(p, axis=-1, keepdims=True) + p_new
    acc = p_new * kn[:, 0:C]
    pb = p.astype(BF16)
    for c in range(n_chunks):
        acc = acc + _dot(pb[:, c * ck:(c + 1) * ck], kb_ref[c * ck:(c + 1) * ck, :])
    o_ref[...] = acc / l


def _attn_dec(q_s, kn, cache_ckv, kpe_pages, page_table):
    B, H, Wq = q_s.shape
    _, page, C = cache_ckv.shape
    R = kpe_pages.shape[1]
    n_pages = page_table.shape[1]
    P = n_pages * page
    any_spec = pl.BlockSpec(memory_space=pl.ANY)
    grid_spec = pltpu.PrefetchScalarGridSpec(
        num_scalar_prefetch=1, grid=(B,),
        in_specs=[pl.BlockSpec((None, H, Wq), lambda b, pt: (b, 0, 0)),
                  pl.BlockSpec((None, 1, Wq), lambda b, pt: (b, 0, 0)), any_spec, any_spec],
        out_specs=pl.BlockSpec((None, H, C), lambda b, pt: (b, 0, 0)),
        scratch_shapes=[pltpu.VMEM((2, n_pages, page, C), F32), pltpu.VMEM((2, n_pages, R, page), F32),
                        pltpu.VMEM((P, C), BF16), pltpu.VMEM((H, P), F32),
                        pltpu.SemaphoreType.DMA((2,)), pltpu.SemaphoreType.DMA((2,))])
    return pl.pallas_call(
        functools.partial(_attn_dec_kernel, C=C, R=R, n_pages=n_pages, page=page, cp=DEC_CHUNK_PAGES),
        grid_spec=grid_spec, out_shape=jax.ShapeDtypeStruct((B, H, C), F32), name="attn_dec",
        compiler_params=_cparams(1),
    )(page_table.reshape(-1), q_s, kn, cache_ckv, kpe_pages)


def _mla_out_kernel(x_ref, o_ref, mod_ref, wuvt_ref, wo_ref, xo_ref, *, D, H):
    heads = [o_ref[hh] for hh in range(H)]
    xo_ref[...] = _mla_out(heads, x_ref[...], mod_ref[:, 2 * D:3 * D], wuvt_ref, wo_ref)


def _final_kernel(xs_ref, y_ref, pmod_ref, g_ref, o_ref, *, D):
    o_ref[...] = _rms(_resid(xs_ref, y_ref, pmod_ref, D), g_ref[...])


def _rope_table(pos, d, width):
    inv = ROPE_THETA ** (-jnp.arange(0, d, 2, dtype=F32) / d)
    ang = pos.astype(F32)[:, None] * inv[None, :]
    cos, sin = jnp.cos(ang), jnp.sin(ang)
    pad = jnp.zeros((pos.shape[0], width - d), F32)
    return jnp.concatenate([cos, cos, pad], axis=-1), jnp.concatenate([sin, sin, pad], axis=-1)


def _rot_half_cols(w):
    half = w.shape[-1] // 2
    return jnp.concatenate([-w[..., half:], w[..., :half]], axis=-1)


def _pad_lanes(w):
    pad = jnp.zeros(w.shape[:-1] + (LANE - w.shape[-1],), w.dtype)
    return jnp.concatenate([w, pad], axis=-1)


def _ret_tables(L):
    log_g = jnp.log1p(-jnp.exp2(-5.0 - jnp.arange(RET_HEADS, dtype=F32)))
    idx = jnp.arange(L, dtype=F32)
    diff = idx[:, None] - idx[None, :]
    dec = jnp.where(diff[None] >= 0, jnp.exp(jnp.maximum(diff, 0.0)[None] * log_g[:, None, None]), 0.0)
    qd = jnp.exp((idx + 1.0)[:, None] * log_g[None, :])
    kd = jnp.exp((L - 1.0 - idx)[:, None] * log_g[None, :])
    gl = jnp.exp(L * log_g)[None, :]
    return dec, qd, kd, gl


def kernel(x_prompt, x_sample, c_prompt, c_sample, state_ret, cache_ckv, cache_kpe, page_table, w_ada, b_ada, norm_mix_g, norm_ffn_g, ret_w_in, ret_gn_g, ret_w_out, kv_w_ada, kv_b_ada, kv_norm_g, mla_w_dkv, mla_kv_norm_g, mla_w_uk, mla_w_uv, mla_w_dq, mla_q_norm_g, mla_w_uq, mla_w_o, router_w, router_bias, moe_w_gate, moe_w_up, moe_w_down, shared_w_gate, shared_w_up, shared_w_down, final_norm_g):
    B, S, D = x_prompt.shape
    Bs, Ss, _ = x_sample.shape
    assert Ss == 1
    depth = w_ada.shape[0]
    n_a = ret_w_in.shape[0]
    H = MLA_HEADS
    C = mla_w_dkv.shape[1] - QK_ROPE
    R = QK_ROPE
    dn = mla_w_uk.shape[1]
    past_len = page_table.shape[1] * cache_ckv.shape[1]

    gp = _Group(B, S, TM_PROMPT, False)
    gs = _Group(Bs, 1, Bs, True)
    groups = (gp, gs)
    T_all = gp.T + gs.T
    ys_off = gp.T // gs.tm

    n_c = B + Bs
    n_c_pad = -(-n_c // 8) * 8
    c_all = jnp.concatenate([c_prompt, c_sample, jnp.zeros((n_c_pad - n_c, D), F32)], axis=0)
    mod_all = _ada_mod(c_all, w_ada, b_ada)
    kvmod_all = _ada_mod(c_all, kv_w_ada[None], kv_b_ada[None])[0]
    mods = [[g.mod_array(mod_all[l, lo:lo + g.n_seq]) for l in range(depth)]
            for g, lo in zip(groups, (0, B))]
    kvmods = [g.mod_array(kvmod_all[lo:lo + g.n_seq]) for g, lo in zip(groups, (0, B))]

    pos = (jnp.arange(S), past_len + jnp.arange(1))
    ret_rope = [_rope_table(p, D // RET_HEADS, D // RET_HEADS)[0:2] for p in pos]
    ret_rope = [(c[:, :c.shape[1] // 2], s[:, :s.shape[1] // 2]) for c, s in ret_rope]
    mla_rope = [_rope_table(p, R, LANE) for p in pos]
    tabs_p = _ret_tables(RET_CHUNK)
    gl_s = _ret_tables(1)[3]

    row = lambda g: g.reshape(1, -1)
    w_in = ret_w_in.astype(BF16)
    w_out = ret_w_out.astype(BF16)
    wsg, wsu, wsd = shared_w_gate.astype(BF16), shared_w_up.astype(BF16), shared_w_down.astype(BF16)
    wg, wu, wd = moe_w_gate.astype(BF16), moe_w_up.astype(BF16), moe_w_down.astype(BF16)
    rw_t = router_w.T
    rb = router_bias.reshape(-1, 1)
    w_dkv_c = mla_w_dkv[:, :C].astype(BF16)
    w_dkv_p = _pad_lanes(mla_w_dkv[:, C:]).astype(BF16)
    w_dkv_r = _pad_lanes(_rot_half_cols(mla_w_dkv[:, C:])).astype(BF16)
    w_uk = mla_w_uk.astype(BF16)
    w_uvt = jnp.swapaxes(mla_w_uv, 1, 2).astype(BF16)
    wq = []
    for j in range(depth - n_a):
        w3 = mla_w_uq[j].reshape(-1, H, dn + R)
        w_n = w3[:, :, :dn].reshape(-1, H * dn).astype(BF16)
        w_p = _pad_lanes(w3[:, :, dn:]).reshape(-1, H * LANE).astype(BF16)
        w_r = _pad_lanes(_rot_half_cols(w3[:, :, dn:])).reshape(-1, H * LANE).astype(BF16)
        wq.append((mla_w_dq[j].astype(BF16), row(mla_q_norm_g[j]), w_n, w_p, w_r, w_uk))
    w_o = mla_w_o.astype(BF16)

    kpe_pages = jnp.swapaxes(cache_kpe, 1, 2)

    xs = [x_prompt.reshape(gp.T, D), x_sample.reshape(gs.T, D)]
    y = None
    states_p, state_s = [], None
    ckv_out, kpe_out, kcat = [None, None], [None, None], [None, None]

    for l in range(depth):
        y_offs = (0, ys_off)
        xmid = [None, None]
        if l < n_a:
            for gi, g in enumerate(groups):
                pm = mods[gi][l - 1] if y is not None else None
                x, q, k, v, gt = _ret_in(g, xs[gi], y, y_offs[gi], pm, mods[gi][l], ret_rope[gi][0], ret_rope[gi][1],
                                         row(norm_mix_g[l]), w_in[l], BF16 if gi == 0 else F32)
                if gi == 0:
                    xmid[gi], st = _ret_chunk(B, S, x, q, k, v, gt, mods[gi][l], tabs_p, row(ret_gn_g[l]), w_out[l])
                else:
                    state_s, o = _ret_dec(state_ret if l == 0 else state_s, l, q, k, v, gl_s, in_place=l > 0)
                    xmid[gi], = _tok_call("ret_post", functools.partial(_ret_post_kernel, D=D, H=RET_HEADS), g,
                                          [(x, 0), (o, 0), (gt, 0)], [mods[gi][l]], [],
                                          [row(ret_gn_g[l]), w_out[l]], [(D, F32)])
                if gi == 0:
                    states_p.append(st)
        else:
            j = l - n_a
            for gi, g in enumerate(groups):
                xsg, yy, pm = xs[gi], y, mods[gi][l - 1]
                if l == n_a:
                    res = _kv_stream(g, xsg, y, y_offs[gi], pm, kvmods[gi], mla_rope[gi][0], mla_rope[gi][1],
                                     [row(kv_norm_g), w_dkv_c, w_dkv_p, w_dkv_r, row(mla_kv_norm_g)], C, R,
                                     BF16 if gi == 0 else F32, gi == 0)
                    xsg, ckv_out[gi], kpe_out[gi], kcat[gi] = res[:4]
                    if gi == 0:
                        vt_p = res[4]
                    yy, pm = None, None
                x, qcat = _mla_q(g, xsg, yy, y_offs[gi], pm, mods[gi][l], mla_rope[gi][0], mla_rope[gi][1],
                                 row(norm_mix_g[l]), wq[j], BF16 if gi == 0 else F32, gi == 0)
                if gi == 0:
                    xmid[gi] = _attn_prompt(B, S, x, qcat, kcat[gi], vt_p, mods[gi][l], w_uvt, w_o[j])
                else:
                    o_lat = _attn_dec(jnp.swapaxes(qcat, 0, 1), kcat[gi][:, None, :], cache_ckv, kpe_pages, page_table)
                    xmid[gi] = pl.pallas_call(
                        functools.partial(_mla_out_kernel, D=D, H=H), grid=(1,),
                        in_specs=[_const_spec(x), pl.BlockSpec((H, Bs, C), lambda i: (0, 0, 0)),
                                  _const_spec(mods[gi][l]), _const_spec(w_uvt), _const_spec(w_o[j])],
                        out_specs=_const_spec(x), out_shape=jax.ShapeDtypeStruct(x.shape, F32),
                        name="mla_out", compiler_params=_cparams(1),
                    )(x, jnp.swapaxes(o_lat, 0, 1), mods[gi][l], w_uvt, w_o[j])
        hx, xs_all, cls = _moe_pre(gp, gs, xmid[0], xmid[1], mods[0][l], mods[1][l], row(norm_ffn_g[l]), rw_t, rb,
                                   wsg[l], wsu[l], wsd[l])
        xs = [xs_all, xs_all]
        y = _moe_routed(hx, cls, l, wg, wu, wd)

    outs = []
    for gi, g in enumerate(groups):
        o, = _tok_call("final_norm", functools.partial(_final_kernel, D=D), g,
                       [(xs[gi], (0, ys_off)[gi]), (y, (0, ys_off)[gi])],
                       [mods[gi][depth - 1]], [], [row(final_norm_g)], [(D, F32)])
        outs.append(o)
    return (outs[0].reshape(B, S, D), outs[1].reshape(Bs, 1, D),
            jnp.stack(states_p), state_s,
            ckv_out[0].reshape(B, S, C), kpe_out[0].reshape(B, S, R),
            ckv_out[1].reshape(Bs, 1, C), kpe_out[1].reshape(Bs, 1, R))
```

```python
import functools

import jax
import jax.numpy as jnp
from jax import lax
from jax.experimental import pallas as pl
from jax.experimental.pallas import tpu as pltpu

F32 = jnp.float32
BF16 = jnp.bfloat16

RET_HEADS = 4
MLA_HEADS = 8
QK_ROPE = 64
N_GROUPS = 4
EXPERTS_PER_GROUP = 4
ROPE_THETA = 10000.0
EPS = 1e-6
LOG2_E = 1.4426950408889634

_PAIRS = ((0, 1), (0, 2), (0, 3), (1, 2), (1, 3), (2, 3))
N_CLASSES = N_GROUPS * len(_PAIRS)

LANE = 128
VMEM_LIMIT_BYTES = 56 * 2 ** 20
TM_PROMPT = 256
MOE_TILE = 256
RET_CHUNK = 256
ATT_TQ = 256
ATT_TK = 256
ATT_CW = 512
DEC_NB = 8
DEC_CHUNK_PAGES = 8


def _dot(a, b):
    return jnp.dot(a, b, preferred_element_type=F32)


def _dot_nt(a, b):
    return lax.dot_general(a, b, (((1,), (1,)), ((), ())), preferred_element_type=F32)


def _dot_tn(a, b):
    return lax.dot_general(a, b, (((0,), (0,)), ((), ())), preferred_element_type=F32)


def _silu(x):
    return x * jax.nn.sigmoid(x)


def _rms(x, g):
    return x * lax.rsqrt(jnp.mean(x * x, axis=-1, keepdims=True) + EPS) * g


def _cparams(n_axes=1):
    return pltpu.CompilerParams(dimension_semantics=("arbitrary",) * n_axes,
                                vmem_limit_bytes=VMEM_LIMIT_BYTES)


class _Group:
    def __init__(self, n_seq, seq_len, tm, per_token_mod):
        self.n_seq, self.seq_len, self.tm = n_seq, seq_len, tm
        self.T = n_seq * seq_len
        self.n_tiles = self.T // tm
        self.per_token_mod = per_token_mod
        self.tiles_per_seq = max(seq_len // tm, 1)

    def tok_spec(self, width, offset=0):
        return pl.BlockSpec((self.tm, width), lambda i: (i + offset, 0))

    def mod_spec(self, width):
        if self.per_token_mod:
            return pl.BlockSpec((self.tm, width), lambda i: (i, 0))
        tps = self.tiles_per_seq
        return pl.BlockSpec((None, 1, width), lambda i: (i // tps, 0, 0))

    def rope_spec(self, width):
        if self.per_token_mod:
            return pl.BlockSpec((1, width), lambda i: (0, 0))
        tps = self.tiles_per_seq
        return pl.BlockSpec((self.tm, width), lambda i: (i % tps, 0))

    def mod_array(self, m):
        return m if self.per_token_mod else m[:, None, :]


def _const_spec(a):
    nd = a.ndim
    return pl.BlockSpec(a.shape, lambda i: (0,) * nd)


def _tile_call(name, body, grp, in_specs, args, out_specs, out_shape, y_index=None, pos=None):
    if y_index is None:
        return pl.pallas_call(body, grid=(grp.n_tiles,), in_specs=in_specs, out_specs=out_specs,
                              out_shape=out_shape, name=name, compiler_params=_cparams(1))(*args)
    tm, D = grp.tm, args[y_index].shape[-1]

    def lift(spec):
        return pl.BlockSpec(spec.block_shape, lambda i, pos_ref, _map=spec.index_map: _map(i))
    in_specs = [pl.BlockSpec(memory_space=pl.ANY) if k == y_index else lift(s) for k, s in enumerate(in_specs)]
    out_specs = [lift(s) for s in out_specs]
    n_refs = len(args) + len(out_specs)

    def kern(pos_ref, *refs):
        main, (ybuf, sem) = list(refs[:n_refs]), refs[n_refs:]
        y_hbm = main[y_index]
        i = pl.program_id(0)
        n = pl.num_programs(0)
        slot = i % 2

        def start_row(tile, sl, r):
            p = pos_ref[tile * tm + r]
            pltpu.make_async_copy(y_hbm.at[pl.ds(p, 1)], ybuf.at[sl, pl.ds(r, 1)], sem.at[sl]).start()

        @pl.when(i == 0)
        def _():
            def step(r, c):
                start_row(0, 0, r)
                return c
            lax.fori_loop(0, tm, step, 0, unroll=8)

        pltpu.make_async_copy(y_hbm.at[pl.ds(0, tm)], ybuf.at[slot], sem.at[slot]).wait()
        main[y_index] = ybuf.at[slot]

        @pl.when(i + 1 < n)
        def _():
            for r in range(tm):
                start_row(i + 1, 1 - slot, r)
            body(*main)

        @pl.when(i + 1 == n)
        def _():
            body(*main)

    grid_spec = pltpu.PrefetchScalarGridSpec(
        num_scalar_prefetch=1, grid=(grp.n_tiles,), in_specs=in_specs, out_specs=out_specs,
        scratch_shapes=[pltpu.VMEM((2, tm, D), F32), pltpu.SemaphoreType.DMA((2,))])
    return pl.pallas_call(kern, grid_spec=grid_spec, out_shape=out_shape, name=name,
                          compiler_params=_cparams(1))(pos, *args)


def _tok_call(name, body, grp, tok_ins, mod_ins, rope_ins, const_ins, outs, y_index=None, pos=None):
    in_specs, args = [], []
    for a, off in tok_ins:
        in_specs.append(grp.tok_spec(a.shape[-1], off)); args.append(a)
    for a in mod_ins:
        in_specs.append(grp.mod_spec(a.shape[-1])); args.append(a)
    for a in rope_ins:
        in_specs.append(grp.rope_spec(a.shape[-1])); args.append(a)
    for a in const_ins:
        in_specs.append(_const_spec(a)); args.append(a)
    out_specs = [grp.tok_spec(width) for width, _ in outs]
    out_shape = [jax.ShapeDtypeStruct((grp.T, width), dtype) for width, dtype in outs]
    return _tile_call(name, body, grp, in_specs, args, out_specs, out_shape, y_index, pos)


def _mod_kernel(c_ref, w_ref, b_ref, o_ref):
    c = c_ref[...]
    o_ref[...] = _dot(_silu(c).astype(BF16), w_ref[...].astype(BF16)) + b_ref[...]


def _ada_mod(c_all, w, b):
    L, D, N = w.shape
    Bc = c_all.shape[0]
    tn = min(N, 2048)
    return pl.pallas_call(
        _mod_kernel, grid=(L, N // tn),
        in_specs=[pl.BlockSpec((Bc, D), lambda l, j: (0, 0)),
                  pl.BlockSpec((None, D, tn), lambda l, j: (l, 0, j)),
                  pl.BlockSpec((None, 1, tn), lambda l, j: (l, 0, j))],
        out_specs=pl.BlockSpec((None, Bc, tn), lambda l, j: (l, 0, j)),
        out_shape=jax.ShapeDtypeStruct((L, Bc, N), F32),
        name="ada_mod", compiler_params=_cparams(2),
    )(c_all, w, b[:, None, :])


def _resid(xs_ref, y_ref, pmod_ref, D):
    x = xs_ref[...]
    if y_ref is not None:
        x = x + pmod_ref[:, 5 * D:6 * D] * y_ref[...]
    return x


def _ret_in_kernel(*refs, has_y, D, H):
    refs = list(refs)
    xs_ref = refs.pop(0)
    y_ref = refs.pop(0) if has_y else None
    pmod_ref = refs.pop(0) if has_y else None
    mod_ref, cos_ref, sin_ref, g_ref, w_ref = refs[:5]
    outs = refs[5:]
    if has_y:
        xo_ref, outs = outs[0], outs[1:]
    q_ref, k_ref, v_ref, gt_ref = outs
    x = _resid(xs_ref, y_ref, pmod_ref, D)
    if has_y:
        xo_ref[...] = x
    h = _rms(x, g_ref[...]) * (1.0 + mod_ref[:, D:2 * D]) + mod_ref[:, 0:D]
    hb = h.astype(BF16)
    dk = D // H
    half = dk // 2
    cos, sin = cos_ref[...], sin_ref[...]
    for idx, (o_ref, scale) in enumerate(((q_ref, float(dk) ** -0.5), (k_ref, 1.0))):
        t = _dot(hb, w_ref[:, idx * D:(idx + 1) * D])
        for hh in range(H):
            x1 = t[:, hh * dk:hh * dk + half]
            x2 = t[:, hh * dk + half:(hh + 1) * dk]
            o_ref[:, hh * dk:hh * dk + half] = ((x1 * cos - x2 * sin) * scale).astype(o_ref.dtype)
            o_ref[:, hh * dk + half:(hh + 1) * dk] = ((x1 * sin + x2 * cos) * scale).astype(o_ref.dtype)
    v_ref[...] = _dot(hb, w_ref[:, 2 * D:3 * D]).astype(v_ref.dtype)
    gt_ref[...] = _dot(hb, w_ref[:, 3 * D:4 * D])


def _ret_in(grp, xs, y, y_off, pos, pmod, mod, cos, sin, norm_g, w_in, qkv_dtype):
    D = xs.shape[-1]
    has_y = y is not None
    tok = [(xs, y_off if has_y else 0)] + ([(y, y_off)] if has_y else [])
    mods = ([pmod] if has_y else []) + [mod]
    outs = ([(D, F32)] if has_y else []) + [(D, qkv_dtype)] * 3 + [(D, F32)]
    res = _tok_call("ret_in", functools.partial(_ret_in_kernel, has_y=has_y, D=D, H=RET_HEADS), grp,
                    tok, mods, [cos, sin], [norm_g, w_in], outs, 1 if has_y else None, pos)
    if not has_y:
        res = [xs] + list(res)
    return res


def _gn_gate(o, g, gn):
    mu = jnp.mean(o, axis=-1, keepdims=True)
    c = o - mu
    var = jnp.mean(c * c, axis=-1, keepdims=True)
    return _silu(g) * (c * lax.rsqrt(var + EPS) * gn)


def _ret_chunk_kernel(x_ref, q_ref, k_ref, v_ref, gt_ref, mod_ref, dec_ref, qd_ref, kd_ref, gl_ref,
                      gn_ref, wo_ref, xo_ref, s_ref, gat_ref, *, D, H):
    @pl.when(pl.program_id(1) == 0)
    def _():
        s_ref[...] = jnp.zeros_like(s_ref)

    dk = D // H
    for h in range(H):
        hs = slice(h * dk, (h + 1) * dk)
        qh, kh, vh = q_ref[:, hs], k_ref[:, hs], v_ref[:, hs]
        S = s_ref[h]
        scores = _dot_nt(qh, kh) * dec_ref[h]
        inner = _dot(scores.astype(BF16), vh)
        cross = _dot((qh.astype(F32) * qd_ref[:, h:h + 1]).astype(BF16), S.astype(BF16))
        kdec = (kh.astype(F32) * kd_ref[:, h:h + 1]).astype(BF16)
        s_ref[h] = gl_ref[:, h:h + 1] * S + _dot_tn(kdec, vh)
        gat_ref[:, hs] = _gn_gate(inner + cross, gt_ref[:, hs], gn_ref[:, hs]).astype(BF16)
    y = _dot(gat_ref[...], wo_ref[...])
    xo_ref[...] = x_ref[...] + mod_ref[:, 2 * D:3 * D] * y


def _ret_chunk(B, S, x, q, k, v, gt, mod, tabs, gn_g, w_out):
    D = x.shape[-1]
    H = RET_HEADS
    L = RET_CHUNK
    nc = S // L
    dk = D // H
    dec, qd, kd, gl = tabs
    tok = lambda w: pl.BlockSpec((L, w), lambda b, c: (b * nc + c, 0))
    cst = lambda a: pl.BlockSpec(a.shape, lambda b, c: (0,) * a.ndim)
    return pl.pallas_call(
        functools.partial(_ret_chunk_kernel, D=D, H=H), grid=(B, nc),
        in_specs=[tok(D)] * 5 + [pl.BlockSpec((None, 1, mod.shape[-1]), lambda b, c: (b, 0, 0)),
                                 cst(dec), cst(qd), cst(kd), cst(gl), cst(gn_g), cst(w_out)],
        out_specs=[tok(D), pl.BlockSpec((None, H, dk, dk), lambda b, c: (b, 0, 0, 0))],
        out_shape=[jax.ShapeDtypeStruct(x.shape, F32), jax.ShapeDtypeStruct((B, H, dk, dk), F32)],
        scratch_shapes=[pltpu.VMEM((L, D), BF16)],
        name="ret_chunk", compiler_params=_cparams(2),
    )(x, q, k, v, gt, mod, dec, qd, kd, gl, gn_g, w_out)


def _ret_dec_kernel(st_ref, q_ref, k_ref, v_ref, gl_ref, so_ref, o_ref, *, D, H, nb):
    is_update = pl.program_id(0) == pl.num_programs(0) - 1

    @pl.when(jnp.logical_not(is_update))
    def _():
        so_ref[...] = st_ref[...]
        o_ref[...] = jnp.zeros(o_ref.shape, F32)

    @pl.when(is_update)
    def _():
        _ret_dec_update(st_ref, q_ref, k_ref, v_ref, gl_ref, so_ref, o_ref, D=D, H=H, nb=nb)


def _ret_dec_update(st_ref, q_ref, k_ref, v_ref, gl_ref, so_ref, o_ref, *, D, H, nb):
    dk = D // H
    for h in range(H):
        hs = slice(h * dk, (h + 1) * dk)
        gh = gl_ref[:, h:h + 1]
        qh, kh, vh = q_ref[:, hs], k_ref[:, hs], v_ref[:, hs]
        inner = jnp.sum(qh * kh, axis=-1, keepdims=True) * vh
        qT = (qh * gh).T
        kT = kh.T
        rows = []
        for j in range(nb):
            S = st_ref[j, h]
            so_ref[j, h] = gh * S + kT[:, j:j + 1] * vh[j:j + 1, :]
            rows.append(jnp.sum(qT[:, j:j + 1] * S, axis=0, keepdims=True))
        o_ref[:, hs] = inner + jnp.concatenate(rows, axis=0)


def _ret_dec(state, layer, q, k, v, gl, in_place):
    L, B, H, dk, dv = state.shape
    D = q.shape[-1]
    nb = DEC_NB
    phases = 1 if in_place else L
    which = lambda p: (layer + 1 + p) % L if not in_place else layer
    row = pl.BlockSpec((nb, D), lambda p, i: (i, 0))
    st_spec = pl.BlockSpec((None, nb, H, dk, dv), lambda p, i: (which(p), i, 0, 0, 0))
    new_state, o = pl.pallas_call(
        functools.partial(_ret_dec_kernel, D=D, H=H, nb=nb), grid=(phases, B // nb),
        in_specs=[st_spec, row, row, row, pl.BlockSpec(gl.shape, lambda p, i: (0, 0))],
        out_specs=[st_spec, pl.BlockSpec((None, nb, D), lambda p, i: (p, i, 0))],
        out_shape=[jax.ShapeDtypeStruct(state.shape, F32), jax.ShapeDtypeStruct((phases, B, D), F32)],
        input_output_aliases={0: 0} if in_place else {},
        name="ret_dec", compiler_params=_cparams(2),
    )(state, q, k, v, gl)
    return new_state, o[phases - 1]


def _ret_post_kernel(x_ref, o_ref, gt_ref, mod_ref, gn_ref, wo_ref, xo_ref, *, D, H):
    dk = D // H
    parts = []
    for h in range(H):
        hs = slice(h * dk, (h + 1) * dk)
        parts.append(_gn_gate(o_ref[:, hs], gt_ref[:, hs], gn_ref[:, hs]).astype(BF16))
    y = _dot(jnp.concatenate(parts, axis=-1), wo_ref[...])
    xo_ref[...] = x_ref[...] + mod_ref[:, 2 * D:3 * D] * y


def _route(logits_t, bias):
    scores = jax.nn.sigmoid(logits_t)
    sel = scores + bias
    n_e = EXPERTS_PER_GROUP
    row = lambda a, e: a[e:e + 1, :]
    gscore = []
    for g in range(N_GROUPS):
        best2 = None
        for a, b in _PAIRS:
            s = row(sel, n_e * g + a) + row(sel, n_e * g + b)
            best2 = s if best2 is None else jnp.maximum(best2, s)
        gscore.append(best2)
    best, bestv = jnp.zeros_like(gscore[0], dtype=jnp.int32), gscore[0]
    for g in range(1, N_GROUPS):
        upd = gscore[g] > bestv
        best = jnp.where(upd, g, best)
        bestv = jnp.where(upd, gscore[g], bestv)

    def pick(a, j):
        out = row(a, j)
        for g in range(1, N_GROUPS):
            out = jnp.where(best == g, row(a, n_e * g + j), out)
        return out

    v = [pick(sel, j) for j in range(n_e)]
    sc = [pick(scores, j) for j in range(n_e)]
    i1, v1 = jnp.zeros_like(best), v[0]
    for j in range(1, n_e):
        upd = v[j] > v1
        i1 = jnp.where(upd, j, i1)
        v1 = jnp.where(upd, v[j], v1)
    i2, v2 = None, None
    for j in range(n_e):
        ok = i1 != j
        if i2 is None:
            i2 = jnp.where(ok, j, n_e)
            v2 = jnp.where(ok, v[j], -jnp.inf)
        else:
            upd = ok & (v[j] > v2)
            i2 = jnp.where(upd, j, i2)
            v2 = jnp.where(upd, v[j], v2)
    s1 = sc[0]
    s2 = sc[0]
    for j in range(1, n_e):
        s1 = jnp.where(i1 == j, sc[j], s1)
        s2 = jnp.where(i2 == j, sc[j], s2)
    tot = s1 + s2
    w1, w2 = s1 / tot, s2 / tot
    lo = jnp.minimum(i1, i2)
    hi = jnp.maximum(i1, i2)
    base = jnp.where(lo == 0, 0, jnp.where(lo == 1, 3, 5))
    cls = best * len(_PAIRS) + base + hi - lo - 1
    first_is_lo = i1 < i2
    return cls, jnp.where(first_is_lo, w1, w2), jnp.where(first_is_lo, w2, w1)


def _moe_pre_kernel(xp_ref, xsm_ref, modp_ref, mods_ref, g_ref, rwt_ref, rb_ref, wsg_ref, wsu_ref, wsd_ref,
                    hx_ref, xs_ref, cls_ref, *, D, n_p):
    is_s = pl.program_id(0) == n_p
    tm, ts = xp_ref.shape[0], xsm_ref.shape[0]
    pad_rows = lambda a: jnp.concatenate([a, jnp.zeros((tm - ts, a.shape[1]), a.dtype)], axis=0)
    x = jnp.where(is_s, pad_rows(xsm_ref[...]), xp_ref[...])
    mod = jnp.where(is_s, pad_rows(mods_ref[:, 3 * D:6 * D]), modp_ref[:, 3 * D:6 * D])
    h = _rms(x, g_ref[...]) * (1.0 + mod[:, D:2 * D]) + mod[:, 0:D]
    hb = h.astype(BF16)
    h_lo = (h - hb.astype(F32)).astype(BF16)
    rw = rwt_ref[...]
    rw_hi = rw.astype(BF16)
    rw_lo = (rw - rw_hi.astype(F32)).astype(BF16)
    logits_t = _dot_nt(rw_hi, hb) + (_dot_nt(rw_lo, hb) + _dot_nt(rw_hi, h_lo))
    cls, wa, wb = _route(logits_t, rb_ref[...])
    cls_ref[0] = cls
    rowi = lax.broadcasted_iota(jnp.int32, (LANE, tm), 0)
    extra = jnp.where(rowi == 0, wa, jnp.where(rowi == 1, wb, 0.0)).T
    hx_ref[:, 0:D] = h
    hx_ref[:, D:D + LANE] = extra
    hid = _silu(_dot(hb, wsg_ref[...])) * _dot(hb, wsu_ref[...])
    shared = _dot(hid.astype(BF16), wsd_ref[...])
    xs_ref[...] = x + mod[:, 2 * D:3 * D] * shared


def _moe_pre(gp, gs, x_p, x_s, mod_p, mod_s, norm_g, rw_t, rb, wsg, wsu, wsd):
    D = x_p.shape[-1]
    n_p, tm, tps = gp.n_tiles, gp.tm, gp.tiles_per_seq
    assert gs.T <= tm and gp.T % gs.T == 0
    T_all = gp.T + gs.T
    clamp = lambda i: jnp.minimum(i, n_p - 1)
    consts = (norm_g, rw_t, rb, wsg, wsu, wsd)
    in_specs = [pl.BlockSpec((tm, D), lambda i: (clamp(i), 0)), _const_spec(x_s),
                pl.BlockSpec((None, 1, mod_p.shape[-1]), lambda i: (clamp(i) // tps, 0, 0)), _const_spec(mod_s)]
    in_specs += [_const_spec(a) for a in consts]
    hx, xs, cls = pl.pallas_call(
        functools.partial(_moe_pre_kernel, D=D, n_p=n_p), grid=(n_p + 1,), in_specs=in_specs,
        out_specs=[gp.tok_spec(D + LANE), gp.tok_spec(D), pl.BlockSpec((1, 1, tm), lambda i: (i, 0, 0))],
        out_shape=[jax.ShapeDtypeStruct((T_all, D + LANE), F32), jax.ShapeDtypeStruct((T_all, D), F32),
                   jax.ShapeDtypeStruct((n_p + 1, 1, tm), jnp.int32)],
        name="moe_pre", compiler_params=_cparams(1),
    )(x_p, x_s, mod_p, mod_s, *consts)
    return hx, xs, cls.reshape(-1)[:T_all]


def _moe_kernel(pos_ref, te1_ref, te2_ref, nused_ref, hx_hbm, wg1, wu1, wd1, wg2, wu2, wd2, y_ref,
                src_ref, hbuf, gsem, *, T_all, n_rows, tm, D):
    i = pl.program_id(0)
    nused = nused_ref[0]
    slot = i % 2

    def gather_row(tile, sl, r):
        t = src_ref[tile * tm + r]
        pltpu.make_async_copy(hx_hbm.at[pl.ds(t, 1)], hbuf.at[sl, pl.ds(r, 1)], gsem.at[sl]).start()

    def wait_gather(sl):
        pltpu.make_async_copy(hx_hbm.at[pl.ds(0, tm)], hbuf.at[sl], gsem.at[sl]).wait()

    def compute(sl):
        hx = hbuf[sl]
        hb = hx[:, 0:D].astype(BF16)
        wa = hx[:, D:D + 1]
        wb = hx[:, D + 1:D + 2]
        bf = lambda w_ref: w_ref[...].astype(BF16)
        hid_a = (_silu(_dot(hb, bf(wg1))) * _dot(hb, bf(wu1)) * wa).astype(BF16)
        hid_b = (_silu(_dot(hb, bf(wg2))) * _dot(hb, bf(wu2)) * wb).astype(BF16)
        y_ref[...] = _dot(hid_a, bf(wd1)) + _dot(hid_b, bf(wd2))

    @pl.when(i == 0)
    def _():
        def init(r, c):
            src_ref[r] = 0
            return c
        lax.fori_loop(0, n_rows, init, 0, unroll=8)

        def fill(t, c):
            src_ref[pos_ref[t]] = t
            return c
        lax.fori_loop(0, T_all, fill, 0, unroll=8)

        def first(r, c):
            gather_row(0, 0, r)
            return c
        lax.fori_loop(0, tm, first, 0, unroll=8)

    active = i < nused

    @pl.when(active)
    def _():
        wait_gather(slot)

    @pl.when(i + 1 < nused)
    def _():
        for r in range(tm):
            gather_row(i + 1, 1 - slot, r)
        compute(slot)

    @pl.when(i + 1 == nused)
    def _():
        compute(slot)

    @pl.when(jnp.logical_not(active))
    def _():
        y_ref[...] = jnp.zeros(y_ref.shape, F32)


def _moe_routed(hx, cls, layer, wg, wu, wd):
    T_all = hx.shape[0]
    D, Fe = wg.shape[-2:]
    tm = MOE_TILE
    n_tiles = (T_all + N_CLASSES * (tm - 1)) // tm
    n_rows = n_tiles * tm
    onehot = (cls[:, None] == jnp.arange(N_CLASSES, dtype=jnp.int32)[None, :]).astype(jnp.int32)
    counts = jnp.sum(onehot, axis=0)
    rank = jnp.sum((jnp.cumsum(onehot, axis=0) - onehot) * onehot, axis=1)
    ntile_c = (counts + tm - 1) // tm
    tile_end = jnp.cumsum(ntile_c)
    tile_off = tile_end - ntile_c
    pos = (jnp.sum(onehot * tile_off[None, :], axis=1) * tm + rank).astype(jnp.int32)
    nused = tile_end[-1]
    tiles = jnp.arange(n_tiles, dtype=jnp.int32)
    tid = jnp.minimum(tiles, nused - 1)
    tile_cls = jnp.sum((tid[:, None] >= tile_end[None, :]).astype(jnp.int32), axis=1)
    pair_lo = jnp.array([a for a, _ in _PAIRS], jnp.int32)
    pair_hi = jnp.array([b for _, b in _PAIRS], jnp.int32)
    grp_id, pair_id = tile_cls // len(_PAIRS), tile_cls % len(_PAIRS)
    te1 = (grp_id * EXPERTS_PER_GROUP + pair_lo[pair_id]).astype(jnp.int32)
    te2 = (grp_id * EXPERTS_PER_GROUP + pair_hi[pair_id]).astype(jnp.int32)

    w1 = lambda shape: pl.BlockSpec((None, None) + shape, lambda i, pos, te1, *_: (layer, te1[i], 0, 0))
    w2 = lambda shape: pl.BlockSpec((None, None) + shape, lambda i, pos, te1, te2, *_: (layer, te2[i], 0, 0))
    grid_spec = pltpu.PrefetchScalarGridSpec(
        num_scalar_prefetch=4, grid=(n_tiles,),
        in_specs=[pl.BlockSpec(memory_space=pl.ANY),
                  w1((D, Fe)), w1((D, Fe)), w1((Fe, D)), w2((D, Fe)), w2((D, Fe)), w2((Fe, D))],
        out_specs=pl.BlockSpec((tm, D), lambda i, *_: (i, 0)),
        scratch_shapes=[pltpu.SMEM((n_rows,), jnp.int32), pltpu.VMEM((2, tm, D + LANE), F32),
                        pltpu.SemaphoreType.DMA((2,))])
    y = pl.pallas_call(
        functools.partial(_moe_kernel, T_all=T_all, n_rows=n_rows, tm=tm, D=D),
        grid_spec=grid_spec, out_shape=jax.ShapeDtypeStruct((n_rows, D), F32),
        name="moe_routed", compiler_params=_cparams(1),
    )(pos, te1, te2, nused.reshape(1).astype(jnp.int32), hx, wg, wu, wd, wg, wu, wd)
    return y, pos


def _kv_kernel(xs_ref, y_ref, pmod_ref, kvmod_ref, cos_ref, sin_ref, g_ref, wc_ref, wp_ref, wr_ref, lg_ref,
               xo_ref, ckv_ref, kpe_ref, kcat_ref, *maybe_vt_ref, D, C, R):
    x = _resid(xs_ref, y_ref, pmod_ref, D)
    xo_ref[...] = x
    hn = (_rms(x, g_ref[...]) * (1.0 + kvmod_ref[:, D:2 * D]) + kvmod_ref[:, 0:D]).astype(BF16)
    ckv = _rms(_dot(hn, wc_ref[...]), lg_ref[...])
    kpe = _dot(hn, wp_ref[...]) * cos_ref[...] + _dot(hn, wr_ref[...]) * sin_ref[...]
    ckv_ref[...] = ckv
    kpe_ref[...] = kpe[:, 0:R]
    kcat_ref[:, 0:C] = ckv.astype(kcat_ref.dtype)
    kcat_ref[:, C:C + LANE] = kpe.astype(kcat_ref.dtype)
    if maybe_vt_ref:
        maybe_vt_ref[0][...] = ckv.T.astype(maybe_vt_ref[0].dtype)


def _kv_stream(grp, xs, y, y_off, pos, pmod, kvmod, cos, sin, consts, C, R, kcat_dtype, with_vt):
    D = xs.shape[-1]
    in_specs = [grp.tok_spec(D, y_off), grp.tok_spec(D, y_off), grp.mod_spec(pmod.shape[-1]),
                grp.mod_spec(kvmod.shape[-1]), grp.rope_spec(LANE), grp.rope_spec(LANE)]
    in_specs += [_const_spec(a) for a in consts]
    widths = [(D, F32), (C, F32), (R, F32), (C + LANE, kcat_dtype)]
    out_specs = [grp.tok_spec(w) for w, _ in widths]
    out_shape = [jax.ShapeDtypeStruct((grp.T, w), dt) for w, dt in widths]
    if with_vt:
        tps = grp.tiles_per_seq
        out_specs.append(pl.BlockSpec((None, C, grp.tm), lambda i: (i // tps, 0, i % tps)))
        out_shape.append(jax.ShapeDtypeStruct((grp.n_seq, C, grp.seq_len), BF16))
    return _tile_call("kv_stream", functools.partial(_kv_kernel, D=D, C=C, R=R), grp, in_specs,
                      [xs, y, pmod, kvmod, cos, sin, *consts], out_specs, out_shape, 1, pos)


def _mla_q_kernel(*refs, has_y, transposed, D, C, H, scale):
    refs = list(refs)
    xs_ref = refs.pop(0)
    y_ref = refs.pop(0) if has_y else None
    pmod_ref = refs.pop(0) if has_y else None
    mod_ref, cos_ref, sin_ref, g_ref, wdq_ref, qg_ref, wn_ref, wp_ref, wr_ref, wuk_ref = refs[:10]
    outs = refs[10:]
    if has_y:
        xo_ref, outs = outs[0], outs[1:]
    qcat_ref, = outs
    x = _resid(xs_ref, y_ref, pmod_ref, D)
    if has_y:
        xo_ref[...] = x
    h = (_rms(x, g_ref[...]) * (1.0 + mod_ref[:, D:2 * D]) + mod_ref[:, 0:D]).astype(BF16)
    cq = _rms(_dot(h, wdq_ref[...]), qg_ref[...]).astype(BF16)
    q_nope = _dot(cq, wn_ref[...]).astype(BF16)
    cos, sin = cos_ref[...], sin_ref[...]
    dn = q_nope.shape[-1] // H
    for hh in range(H):
        ls = slice(hh * LANE, (hh + 1) * LANE)
        q_lat = _dot(q_nope[:, hh * dn:(hh + 1) * dn], wuk_ref[hh])
        q_pe = _dot(cq, wp_ref[:, ls]) * cos + _dot(cq, wr_ref[:, ls]) * sin
        if transposed:
            tm = q_lat.shape[0]
            qcat_ref[0:C, hh * tm:(hh + 1) * tm] = (q_lat * scale).T.astype(qcat_ref.dtype)
            qcat_ref[C:C + LANE, hh * tm:(hh + 1) * tm] = (q_pe * scale).T.astype(qcat_ref.dtype)
        else:
            qcat_ref[hh, :, 0:C] = (q_lat * scale).astype(qcat_ref.dtype)
            qcat_ref[hh, :, C:C + LANE] = (q_pe * scale).astype(qcat_ref.dtype)


def _mla_q(grp, xs, y, y_off, pos, pmod, mod, cos, sin, norm_g, wq, out_dtype, transposed):
    D = xs.shape[-1]
    w_dq, qg, w_n, w_p, w_r, w_uk = wq
    C = w_uk.shape[-1]
    H = MLA_HEADS
    has_y = y is not None
    scale = float(w_uk.shape[1] + QK_ROPE) ** -0.5
    if transposed:
        scale *= LOG2_E
    in_specs = [grp.tok_spec(D, y_off if has_y else 0)]
    in_specs += [grp.tok_spec(D, y_off), grp.mod_spec(pmod.shape[-1])] if has_y else []
    args = [xs] + ([y, pmod] if has_y else [])
    in_specs += [grp.mod_spec(mod.shape[-1]), grp.rope_spec(LANE), grp.rope_spec(LANE)]
    args += [mod, cos, sin]
    for a in (norm_g, w_dq, qg, w_n, w_p, w_r, w_uk):
        in_specs.append(_const_spec(a)); args.append(a)
    if transposed:
        q_spec = pl.BlockSpec((None, C + LANE, H * grp.tm), lambda i: (i, 0, 0))
        q_shape = jax.ShapeDtypeStruct((grp.n_tiles, C + LANE, H * grp.tm), out_dtype)
    else:
        q_spec = pl.BlockSpec((H, grp.tm, C + LANE), lambda i: (0, i, 0))
        q_shape = jax.ShapeDtypeStruct((H, grp.T, C + LANE), out_dtype)
    out_specs = ([grp.tok_spec(D)] if has_y else []) + [q_spec]
    out_shape = ([jax.ShapeDtypeStruct((grp.T, D), F32)] if has_y else []) + [q_shape]
    res = _tile_call(
        "mla_q", functools.partial(_mla_q_kernel, has_y=has_y, transposed=transposed, D=D, C=C, H=H, scale=scale),
        grp, in_specs, args, out_specs, out_shape, 1 if has_y else None, pos)
    return (res[0], res[1]) if has_y else (xs, res[0])


def _mla_out(o_lat_heads, x, g1, wuvt_ref, wo_ref):
    parts = [_dot(o.astype(BF16), wuvt_ref[hh]).astype(BF16) for hh, o in enumerate(o_lat_heads)]
    return x + g1 * _dot(jnp.concatenate(parts, axis=-1), wo_ref[...])


def _attn_kernel(x_ref, qt_ref, k_ref, vt_ref, mod_ref, wuvt_ref, wo_ref, xo_ref, m_ref, l_ref, acc_ref, s0_ref,
                 *, D, C, H, tq, tk, cw):
    qi = pl.program_id(1)
    m_ref[...] = jnp.full(m_ref.shape, -jnp.inf, F32)
    l_ref[...] = jnp.zeros(l_ref.shape, F32)
    acc_ref[...] = jnp.zeros(acc_ref.shape, F32)
    n_col = H * tq

    def keys(j):
        return k_ref[pl.ds(pl.multiple_of(j * tk, tk), tk), :]

    s0_ref[...] = _dot(keys(0), qt_ref[:, 0:cw])

    def block(j, masked):
        kb = keys(j)
        vt = vt_ref[:, pl.ds(pl.multiple_of(j * tk, tk), tk)]
        if masked:
            kpos = lax.broadcasted_iota(jnp.int32, (tk, cw), 0)
            qpos = lax.broadcasted_iota(jnp.int32, (tk, cw), 1) & (tq - 1)
            keep = kpos <= qpos
        s_next = s0_ref[...]
        for c0 in range(0, n_col, cw):
            cs = slice(c0, c0 + cw)
            s = s_next
            if c0 + cw < n_col:
                s_next = _dot(kb, qt_ref[:, c0 + cw:c0 + 2 * cw])
            elif not masked:
                s0_ref[...] = _dot(keys(j + 1), qt_ref[:, 0:cw])
            if masked:
                s = jnp.where(keep, s, -1e30)
            m_old = m_ref[:, cs]
            m_new = jnp.maximum(m_old, jnp.max(s, axis=0, keepdims=True))
            alpha = jnp.exp2(m_old - m_new)
            p = jnp.exp2(s - m_new)
            l_ref[:, cs] = alpha * l_ref[:, cs] + jnp.sum(p, axis=0, keepdims=True)
            acc_ref[:, cs] = alpha * acc_ref[:, cs] + _dot(vt, p.astype(BF16))
            m_ref[:, cs] = m_new

    def body(j, c):
        block(j, False)
        return c
    lax.fori_loop(0, qi, body, 0)
    block(qi, True)
    parts = []
    for hh in range(H):
        cs = slice(hh * tq, (hh + 1) * tq)
        o_t = (acc_ref[:, cs] / l_ref[:, cs]).astype(BF16)
        parts.append(_dot_tn(o_t, wuvt_ref[hh]).astype(BF16))
    y = _dot(jnp.concatenate(parts, axis=-1), wo_ref[...])
    xo_ref[...] = x_ref[...] + mod_ref[:, 2 * D:3 * D] * y


def _attn_prompt(B, S, x, qt, kcat, vt, mod, w_uvt, w_o):
    D = x.shape[-1]
    H = MLA_HEADS
    _, Wq, n_col = qt.shape
    C = w_uvt.shape[1]
    tq, tk = ATT_TQ, ATT_TK
    assert tq == tk and tq & (tq - 1) == 0 and n_col == H * tq
    nq = S // tq
    cst = lambda a: pl.BlockSpec(a.shape, lambda b, i: (0,) * a.ndim)
    return pl.pallas_call(
        functools.partial(_attn_kernel, D=D, C=C, H=H, tq=tq, tk=tk, cw=ATT_CW), grid=(B, nq),
        in_specs=[pl.BlockSpec((tq, D), lambda b, i: (b * nq + i, 0)),
                  pl.BlockSpec((None, Wq, n_col), lambda b, i: (b * nq + i, 0, 0)),
                  pl.BlockSpec((S, Wq), lambda b, i: (b, 0)),
                  pl.BlockSpec((None, C, S), lambda b, i: (b, 0, 0)),
                  pl.BlockSpec((None, 1, mod.shape[-1]), lambda b, i: (b, 0, 0)), cst(w_uvt), cst(w_o)],
        out_specs=pl.BlockSpec((tq, D), lambda b, i: (b * nq + i, 0)),
        out_shape=jax.ShapeDtypeStruct(x.shape, F32),
        scratch_shapes=[pltpu.VMEM((1, n_col), F32), pltpu.VMEM((1, n_col), F32), pltpu.VMEM((C, n_col), F32),
                        pltpu.VMEM((tk, ATT_CW), F32)],
        name="attn_prompt", compiler_params=_cparams(2),
    )(x, qt, kcat, vt, mod, w_uvt, w_o)


def _attn_dec_kernel(pt_ref, q_ref, kn_ref, ckv_hbm, kpe_hbm, o_ref, cbuf, pbuf, kb_ref, s_ref, csem, psem,
                     *, C, R, n_pages, page, cp):
    b = pl.program_id(0)
    nb = pl.num_programs(0)
    slot = b % 2

    def start(seq, sl):
        def body(p, c):
            pg = pt_ref[seq * n_pages + p]
            pltpu.make_async_copy(ckv_hbm.at[pl.ds(pg, 1)], cbuf.at[sl, pl.ds(p, 1)], csem.at[sl]).start()
            pltpu.make_async_copy(kpe_hbm.at[pl.ds(pg, 1)], pbuf.at[sl, pl.ds(p, 1)], psem.at[sl]).start()
            return c
        lax.fori_loop(0, n_pages, body, 0, unroll=4)

    @pl.when(b == 0)
    def _():
        start(0, 0)

    @pl.when(b + 1 < nb)
    def _():
        start(b + 1, 1 - slot)

    pltpu.make_async_copy(ckv_hbm.at[pl.ds(0, n_pages)], cbuf.at[slot], csem.at[slot]).wait()
    pltpu.make_async_copy(kpe_hbm.at[pl.ds(0, n_pages)], pbuf.at[slot], psem.at[slot]).wait()

    q = q_ref[...]
    ql = q[:, 0:C].astype(BF16)
    qp = q[:, C:C + R].astype(BF16)
    ck = cp * page
    n_chunks = n_pages // cp
    for c in range(n_chunks):
        kc = cbuf[slot, c * cp:(c + 1) * cp].reshape(ck, C).astype(BF16)
        pct = jnp.concatenate([pbuf[slot, c * cp + i] for i in range(cp)], axis=-1).astype(BF16)
        kb_ref[c * ck:(c + 1) * ck, :] = kc
        s_ref[:, c * ck:(c + 1) * ck] = _dot_nt(ql, kc) + _dot(qp, pct)
    kn = kn_ref[...]
    s_new = jnp.sum(q * kn, axis=-1, keepdims=True)
    s = s_ref[...]
    m = jnp.maximum(jnp.max(s, axis=-1, keepdims=True), s_new)
    p = jnp.exp(s - m)
    p_new = jnp.exp(s_new - m)
    l = jnp.sum(p, axis=-1, keepdims=True) + p_new
    acc = p_new * kn[:, 0:C]
    pb = p.astype(BF16)
    for c in range(n_chunks):
        acc = acc + _dot(pb[:, c * ck:(c + 1) * ck], kb_ref[c * ck:(c + 1) * ck, :])
    o_ref[...] = acc / l


def _attn_dec(q_s, kn, cache_ckv, kpe_pages, page_table):
    B, H, Wq = q_s.shape
    _, page, C = cache_ckv.shape
    R = kpe_pages.shape[1]
    n_pages = page_table.shape[1]
    P = n_pages * page
    any_spec = pl.BlockSpec(memory_space=pl.ANY)
    grid_spec = pltpu.PrefetchScalarGridSpec(
        num_scalar_prefetch=1, grid=(B,),
        in_specs=[pl.BlockSpec((None, H, Wq), lambda b, pt: (b, 0, 0)),
                  pl.BlockSpec((None, 1, Wq), lambda b, pt: (b, 0, 0)), any_spec, any_spec],
        out_specs=pl.BlockSpec((None, H, C), lambda b, pt: (b, 0, 0)),
        scratch_shapes=[pltpu.VMEM((2, n_pages, page, C), F32), pltpu.VMEM((2, n_pages, R, page), F32),
                        pltpu.VMEM((P, C), BF16), pltpu.VMEM((H, P), F32),
                        pltpu.SemaphoreType.DMA((2,)), pltpu.SemaphoreType.DMA((2,))])
    return pl.pallas_call(
        functools.partial(_attn_dec_kernel, C=C, R=R, n_pages=n_pages, page=page, cp=DEC_CHUNK_PAGES),
        grid_spec=grid_spec, out_shape=jax.ShapeDtypeStruct((B, H, C), F32), name="attn_dec",
        compiler_params=_cparams(1),
    )(page_table.reshape(-1), q_s, kn, cache_ckv, kpe_pages)


def _mla_out_kernel(x_ref, o_ref, mod_ref, wuvt_ref, wo_ref, xo_ref, *, D, H):
    heads = [o_ref[hh] for hh in range(H)]
    xo_ref[...] = _mla_out(heads, x_ref[...], mod_ref[:, 2 * D:3 * D], wuvt_ref, wo_ref)


def _final_kernel(xs_ref, y_ref, pmod_ref, g_ref, o_ref, *, D):
    o_ref[...] = _rms(_resid(xs_ref, y_ref, pmod_ref, D), g_ref[...])


def _rope_table(pos, d, width):
    inv = ROPE_THETA ** (-jnp.arange(0, d, 2, dtype=F32) / d)
    ang = pos.astype(F32)[:, None] * inv[None, :]
    cos, sin = jnp.cos(ang), jnp.sin(ang)
    pad = jnp.zeros((pos.shape[0], width - d), F32)
    return jnp.concatenate([cos, cos, pad], axis=-1), jnp.concatenate([sin, sin, pad], axis=-1)


def _rot_half_cols(w):
    half = w.shape[-1] // 2
    return jnp.concatenate([-w[..., half:], w[..., :half]], axis=-1)


def _pad_lanes(w):
    pad = jnp.zeros(w.shape[:-1] + (LANE - w.shape[-1],), w.dtype)
    return jnp.concatenate([w, pad], axis=-1)


def _ret_tables(L):
    log_g = jnp.log1p(-jnp.exp2(-5.0 - jnp.arange(RET_HEADS, dtype=F32)))
    idx = jnp.arange(L, dtype=F32)
    diff = idx[:, None] - idx[None, :]
    dec = jnp.where(diff[None] >= 0, jnp.exp(jnp.maximum(diff, 0.0)[None] * log_g[:, None, None]), 0.0)
    qd = jnp.exp((idx + 1.0)[:, None] * log_g[None, :])
    kd = jnp.exp((L - 1.0 - idx)[:, None] * log_g[None, :])
    gl = jnp.exp(L * log_g)[None, :]
    return dec, qd, kd, gl


def kernel(x_prompt, x_sample, c_prompt, c_sample, state_ret, cache_ckv, cache_kpe, page_table, w_ada, b_ada, norm_mix_g, norm_ffn_g, ret_w_in, ret_gn_g, ret_w_out, kv_w_ada, kv_b_ada, kv_norm_g, mla_w_dkv, mla_kv_norm_g, mla_w_uk, mla_w_uv, mla_w_dq, mla_q_norm_g, mla_w_uq, mla_w_o, router_w, router_bias, moe_w_gate, moe_w_up, moe_w_down, shared_w_gate, shared_w_up, shared_w_down, final_norm_g):
    B, S, D = x_prompt.shape
    Bs, Ss, _ = x_sample.shape
    assert Ss == 1
    depth = w_ada.shape[0]
    n_a = ret_w_in.shape[0]
    H = MLA_HEADS
    C = mla_w_dkv.shape[1] - QK_ROPE
    R = QK_ROPE
    dn = mla_w_uk.shape[1]
    past_len = page_table.shape[1] * cache_ckv.shape[1]

    gp = _Group(B, S, TM_PROMPT, False)
    gs = _Group(Bs, 1, Bs, True)
    groups = (gp, gs)
    T_all = gp.T + gs.T
    ys_off = gp.T // gs.tm

    n_c = B + Bs
    n_c_pad = -(-n_c // 8) * 8
    c_all = jnp.concatenate([c_prompt, c_sample, jnp.zeros((n_c_pad - n_c, D), F32)], axis=0)
    mod_all = _ada_mod(c_all, w_ada, b_ada)
    kvmod_all = _ada_mod(c_all, kv_w_ada[None], kv_b_ada[None])[0]
    mods = [[g.mod_array(mod_all[l, lo:lo + g.n_seq]) for l in range(depth)]
            for g, lo in zip(groups, (0, B))]
    kvmods = [g.mod_array(kvmod_all[lo:lo + g.n_seq]) for g, lo in zip(groups, (0, B))]

    pos = (jnp.arange(S), past_len + jnp.arange(1))
    ret_rope = [_rope_table(p, D // RET_HEADS, D // RET_HEADS)[0:2] for p in pos]
    ret_rope = [(c[:, :c.shape[1] // 2], s[:, :s.shape[1] // 2]) for c, s in ret_rope]
    mla_rope = [_rope_table(p, R, LANE) for p in pos]
    tabs_p = _ret_tables(RET_CHUNK)
    gl_s = _ret_tables(1)[3]

    row = lambda g: g.reshape(1, -1)
    w_in = ret_w_in.astype(BF16)
    w_out = ret_w_out.astype(BF16)
    wsg, wsu, wsd = shared_w_gate.astype(BF16), shared_w_up.astype(BF16), shared_w_down.astype(BF16)
    rw_t = router_w.T
    rb = router_bias.reshape(-1, 1)
    w_dkv_c = mla_w_dkv[:, :C].astype(BF16)
    w_dkv_p = _pad_lanes(mla_w_dkv[:, C:]).astype(BF16)
    w_dkv_r = _pad_lanes(_rot_half_cols(mla_w_dkv[:, C:])).astype(BF16)
    w_uk = mla_w_uk.astype(BF16)
    w_uvt = jnp.swapaxes(mla_w_uv, 1, 2).astype(BF16)
    wq = []
    for j in range(depth - n_a):
        w3 = mla_w_uq[j].reshape(-1, H, dn + R)
        w_n = w3[:, :, :dn].reshape(-1, H * dn).astype(BF16)
        w_p = _pad_lanes(w3[:, :, dn:]).reshape(-1, H * LANE).astype(BF16)
        w_r = _pad_lanes(_rot_half_cols(w3[:, :, dn:])).reshape(-1, H * LANE).astype(BF16)
        wq.append((mla_w_dq[j].astype(BF16), row(mla_q_norm_g[j]), w_n, w_p, w_r, w_uk))
    w_o = mla_w_o.astype(BF16)

    kpe_pages = jnp.swapaxes(cache_kpe, 1, 2)

    xs = [x_prompt.reshape(gp.T, D), x_sample.reshape(gs.T, D)]
    y, ypos = None, [None, None]
    states_p, state_s = [], None
    ckv_out, kpe_out, kcat = [None, None], [None, None], [None, None]

    for l in range(depth):
        y_offs = (0, ys_off)
        xmid = [None, None]
        if l < n_a:
            for gi, g in enumerate(groups):
                pm = mods[gi][l - 1] if y is not None else None
                x, q, k, v, gt = _ret_in(g, xs[gi], y, y_offs[gi], ypos[gi], pm, mods[gi][l],
                                         ret_rope[gi][0], ret_rope[gi][1],
                                         row(norm_mix_g[l]), w_in[l], BF16 if gi == 0 else F32)
                if gi == 0:
                    xmid[gi], st = _ret_chunk(B, S, x, q, k, v, gt, mods[gi][l], tabs_p, row(ret_gn_g[l]), w_out[l])
                else:
                    state_s, o = _ret_dec(state_ret if l == 0 else state_s, l, q, k, v, gl_s, in_place=l > 0)
                    xmid[gi], = _tok_call("ret_post", functools.partial(_ret_post_kernel, D=D, H=RET_HEADS), g,
                                          [(x, 0), (o, 0), (gt, 0)], [mods[gi][l]], [],
                                          [row(ret_gn_g[l]), w_out[l]], [(D, F32)])
                if gi == 0:
                    states_p.append(st)
        else:
            j = l - n_a
            for gi, g in enumerate(groups):
                xsg, yy, pm = xs[gi], y, mods[gi][l - 1]
                if l == n_a:
                    res = _kv_stream(g, xsg, y, y_offs[gi], ypos[gi], pm, kvmods[gi],
                                     mla_rope[gi][0], mla_rope[gi][1],
                                     [row(kv_norm_g), w_dkv_c, w_dkv_p, w_dkv_r, row(mla_kv_norm_g)], C, R,
                                     BF16 if gi == 0 else F32, gi == 0)
                    xsg, ckv_out[gi], kpe_out[gi], kcat[gi] = res[:4]
                    if gi == 0:
                        vt_p = res[4]
                    yy, pm = None, None
                x, qcat = _mla_q(g, xsg, yy, y_offs[gi], ypos[gi], pm, mods[gi][l],
                                 mla_rope[gi][0], mla_rope[gi][1],
                                 row(norm_mix_g[l]), wq[j], BF16 if gi == 0 else F32, gi == 0)
                if gi == 0:
                    xmid[gi] = _attn_prompt(B, S, x, qcat, kcat[gi], vt_p, mods[gi][l], w_uvt, w_o[j])
                else:
                    o_lat = _attn_dec(jnp.swapaxes(qcat, 0, 1), kcat[gi][:, None, :], cache_ckv, kpe_pages, page_table)
                    xmid[gi] = pl.pallas_call(
                        functools.partial(_mla_out_kernel, D=D, H=H), grid=(1,),
                        in_specs=[_const_spec(x), pl.BlockSpec((H, Bs, C), lambda i: (0, 0, 0)),
                                  _const_spec(mods[gi][l]), _const_spec(w_uvt), _const_spec(w_o[j])],
                        out_specs=_const_spec(x), out_shape=jax.ShapeDtypeStruct(x.shape, F32),
                        name="mla_out", compiler_params=_cparams(1),
                    )(x, jnp.swapaxes(o_lat, 0, 1), mods[gi][l], w_uvt, w_o[j])
        hx, xs_all, cls = _moe_pre(gp, gs, xmid[0], xmid[1], mods[0][l], mods[1][l], row(norm_ffn_g[l]), rw_t, rb,
                                   wsg[l], wsu[l], wsd[l])
        xs = [xs_all, xs_all]
        y, pos = _moe_routed(hx, cls, l, moe_w_gate, moe_w_up, moe_w_down)
        ypos = [pos[:gp.T], pos[gp.T:]]

    outs = []
    for gi, g in enumerate(groups):
        o, = _tok_call("final_norm", functools.partial(_final_kernel, D=D), g,
                       [(xs[gi], (0, ys_off)[gi]), (y, 0)],
                       [mods[gi][depth - 1]], [], [row(final_norm_g)], [(D, F32)], 1, ypos[gi])
        outs.append(o)
    return (outs[0].reshape(B, S, D), outs[1].reshape(Bs, 1, D),
            jnp.stack(states_p), state_s,
            ckv_out[0].reshape(B, S, C), kpe_out[0].reshape(B, S, R),
            ckv_out[1].reshape(Bs, 1, C), kpe_out[1].reshape(Bs, 1, R))
```

```python
import functools

import jax
import jax.numpy as jnp
from jax import lax
from jax.experimental import pallas as pl
from jax.experimental.pallas import tpu as pltpu

F32 = jnp.float32
BF16 = jnp.bfloat16

RET_HEADS = 4
MLA_HEADS = 8
QK_ROPE = 64
N_GROUPS = 4
EXPERTS_PER_GROUP = 4
ROPE_THETA = 10000.0
EPS = 1e-6
LOG2_E = 1.4426950408889634

_PAIRS = ((0, 1), (0, 2), (0, 3), (1, 2), (1, 3), (2, 3))
N_CLASSES = N_GROUPS * len(_PAIRS)

LANE = 128
VMEM_LIMIT_BYTES = 56 * 2 ** 20
TM_PROMPT = 256
MOE_TILE = 256
RET_CHUNK = 256
ATT_TQ = 256
ATT_TK = 256
ATT_CW = 512
DEC_NB = 8
DEC_CHUNK_PAGES = 8


def _dot(a, b):
    return jnp.dot(a, b, preferred_element_type=F32)


def _dot_nt(a, b):
    return lax.dot_general(a, b, (((1,), (1,)), ((), ())), preferred_element_type=F32)


def _dot_tn(a, b):
    return lax.dot_general(a, b, (((0,), (0,)), ((), ())), preferred_element_type=F32)


def _silu(x):
    return x * jax.nn.sigmoid(x)


def _rms(x, g):
    return x * lax.rsqrt(jnp.mean(x * x, axis=-1, keepdims=True) + EPS) * g


def _cparams(n_axes=1):
    return pltpu.CompilerParams(dimension_semantics=("arbitrary",) * n_axes,
                                vmem_limit_bytes=VMEM_LIMIT_BYTES)


class _Group:
    def __init__(self, n_seq, seq_len, tm, per_token_mod):
        self.n_seq, self.seq_len, self.tm = n_seq, seq_len, tm
        self.T = n_seq * seq_len
        self.n_tiles = self.T // tm
        self.per_token_mod = per_token_mod
        self.tiles_per_seq = max(seq_len // tm, 1)

    def tok_spec(self, width, offset=0):
        return pl.BlockSpec((self.tm, width), lambda i: (i + offset, 0))

    def mod_spec(self, width):
        if self.per_token_mod:
            return pl.BlockSpec((self.tm, width), lambda i: (i, 0))
        tps = self.tiles_per_seq
        return pl.BlockSpec((None, 1, width), lambda i: (i // tps, 0, 0))

    def rope_spec(self, width):
        if self.per_token_mod:
            return pl.BlockSpec((1, width), lambda i: (0, 0))
        tps = self.tiles_per_seq
        return pl.BlockSpec((self.tm, width), lambda i: (i % tps, 0))

    def mod_array(self, m):
        return m if self.per_token_mod else m[:, None, :]


def _const_spec(a):
    nd = a.ndim
    return pl.BlockSpec(a.shape, lambda i: (0,) * nd)


def _tile_call(name, body, grp, in_specs, args, out_specs, out_shape, y_index=None, pos=None):
    if y_index is None or pos is None:
        return pl.pallas_call(body, grid=(grp.n_tiles,), in_specs=in_specs, out_specs=out_specs,
                              out_shape=out_shape, name=name, compiler_params=_cparams(1))(*args)
    tm, D = grp.tm, args[y_index].shape[-1]

    def lift(spec):
        return pl.BlockSpec(spec.block_shape, lambda i, pos_ref, _map=spec.index_map: _map(i))
    in_specs = [pl.BlockSpec(memory_space=pl.ANY) if k == y_index else lift(s) for k, s in enumerate(in_specs)]
    out_specs = [lift(s) for s in out_specs]
    n_refs = len(args) + len(out_specs)

    def kern(pos_ref, *refs):
        main, (ybuf, sem) = list(refs[:n_refs]), refs[n_refs:]
        y_hbm = main[y_index]
        i = pl.program_id(0)
        n = pl.num_programs(0)
        slot = i % 2

        def start_row(tile, sl, r):
            p = pos_ref[tile * tm + r]
            pltpu.make_async_copy(y_hbm.at[pl.ds(p, 1)], ybuf.at[sl, pl.ds(r, 1)], sem.at[sl]).start()

        @pl.when(i == 0)
        def _():
            def step(r, c):
                start_row(0, 0, r)
                return c
            lax.fori_loop(0, tm, step, 0, unroll=8)

        pltpu.make_async_copy(y_hbm.at[pl.ds(0, tm)], ybuf.at[slot], sem.at[slot]).wait()
        main[y_index] = ybuf.at[slot]

        @pl.when(i + 1 < n)
        def _():
            for r in range(tm):
                start_row(i + 1, 1 - slot, r)
            body(*main)

        @pl.when(i + 1 == n)
        def _():
            body(*main)

    grid_spec = pltpu.PrefetchScalarGridSpec(
        num_scalar_prefetch=1, grid=(grp.n_tiles,), in_specs=in_specs, out_specs=out_specs,
        scratch_shapes=[pltpu.VMEM((2, tm, D), F32), pltpu.SemaphoreType.DMA((2,))])
    return pl.pallas_call(kern, grid_spec=grid_spec, out_shape=out_shape, name=name,
                          compiler_params=_cparams(1))(pos, *args)


def _tok_call(name, body, grp, tok_ins, mod_ins, rope_ins, const_ins, outs, y_index=None, pos=None):
    in_specs, args = [], []
    for a, off in tok_ins:
        in_specs.append(grp.tok_spec(a.shape[-1], off)); args.append(a)
    for a in mod_ins:
        in_specs.append(grp.mod_spec(a.shape[-1])); args.append(a)
    for a in rope_ins:
        in_specs.append(grp.rope_spec(a.shape[-1])); args.append(a)
    for a in const_ins:
        in_specs.append(_const_spec(a)); args.append(a)
    out_specs = [grp.tok_spec(width) for width, _ in outs]
    out_shape = [jax.ShapeDtypeStruct((grp.T, width), dtype) for width, dtype in outs]
    return _tile_call(name, body, grp, in_specs, args, out_specs, out_shape, y_index, pos)


def _mod_kernel(c_ref, w_ref, b_ref, o_ref):
    c = c_ref[...]
    o_ref[...] = _dot(_silu(c).astype(BF16), w_ref[...].astype(BF16)) + b_ref[...]


def _ada_mod(c_all, w, b):
    L, D, N = w.shape
    Bc = c_all.shape[0]
    tn = min(N, 2048)
    return pl.pallas_call(
        _mod_kernel, grid=(L, N // tn),
        in_specs=[pl.BlockSpec((Bc, D), lambda l, j: (0, 0)),
                  pl.BlockSpec((None, D, tn), lambda l, j: (l, 0, j)),
                  pl.BlockSpec((None, 1, tn), lambda l, j: (l, 0, j))],
        out_specs=pl.BlockSpec((None, Bc, tn), lambda l, j: (l, 0, j)),
        out_shape=jax.ShapeDtypeStruct((L, Bc, N), F32),
        name="ada_mod", compiler_params=_cparams(2),
    )(c_all, w, b[:, None, :])


def _resid(xs_ref, y_ref, pmod_ref, D):
    x = xs_ref[...]
    if y_ref is not None:
        x = x + pmod_ref[:, 5 * D:6 * D] * y_ref[...]
    return x


def _ret_in_kernel(*refs, has_y, D, H):
    refs = list(refs)
    xs_ref = refs.pop(0)
    y_ref = refs.pop(0) if has_y else None
    pmod_ref = refs.pop(0) if has_y else None
    mod_ref, cos_ref, sin_ref, g_ref, w_ref = refs[:5]
    outs = refs[5:]
    if has_y:
        xo_ref, outs = outs[0], outs[1:]
    q_ref, k_ref, v_ref, gt_ref = outs
    x = _resid(xs_ref, y_ref, pmod_ref, D)
    if has_y:
        xo_ref[...] = x
    h = _rms(x, g_ref[...]) * (1.0 + mod_ref[:, D:2 * D]) + mod_ref[:, 0:D]
    hb = h.astype(BF16)
    dk = D // H
    half = dk // 2
    cos, sin = cos_ref[...], sin_ref[...]
    for idx, (o_ref, scale) in enumerate(((q_ref, float(dk) ** -0.5), (k_ref, 1.0))):
        t = _dot(hb, w_ref[:, idx * D:(idx + 1) * D])
        for hh in range(H):
            x1 = t[:, hh * dk:hh * dk + half]
            x2 = t[:, hh * dk + half:(hh + 1) * dk]
            o_ref[:, hh * dk:hh * dk + half] = ((x1 * cos - x2 * sin) * scale).astype(o_ref.dtype)
            o_ref[:, hh * dk + half:(hh + 1) * dk] = ((x1 * sin + x2 * cos) * scale).astype(o_ref.dtype)
    v_ref[...] = _dot(hb, w_ref[:, 2 * D:3 * D]).astype(v_ref.dtype)
    gt_ref[...] = _dot(hb, w_ref[:, 3 * D:4 * D])


def _ret_in(grp, xs, y, y_off, pos, pmod, mod, cos, sin, norm_g, w_in, qkv_dtype):
    D = xs.shape[-1]
    has_y = y is not None
    tok = [(xs, y_off if has_y else 0)] + ([(y, y_off)] if has_y else [])
    mods = ([pmod] if has_y else []) + [mod]
    outs = ([(D, F32)] if has_y else []) + [(D, qkv_dtype)] * 3 + [(D, F32)]
    res = _tok_call("ret_in", functools.partial(_ret_in_kernel, has_y=has_y, D=D, H=RET_HEADS), grp,
                    tok, mods, [cos, sin], [norm_g, w_in], outs, 1 if has_y else None, pos)
    if not has_y:
        res = [xs] + list(res)
    return res


def _gn_gate(o, g, gn):
    mu = jnp.mean(o, axis=-1, keepdims=True)
    c = o - mu
    var = jnp.mean(c * c, axis=-1, keepdims=True)
    return _silu(g) * (c * lax.rsqrt(var + EPS) * gn)


def _ret_chunk_kernel(x_ref, q_ref, k_ref, v_ref, gt_ref, mod_ref, dec_ref, qd_ref, kd_ref, gl_ref,
                      gn_ref, wo_ref, xo_ref, s_ref, gat_ref, *, D, H):
    @pl.when(pl.program_id(1) == 0)
    def _():
        s_ref[...] = jnp.zeros_like(s_ref)

    dk = D // H
    for h in range(H):
        hs = slice(h * dk, (h + 1) * dk)
        qh, kh, vh = q_ref[:, hs], k_ref[:, hs], v_ref[:, hs]
        S = s_ref[h]
        scores = _dot_nt(qh, kh) * dec_ref[h]
        inner = _dot(scores.astype(BF16), vh)
        cross = _dot((qh.astype(F32) * qd_ref[:, h:h + 1]).astype(BF16), S.astype(BF16))
        kdec = (kh.astype(F32) * kd_ref[:, h:h + 1]).astype(BF16)
        s_ref[h] = gl_ref[:, h:h + 1] * S + _dot_tn(kdec, vh)
        gat_ref[:, hs] = _gn_gate(inner + cross, gt_ref[:, hs], gn_ref[:, hs]).astype(BF16)
    y = _dot(gat_ref[...], wo_ref[...])
    xo_ref[...] = x_ref[...] + mod_ref[:, 2 * D:3 * D] * y


def _ret_chunk(B, S, x, q, k, v, gt, mod, tabs, gn_g, w_out):
    D = x.shape[-1]
    H = RET_HEADS
    L = RET_CHUNK
    nc = S // L
    dk = D // H
    dec, qd, kd, gl = tabs
    tok = lambda w: pl.BlockSpec((L, w), lambda b, c: (b * nc + c, 0))
    cst = lambda a: pl.BlockSpec(a.shape, lambda b, c: (0,) * a.ndim)
    return pl.pallas_call(
        functools.partial(_ret_chunk_kernel, D=D, H=H), grid=(B, nc),
        in_specs=[tok(D)] * 5 + [pl.BlockSpec((None, 1, mod.shape[-1]), lambda b, c: (b, 0, 0)),
                                 cst(dec), cst(qd), cst(kd), cst(gl), cst(gn_g), cst(w_out)],
        out_specs=[tok(D), pl.BlockSpec((None, H, dk, dk), lambda b, c: (b, 0, 0, 0))],
        out_shape=[jax.ShapeDtypeStruct(x.shape, F32), jax.ShapeDtypeStruct((B, H, dk, dk), F32)],
        scratch_shapes=[pltpu.VMEM((L, D), BF16)],
        name="ret_chunk", compiler_params=_cparams(2),
    )(x, q, k, v, gt, mod, dec, qd, kd, gl, gn_g, w_out)


def _ret_dec_kernel(st_ref, q_ref, k_ref, v_ref, gl_ref, so_ref, o_ref, *, D, H, nb):
    is_update = pl.program_id(0) == pl.num_programs(0) - 1

    @pl.when(jnp.logical_not(is_update))
    def _():
        so_ref[...] = st_ref[...]
        o_ref[...] = jnp.zeros(o_ref.shape, F32)

    @pl.when(is_update)
    def _():
        _ret_dec_update(st_ref, q_ref, k_ref, v_ref, gl_ref, so_ref, o_ref, D=D, H=H, nb=nb)


def _ret_dec_update(st_ref, q_ref, k_ref, v_ref, gl_ref, so_ref, o_ref, *, D, H, nb):
    dk = D // H
    for h in range(H):
        hs = slice(h * dk, (h + 1) * dk)
        gh = gl_ref[:, h:h + 1]
        qh, kh, vh = q_ref[:, hs], k_ref[:, hs], v_ref[:, hs]
        inner = jnp.sum(qh * kh, axis=-1, keepdims=True) * vh
        qT = (qh * gh).T
        kT = kh.T
        rows = []
        for j in range(nb):
            S = st_ref[j, h]
            so_ref[j, h] = gh * S + kT[:, j:j + 1] * vh[j:j + 1, :]
            rows.append(jnp.sum(qT[:, j:j + 1] * S, axis=0, keepdims=True))
        o_ref[:, hs] = inner + jnp.concatenate(rows, axis=0)


def _ret_dec(state, layer, q, k, v, gl, in_place):
    L, B, H, dk, dv = state.shape
    D = q.shape[-1]
    nb = DEC_NB
    phases = 1 if in_place else L
    which = lambda p: (layer + 1 + p) % L if not in_place else layer
    row = pl.BlockSpec((nb, D), lambda p, i: (i, 0))
    st_spec = pl.BlockSpec((None, nb, H, dk, dv), lambda p, i: (which(p), i, 0, 0, 0))
    new_state, o = pl.pallas_call(
        functools.partial(_ret_dec_kernel, D=D, H=H, nb=nb), grid=(phases, B // nb),
        in_specs=[st_spec, row, row, row, pl.BlockSpec(gl.shape, lambda p, i: (0, 0))],
        out_specs=[st_spec, pl.BlockSpec((None, nb, D), lambda p, i: (p, i, 0))],
        out_shape=[jax.ShapeDtypeStruct(state.shape, F32), jax.ShapeDtypeStruct((phases, B, D), F32)],
        input_output_aliases={0: 0} if in_place else {},
        name="ret_dec", compiler_params=_cparams(2),
    )(state, q, k, v, gl)
    return new_state, o[phases - 1]


def _ret_post_kernel(x_ref, o_ref, gt_ref, mod_ref, gn_ref, wo_ref, xo_ref, *, D, H):
    dk = D // H
    parts = []
    for h in range(H):
        hs = slice(h * dk, (h + 1) * dk)
        parts.append(_gn_gate(o_ref[:, hs], gt_ref[:, hs], gn_ref[:, hs]).astype(BF16))
    y = _dot(jnp.concatenate(parts, axis=-1), wo_ref[...])
    xo_ref[...] = x_ref[...] + mod_ref[:, 2 * D:3 * D] * y


def _route(logits_t, bias):
    scores = jax.nn.sigmoid(logits_t)
    sel = scores + bias
    n_e = EXPERTS_PER_GROUP
    row = lambda a, e: a[e:e + 1, :]
    gscore = []
    for g in range(N_GROUPS):
        best2 = None
        for a, b in _PAIRS:
            s = row(sel, n_e * g + a) + row(sel, n_e * g + b)
            best2 = s if best2 is None else jnp.maximum(best2, s)
        gscore.append(best2)
    best, bestv = jnp.zeros_like(gscore[0], dtype=jnp.int32), gscore[0]
    for g in range(1, N_GROUPS):
        upd = gscore[g] > bestv
        best = jnp.where(upd, g, best)
        bestv = jnp.where(upd, gscore[g], bestv)

    def pick(a, j):
        out = row(a, j)
        for g in range(1, N_GROUPS):
            out = jnp.where(best == g, row(a, n_e * g + j), out)
        return out

    v = [pick(sel, j) for j in range(n_e)]
    sc = [pick(scores, j) for j in range(n_e)]
    i1, v1 = jnp.zeros_like(best), v[0]
    for j in range(1, n_e):
        upd = v[j] > v1
        i1 = jnp.where(upd, j, i1)
        v1 = jnp.where(upd, v[j], v1)
    i2, v2 = None, None
    for j in range(n_e):
        ok = i1 != j
        if i2 is None:
            i2 = jnp.where(ok, j, n_e)
            v2 = jnp.where(ok, v[j], -jnp.inf)
        else:
            upd = ok & (v[j] > v2)
            i2 = jnp.where(upd, j, i2)
            v2 = jnp.where(upd, v[j], v2)
    s1 = sc[0]
    s2 = sc[0]
    for j in range(1, n_e):
        s1 = jnp.where(i1 == j, sc[j], s1)
        s2 = jnp.where(i2 == j, sc[j], s2)
    tot = s1 + s2
    w1, w2 = s1 / tot, s2 / tot
    lo = jnp.minimum(i1, i2)
    hi = jnp.maximum(i1, i2)
    base = jnp.where(lo == 0, 0, jnp.where(lo == 1, 3, 5))
    cls = best * len(_PAIRS) + base + hi - lo - 1
    first_is_lo = i1 < i2
    return cls, jnp.where(first_is_lo, w1, w2), jnp.where(first_is_lo, w2, w1)


def _ffn_input(xp_ref, xsm_ref, modp_ref, mods_ref, g_ref, D, n_p):
    is_s = pl.program_id(0) == n_p
    tm, ts = xp_ref.shape[0], xsm_ref.shape[0]
    pad_rows = lambda a: jnp.concatenate([a, jnp.zeros((tm - ts, a.shape[1]), a.dtype)], axis=0)
    x = jnp.where(is_s, pad_rows(xsm_ref[...]), xp_ref[...])
    mod = jnp.where(is_s, pad_rows(mods_ref[:, 3 * D:6 * D]), modp_ref[:, 3 * D:6 * D])
    h = _rms(x, g_ref[...]) * (1.0 + mod[:, D:2 * D]) + mod[:, 0:D]
    return x, mod, h


def _router_kernel(xp_ref, xsm_ref, modp_ref, mods_ref, g_ref, rwt_ref, rb_ref, cls_ref, gate_ref, *, D, n_p):
    _, _, h = _ffn_input(xp_ref, xsm_ref, modp_ref, mods_ref, g_ref, D, n_p)
    hb = h.astype(BF16)
    h_lo = (h - hb.astype(F32)).astype(BF16)
    rw = rwt_ref[...]
    rw_hi = rw.astype(BF16)
    rw_lo = (rw - rw_hi.astype(F32)).astype(BF16)
    logits_t = _dot_nt(rw_hi, hb) + (_dot_nt(rw_lo, hb) + _dot_nt(rw_hi, h_lo))
    cls, wa, wb = _route(logits_t, rb_ref[...])
    cls_ref[0] = cls
    rowi = lax.broadcasted_iota(jnp.int32, gate_ref.shape, 0)
    gate_ref[...] = jnp.where(rowi == 0, wa, jnp.where(rowi == 1, wb, 0.0))


def _moe_pre_kernel(pos_ref, xp_ref, xsm_ref, modp_ref, mods_ref, g_ref, gate_ref, wsg_ref, wsu_ref, wsd_ref,
                    hx_init, hx_hbm, xs_ref, hbuf, sem, *, D, n_p):
    del hx_init
    i = pl.program_id(0)
    slot = i % 2
    tm, ts = xp_ref.shape[0], xsm_ref.shape[0]

    def start_rows(n):
        for r in range(n):
            p = pos_ref[i * tm + r]
            pltpu.make_async_copy(hbuf.at[slot, pl.ds(r, 1)], hx_hbm.at[pl.ds(p, 1)], sem.at[slot]).start()

    def wait_rows(sl, n):
        pltpu.make_async_copy(hbuf.at[sl, pl.ds(0, n)], hx_hbm.at[pl.ds(0, n)], sem.at[sl]).wait()

    @pl.when(i >= 2)
    def _():
        wait_rows(slot, tm)
    x, mod, h = _ffn_input(xp_ref, xsm_ref, modp_ref, mods_ref, g_ref, D, n_p)
    gates = gate_ref[...]
    rowi = lax.broadcasted_iota(jnp.int32, (LANE, tm), 0)
    extra = jnp.where(rowi == 0, gates[0:1], jnp.where(rowi == 1, gates[1:2], 0.0)).T
    hbuf[slot, :, 0:D] = h
    hbuf[slot, :, D:D + LANE] = extra

    def shared_ffn():
        hb = h.astype(BF16)
        hid = _silu(_dot(hb, wsg_ref[...])) * _dot(hb, wsu_ref[...])
        xs_ref[...] = x + mod[:, 2 * D:3 * D] * _dot(hid.astype(BF16), wsd_ref[...])

    @pl.when(i < n_p)
    def _():
        start_rows(tm)
        shared_ffn()

    @pl.when(i == n_p)
    def _():
        start_rows(ts)
        shared_ffn()
        wait_rows(slot, ts)

        @pl.when(i >= 1)
        def _():
            wait_rows(1 - slot, tm)


def _router(gp, gs, x_p, x_s, mod_p, mod_s, norm_g, rw_t, rb):
    D = x_p.shape[-1]
    n_p, tm = gp.n_tiles, gp.tm
    assert gs.T <= tm and gp.T % gs.T == 0
    consts = (norm_g, rw_t, rb)
    cls, gates = pl.pallas_call(
        functools.partial(_router_kernel, D=D, n_p=n_p), grid=(n_p + 1,),
        in_specs=_ffn_input_specs(gp, x_s, mod_p, mod_s, lambda i: i) + [_const_spec(a) for a in consts],
        out_specs=[pl.BlockSpec((1, 1, tm), lambda i: (i, 0, 0)), pl.BlockSpec((None, 8, tm), lambda i: (i, 0, 0))],
        out_shape=[jax.ShapeDtypeStruct((n_p + 1, 1, tm), jnp.int32), jax.ShapeDtypeStruct((n_p + 1, 8, tm), F32)],
        name="router", compiler_params=_cparams(1),
    )(x_p, x_s, mod_p, mod_s, *consts)
    return cls.reshape(-1), gates


def _ffn_input_specs(gp, x_s, mod_p, mod_s, first):
    n_p, tm, tps = gp.n_tiles, gp.tm, gp.tiles_per_seq
    D = x_s.shape[-1]
    clamp = lambda i: jnp.minimum(i, n_p - 1)
    return [pl.BlockSpec((tm, D), lambda *a: (clamp(first(*a)), 0)),
            pl.BlockSpec(x_s.shape, lambda *a: (0, 0)),
            pl.BlockSpec((None, 1, mod_p.shape[-1]), lambda *a: (clamp(first(*a)) // tps, 0, 0)),
            pl.BlockSpec(mod_s.shape, lambda *a: (0, 0))]


def _moe_pre(gp, gs, x_p, x_s, mod_p, mod_s, norm_g, gates, pos_pad, n_rows, wsg, wsu, wsd):
    D = x_p.shape[-1]
    n_p, tm = gp.n_tiles, gp.tm
    T_all = gp.T + gs.T
    W = D + LANE
    take_i = lambda i, pos: i
    cst = lambda a: pl.BlockSpec(a.shape, lambda i, pos: (0,) * a.ndim)
    in_specs = _ffn_input_specs(gp, x_s, mod_p, mod_s, take_i)
    in_specs += [cst(norm_g), pl.BlockSpec((None, 8, tm), lambda i, pos: (i, 0, 0)), cst(wsg), cst(wsu), cst(wsd),
                 pl.BlockSpec(memory_space=pl.ANY)]
    grid_spec = pltpu.PrefetchScalarGridSpec(
        num_scalar_prefetch=1, grid=(n_p + 1,), in_specs=in_specs,
        out_specs=[pl.BlockSpec(memory_space=pl.ANY), pl.BlockSpec((tm, D), lambda i, pos: (i, 0))],
        scratch_shapes=[pltpu.VMEM((2, tm, W), F32), pltpu.SemaphoreType.DMA((2,))])
    hx, xs = pl.pallas_call(
        functools.partial(_moe_pre_kernel, D=D, n_p=n_p), grid_spec=grid_spec,
        out_shape=[jax.ShapeDtypeStruct((n_rows, W), F32), jax.ShapeDtypeStruct((T_all, D), F32)],
        input_output_aliases={10: 0}, name="moe_pre", compiler_params=_cparams(1),
    )(pos_pad, x_p, x_s, mod_p, mod_s, norm_g, gates, wsg, wsu, wsd, jnp.zeros((n_rows, W), F32))
    return hx, xs


def _moe_kernel(pos_ref, te1_ref, te2_ref, nv_ref, dump_ref, nused_ref, hx_ref, wg1, wu1, wd1, wg2, wu2, wd2,
                y_hbm, src_ref, obuf, ssem, *, T_all, n_rows, tm, D):
    i = pl.program_id(0)
    nused = nused_ref[0]
    slot = i % 2

    def scatter_row(tile, sl, r):
        nv = nv_ref[tile]
        t = jnp.where(r < nv, src_ref[tile * tm + r], dump_ref[tile] + (r - nv))
        pltpu.make_async_copy(obuf.at[sl, pl.ds(r, 1)], y_hbm.at[pl.ds(t, 1)], ssem.at[sl]).start()

    def rolled_scatter(tile, sl):
        def step(r, c):
            scatter_row(tile, sl, r)
            return c
        lax.fori_loop(0, tm, step, 0, unroll=8)

    def wait_scatter(sl):
        pltpu.make_async_copy(obuf.at[sl], y_hbm.at[pl.ds(0, tm)], ssem.at[sl]).wait()

    def compute(sl):
        hx = hx_ref[...]
        hb = hx[:, 0:D].astype(BF16)
        wa = hx[:, D:D + 1]
        wb = hx[:, D + 1:D + 2]
        bf = lambda w_ref: w_ref[...].astype(BF16)
        hid_a = (_silu(_dot(hb, bf(wg1))) * _dot(hb, bf(wu1)) * wa).astype(BF16)
        hid_b = (_silu(_dot(hb, bf(wg2))) * _dot(hb, bf(wu2)) * wb).astype(BF16)
        obuf[sl] = _dot(hid_a, bf(wd1)) + _dot(hid_b, bf(wd2))

    @pl.when(i == 0)
    def _():
        def init(r, c):
            src_ref[r] = 0
            return c
        lax.fori_loop(0, n_rows, init, 0, unroll=8)

        def fill(t, c):
            src_ref[pos_ref[t]] = t
            return c
        lax.fori_loop(0, T_all, fill, 0, unroll=8)

    active = i < nused
    steady = jnp.logical_and(i >= 1, i + 1 < nused)

    @pl.when(jnp.logical_and(active, i >= 2))
    def _():
        wait_scatter(slot)

    @pl.when(steady)
    def _():
        compute(slot)
        for r in range(tm):
            scatter_row(i - 1, 1 - slot, r)

    @pl.when(jnp.logical_and(active, jnp.logical_not(steady)))
    def _():
        compute(slot)

        @pl.when(i >= 1)
        def _():
            rolled_scatter(i - 1, 1 - slot)

        @pl.when(i == nused - 1)
        def _():
            rolled_scatter(i, slot)
            wait_scatter(slot)

            @pl.when(i >= 1)
            def _():
                wait_scatter(1 - slot)

    @pl.when(jnp.logical_not(active))
    def _():
        obuf[slot] = jnp.zeros(obuf.shape[1:], F32)
        rolled_scatter(i, slot)
        wait_scatter(slot)


def _moe_plan(cls, T_all):
    tm = MOE_TILE
    n_tiles = (T_all + N_CLASSES * (tm - 1)) // tm
    n_rows = n_tiles * tm
    cls = cls[:T_all]
    onehot = (cls[:, None] == jnp.arange(N_CLASSES, dtype=jnp.int32)[None, :]).astype(jnp.int32)
    counts = jnp.sum(onehot, axis=0)
    rank = jnp.sum((jnp.cumsum(onehot, axis=0) - onehot) * onehot, axis=1)
    ntile_c = (counts + tm - 1) // tm
    tile_end = jnp.cumsum(ntile_c)
    tile_off = tile_end - ntile_c
    pos = (jnp.sum(onehot * tile_off[None, :], axis=1) * tm + rank).astype(jnp.int32)
    nused = tile_end[-1]
    tiles = jnp.arange(n_tiles, dtype=jnp.int32)
    tid = jnp.minimum(tiles, nused - 1)
    tile_cls = jnp.sum((tid[:, None] >= tile_end[None, :]).astype(jnp.int32), axis=1)
    nvalid = jnp.clip(counts[tile_cls] - (tiles - tile_off[tile_cls]) * tm, 0, tm)
    nvalid = jnp.where(tiles < nused, nvalid, 0).astype(jnp.int32)
    dump = (T_all + tiles * tm - (jnp.cumsum(nvalid) - nvalid)).astype(jnp.int32)
    pair_lo = jnp.array([a for a, _ in _PAIRS], jnp.int32)
    pair_hi = jnp.array([b for _, b in _PAIRS], jnp.int32)
    grp_id, pair_id = tile_cls // len(_PAIRS), tile_cls % len(_PAIRS)
    te1 = (grp_id * EXPERTS_PER_GROUP + pair_lo[pair_id]).astype(jnp.int32)
    te2 = (grp_id * EXPERTS_PER_GROUP + pair_hi[pair_id]).astype(jnp.int32)
    return n_rows, (pos, te1, te2, nvalid, dump, nused.reshape(1).astype(jnp.int32))


def _moe_routed(hx, plan, T_all, layer, wg, wu, wd):
    n_rows, W = hx.shape
    D, Fe = wg.shape[-2:]
    tm = MOE_TILE
    w1 = lambda shape: pl.BlockSpec((None, None) + shape, lambda i, pos, te1, *_: (layer, te1[i], 0, 0))
    w2 = lambda shape: pl.BlockSpec((None, None) + shape, lambda i, pos, te1, te2, *_: (layer, te2[i], 0, 0))
    grid_spec = pltpu.PrefetchScalarGridSpec(
        num_scalar_prefetch=6, grid=(n_rows // tm,),
        in_specs=[pl.BlockSpec((tm, W), lambda i, *_: (i, 0)),
                  w1((D, Fe)), w1((D, Fe)), w1((Fe, D)), w2((D, Fe)), w2((D, Fe)), w2((Fe, D))],
        out_specs=pl.BlockSpec(memory_space=pl.ANY),
        scratch_shapes=[pltpu.SMEM((n_rows,), jnp.int32), pltpu.VMEM((2, tm, D), F32),
                        pltpu.SemaphoreType.DMA((2,))])
    return pl.pallas_call(
        functools.partial(_moe_kernel, T_all=T_all, n_rows=n_rows, tm=tm, D=D),
        grid_spec=grid_spec, out_shape=jax.ShapeDtypeStruct((n_rows, D), F32),
        name="moe_routed", compiler_params=_cparams(1),
    )(*plan, hx, wg, wu, wd, wg, wu, wd)


def _kv_kernel(xs_ref, y_ref, pmod_ref, kvmod_ref, cos_ref, sin_ref, g_ref, wc_ref, wp_ref, wr_ref, lg_ref,
               xo_ref, ckv_ref, kpe_ref, kcat_ref, *maybe_vt_ref, D, C, R):
    x = _resid(xs_ref, y_ref, pmod_ref, D)
    xo_ref[...] = x
    hn = (_rms(x, g_ref[...]) * (1.0 + kvmod_ref[:, D:2 * D]) + kvmod_ref[:, 0:D]).astype(BF16)
    ckv = _rms(_dot(hn, wc_ref[...]), lg_ref[...])
    kpe = _dot(hn, wp_ref[...]) * cos_ref[...] + _dot(hn, wr_ref[...]) * sin_ref[...]
    ckv_ref[...] = ckv
    kpe_ref[...] = kpe[:, 0:R]
    kcat_ref[:, 0:C] = ckv.astype(kcat_ref.dtype)
    kcat_ref[:, C:C + LANE] = kpe.astype(kcat_ref.dtype)
    if maybe_vt_ref:
        maybe_vt_ref[0][...] = ckv.T.astype(maybe_vt_ref[0].dtype)


def _kv_stream(grp, xs, y, y_off, pos, pmod, kvmod, cos, sin, consts, C, R, kcat_dtype, with_vt):
    D = xs.shape[-1]
    in_specs = [grp.tok_spec(D, y_off), grp.tok_spec(D, y_off), grp.mod_spec(pmod.shape[-1]),
                grp.mod_spec(kvmod.shape[-1]), grp.rope_spec(LANE), grp.rope_spec(LANE)]
    in_specs += [_const_spec(a) for a in consts]
    widths = [(D, F32), (C, F32), (R, F32), (C + LANE, kcat_dtype)]
    out_specs = [grp.tok_spec(w) for w, _ in widths]
    out_shape = [jax.ShapeDtypeStruct((grp.T, w), dt) for w, dt in widths]
    if with_vt:
        tps = grp.tiles_per_seq
        out_specs.append(pl.BlockSpec((None, C, grp.tm), lambda i: (i // tps, 0, i % tps)))
        out_shape.append(jax.ShapeDtypeStruct((grp.n_seq, C, grp.seq_len), BF16))
    return _tile_call("kv_stream", functools.partial(_kv_kernel, D=D, C=C, R=R), grp, in_specs,
                      [xs, y, pmod, kvmod, cos, sin, *consts], out_specs, out_shape, 1, pos)


def _mla_q_kernel(*refs, has_y, transposed, D, C, H, scale):
    refs = list(refs)
    xs_ref = refs.pop(0)
    y_ref = refs.pop(0) if has_y else None
    pmod_ref = refs.pop(0) if has_y else None
    mod_ref, cos_ref, sin_ref, g_ref, wdq_ref, qg_ref, wn_ref, wp_ref, wr_ref, wuk_ref = refs[:10]
    outs = refs[10:]
    if has_y:
        xo_ref, outs = outs[0], outs[1:]
    qcat_ref, = outs
    x = _resid(xs_ref, y_ref, pmod_ref, D)
    if has_y:
        xo_ref[...] = x
    h = (_rms(x, g_ref[...]) * (1.0 + mod_ref[:, D:2 * D]) + mod_ref[:, 0:D]).astype(BF16)
    cq = _rms(_dot(h, wdq_ref[...]), qg_ref[...]).astype(BF16)
    q_nope = _dot(cq, wn_ref[...]).astype(BF16)
    cos, sin = cos_ref[...], sin_ref[...]
    dn = q_nope.shape[-1] // H
    for hh in range(H):
        ls = slice(hh * LANE, (hh + 1) * LANE)
        q_lat = _dot(q_nope[:, hh * dn:(hh + 1) * dn], wuk_ref[hh])
        q_pe = _dot(cq, wp_ref[:, ls]) * cos + _dot(cq, wr_ref[:, ls]) * sin
        if transposed:
            tm = q_lat.shape[0]
            qcat_ref[0:C, hh * tm:(hh + 1) * tm] = (q_lat * scale).T.astype(qcat_ref.dtype)
            qcat_ref[C:C + LANE, hh * tm:(hh + 1) * tm] = (q_pe * scale).T.astype(qcat_ref.dtype)
        else:
            qcat_ref[hh, :, 0:C] = (q_lat * scale).astype(qcat_ref.dtype)
            qcat_ref[hh, :, C:C + LANE] = (q_pe * scale).astype(qcat_ref.dtype)


def _mla_q(grp, xs, y, y_off, pos, pmod, mod, cos, sin, norm_g, wq, out_dtype, transposed):
    D = xs.shape[-1]
    w_dq, qg, w_n, w_p, w_r, w_uk = wq
    C = w_uk.shape[-1]
    H = MLA_HEADS
    has_y = y is not None
    scale = float(w_uk.shape[1] + QK_ROPE) ** -0.5
    if transposed:
        scale *= LOG2_E
    in_specs = [grp.tok_spec(D, y_off if has_y else 0)]
    in_specs += [grp.tok_spec(D, y_off), grp.mod_spec(pmod.shape[-1])] if has_y else []
    args = [xs] + ([y, pmod] if has_y else [])
    in_specs += [grp.mod_spec(mod.shape[-1]), grp.rope_spec(LANE), grp.rope_spec(LANE)]
    args += [mod, cos, sin]
    for a in (norm_g, w_dq, qg, w_n, w_p, w_r, w_uk):
        in_specs.append(_const_spec(a)); args.append(a)
    if transposed:
        q_spec = pl.BlockSpec((None, C + LANE, H * grp.tm), lambda i: (i, 0, 0))
        q_shape = jax.ShapeDtypeStruct((grp.n_tiles, C + LANE, H * grp.tm), out_dtype)
    else:
        q_spec = pl.BlockSpec((H, grp.tm, C + LANE), lambda i: (0, i, 0))
        q_shape = jax.ShapeDtypeStruct((H, grp.T, C + LANE), out_dtype)
    out_specs = ([grp.tok_spec(D)] if has_y else []) + [q_spec]
    out_shape = ([jax.ShapeDtypeStruct((grp.T, D), F32)] if has_y else []) + [q_shape]
    res = _tile_call(
        "mla_q", functools.partial(_mla_q_kernel, has_y=has_y, transposed=transposed, D=D, C=C, H=H, scale=scale),
        grp, in_specs, args, out_specs, out_shape, 1 if has_y else None, pos)
    return (res[0], res[1]) if has_y else (xs, res[0])


def _mla_out(o_lat_heads, x, g1, wuvt_ref, wo_ref):
    parts = [_dot(o.astype(BF16), wuvt_ref[hh]).astype(BF16) for hh, o in enumerate(o_lat_heads)]
    return x + g1 * _dot(jnp.concatenate(parts, axis=-1), wo_ref[...])


def _attn_kernel(x_ref, qt_ref, k_ref, vt_ref, mod_ref, wuvt_ref, wo_ref, xo_ref, m_ref, l_ref, acc_ref, s0_ref,
                 *, D, C, H, tq, tk, cw):
    qi = pl.program_id(1)
    m_ref[...] = jnp.full(m_ref.shape, -jnp.inf, F32)
    l_ref[...] = jnp.zeros(l_ref.shape, F32)
    acc_ref[...] = jnp.zeros(acc_ref.shape, F32)
    n_col = H * tq

    def keys(j):
        return k_ref[pl.ds(pl.multiple_of(j * tk, tk), tk), :]

    s0_ref[...] = _dot(keys(0), qt_ref[:, 0:cw])

    def block(j, masked):
        kb = keys(j)
        vt = vt_ref[:, pl.ds(pl.multiple_of(j * tk, tk), tk)]
        if masked:
            kpos = lax.broadcasted_iota(jnp.int32, (tk, cw), 0)
            qpos = lax.broadcasted_iota(jnp.int32, (tk, cw), 1) & (tq - 1)
            keep = kpos <= qpos
        s_next = s0_ref[...]
        for c0 in range(0, n_col, cw):
            cs = slice(c0, c0 + cw)
            s = s_next
            if c0 + cw < n_col:
                s_next = _dot(kb, qt_ref[:, c0 + cw:c0 + 2 * cw])
            elif not masked:
                s0_ref[...] = _dot(keys(j + 1), qt_ref[:, 0:cw])
            if masked:
                s = jnp.where(keep, s, -1e30)
            m_old = m_ref[:, cs]
            m_new = jnp.maximum(m_old, jnp.max(s, axis=0, keepdims=True))
            alpha = jnp.exp2(m_old - m_new)
            p = jnp.exp2(s - m_new)
            l_ref[:, cs] = alpha * l_ref[:, cs] + jnp.sum(p, axis=0, keepdims=True)
            acc_ref[:, cs] = alpha * acc_ref[:, cs] + _dot(vt, p.astype(BF16))
            m_ref[:, cs] = m_new

    def body(j, c):
        block(j, False)
        return c
    lax.fori_loop(0, qi, body, 0)
    block(qi, True)
    parts = []
    for hh in range(H):
        cs = slice(hh * tq, (hh + 1) * tq)
        o_t = (acc_ref[:, cs] / l_ref[:, cs]).astype(BF16)
        parts.append(_dot_tn(o_t, wuvt_ref[hh]).astype(BF16))
    y = _dot(jnp.concatenate(parts, axis=-1), wo_ref[...])
    xo_ref[...] = x_ref[...] + mod_ref[:, 2 * D:3 * D] * y


def _attn_prompt(B, S, x, qt, kcat, vt, mod, w_uvt, w_o):
    D = x.shape[-1]
    H = MLA_HEADS
    _, Wq, n_col = qt.shape
    C = w_uvt.shape[1]
    tq, tk = ATT_TQ, ATT_TK
    assert tq == tk and tq & (tq - 1) == 0 and n_col == H * tq
    nq = S // tq
    cst = lambda a: pl.BlockSpec(a.shape, lambda b, i: (0,) * a.ndim)
    return pl.pallas_call(
        functools.partial(_attn_kernel, D=D, C=C, H=H, tq=tq, tk=tk, cw=ATT_CW), grid=(B, nq),
        in_specs=[pl.BlockSpec((tq, D), lambda b, i: (b * nq + i, 0)),
                  pl.BlockSpec((None, Wq, n_col), lambda b, i: (b * nq + i, 0, 0)),
                  pl.BlockSpec((S, Wq), lambda b, i: (b, 0)),
                  pl.BlockSpec((None, C, S), lambda b, i: (b, 0, 0)),
                  pl.BlockSpec((None, 1, mod.shape[-1]), lambda b, i: (b, 0, 0)), cst(w_uvt), cst(w_o)],
        out_specs=pl.BlockSpec((tq, D), lambda b, i: (b * nq + i, 0)),
        out_shape=jax.ShapeDtypeStruct(x.shape, F32),
        scratch_shapes=[pltpu.VMEM((1, n_col), F32), pltpu.VMEM((1, n_col), F32), pltpu.VMEM((C, n_col), F32),
                        pltpu.VMEM((tk, ATT_CW), F32)],
        name="attn_prompt", compiler_params=_cparams(2),
    )(x, qt, kcat, vt, mod, w_uvt, w_o)


def _attn_dec_kernel(pt_ref, q_ref, kn_ref, ckv_hbm, kpe_hbm, o_ref, cbuf, pbuf, kb_ref, s_ref, csem, psem,
                     *, C, R, n_pages, page, cp):
    b = pl.program_id(0)
    nb = pl.num_programs(0)
    slot = b % 2

    def start(seq, sl):
        def body(p, c):
            pg = pt_ref[seq * n_pages + p]
            pltpu.make_async_copy(ckv_hbm.at[pl.ds(pg, 1)], cbuf.at[sl, pl.ds(p, 1)], csem.at[sl]).start()
            pltpu.make_async_copy(kpe_hbm.at[pl.ds(pg, 1)], pbuf.at[sl, pl.ds(p, 1)], psem.at[sl]).start()
            return c
        lax.fori_loop(0, n_pages, body, 0, unroll=4)

    @pl.when(b == 0)
    def _():
        start(0, 0)

    @pl.when(b + 1 < nb)
    def _():
        start(b + 1, 1 - slot)

    pltpu.make_async_copy(ckv_hbm.at[pl.ds(0, n_pages)], cbuf.at[slot], csem.at[slot]).wait()
    pltpu.make_async_copy(kpe_hbm.at[pl.ds(0, n_pages)], pbuf.at[slot], psem.at[slot]).wait()

    q = q_ref[...]
    ql = q[:, 0:C].astype(BF16)
    qp = q[:, C:C + R].astype(BF16)
    ck = cp * page
    n_chunks = n_pages // cp
    for c in range(n_chunks):
        kc = cbuf[slot, c * cp:(c + 1) * cp].reshape(ck, C).astype(BF16)
        pct = jnp.concatenate([pbuf[slot, c * cp + i] for i in range(cp)], axis=-1).astype(BF16)
        kb_ref[c * ck:(c + 1) * ck, :] = kc
        s_ref[:, c * ck:(c + 1) * ck] = _dot_nt(ql, kc) + _dot(qp, pct)
    kn = kn_ref[...]
    s_new = jnp.sum(q * kn, axis=-1, keepdims=True)
    s = s_ref[...]
    m = jnp.maximum(jnp.max(s, axis=-1, keepdims=True), s_new)
    p = jnp.exp(s - m)
    p_new = jnp.exp(s_new - m)
    l = jnp.sum(p, axis=-1, keepdims=True) + p_new
    acc = p_new * kn[:, 0:C]
    pb = p.astype(BF16)
    for c in range(n_chunks):
        acc = acc + _dot(pb[:, c * ck:(c + 1) * ck], kb_ref[c * ck:(c + 1) * ck, :])
    o_ref[...] = acc / l


def _attn_dec(q_s, kn, cache_ckv, kpe_pages, page_table):
    B, H, Wq = q_s.shape
    _, page, C = cache_ckv.shape
    R = kpe_pages.shape[1]
    n_pages = page_table.shape[1]
    P = n_pages * page
    any_spec = pl.BlockSpec(memory_space=pl.ANY)
    grid_spec = pltpu.PrefetchScalarGridSpec(
        num_scalar_prefetch=1, grid=(B,),
        in_specs=[pl.BlockSpec((None, H, Wq), lambda b, pt: (b, 0, 0)),
                  pl.BlockSpec((None, 1, Wq), lambda b, pt: (b, 0, 0)), any_spec, any_spec],
        out_specs=pl.BlockSpec((None, H, C), lambda b, pt: (b, 0, 0)),
        scratch_shapes=[pltpu.VMEM((2, n_pages, page, C), F32), pltpu.VMEM((2, n_pages, R, page), F32),
                        pltpu.VMEM((P, C), BF16), pltpu.VMEM((H, P), F32),
                        pltpu.SemaphoreType.DMA((2,)), pltpu.SemaphoreType.DMA((2,))])
    return pl.pallas_call(
        functools.partial(_attn_dec_kernel, C=C, R=R, n_pages=n_pages, page=page, cp=DEC_CHUNK_PAGES),
        grid_spec=grid_spec, out_shape=jax.ShapeDtypeStruct((B, H, C), F32), name="attn_dec",
        compiler_params=_cparams(1),
    )(page_table.reshape(-1), q_s, kn, cache_ckv, kpe_pages)


def _mla_out_kernel(x_ref, o_ref, mod_ref, wuvt_ref, wo_ref, xo_ref, *, D, H):
    heads = [o_ref[hh] for hh in range(H)]
    xo_ref[...] = _mla_out(heads, x_ref[...], mod_ref[:, 2 * D:3 * D], wuvt_ref, wo_ref)


def _final_kernel(xs_ref, y_ref, pmod_ref, g_ref, o_ref, *, D):
    o_ref[...] = _rms(_resid(xs_ref, y_ref, pmod_ref, D), g_ref[...])


def _rope_table(pos, d, width):
    inv = ROPE_THETA ** (-jnp.arange(0, d, 2, dtype=F32) / d)
    ang = pos.astype(F32)[:, None] * inv[None, :]
    cos, sin = jnp.cos(ang), jnp.sin(ang)
    pad = jnp.zeros((pos.shape[0], width - d), F32)
    return jnp.concatenate([cos, cos, pad], axis=-1), jnp.concatenate([sin, sin, pad], axis=-1)


def _rot_half_cols(w):
    half = w.shape[-1] // 2
    return jnp.concatenate([-w[..., half:], w[..., :half]], axis=-1)


def _pad_lanes(w):
    pad = jnp.zeros(w.shape[:-1] + (LANE - w.shape[-1],), w.dtype)
    return jnp.concatenate([w, pad], axis=-1)


def _ret_tables(L):
    log_g = jnp.log1p(-jnp.exp2(-5.0 - jnp.arange(RET_HEADS, dtype=F32)))
    idx = jnp.arange(L, dtype=F32)
    diff = idx[:, None] - idx[None, :]
    dec = jnp.where(diff[None] >= 0, jnp.exp(jnp.maximum(diff, 0.0)[None] * log_g[:, None, None]), 0.0)
    qd = jnp.exp((idx + 1.0)[:, None] * log_g[None, :])
    kd = jnp.exp((L - 1.0 - idx)[:, None] * log_g[None, :])
    gl = jnp.exp(L * log_g)[None, :]
    return dec, qd, kd, gl


def kernel(x_prompt, x_sample, c_prompt, c_sample, state_ret, cache_ckv, cache_kpe, page_table, w_ada, b_ada, norm_mix_g, norm_ffn_g, ret_w_in, ret_gn_g, ret_w_out, kv_w_ada, kv_b_ada, kv_norm_g, mla_w_dkv, mla_kv_norm_g, mla_w_uk, mla_w_uv, mla_w_dq, mla_q_norm_g, mla_w_uq, mla_w_o, router_w, router_bias, moe_w_gate, moe_w_up, moe_w_down, shared_w_gate, shared_w_up, shared_w_down, final_norm_g):
    B, S, D = x_prompt.shape
    Bs, Ss, _ = x_sample.shape
    assert Ss == 1
    depth = w_ada.shape[0]
    n_a = ret_w_in.shape[0]
    H = MLA_HEADS
    C = mla_w_dkv.shape[1] - QK_ROPE
    R = QK_ROPE
    dn = mla_w_uk.shape[1]
    past_len = page_table.shape[1] * cache_ckv.shape[1]

    gp = _Group(B, S, TM_PROMPT, False)
    gs = _Group(Bs, 1, Bs, True)
    groups = (gp, gs)
    T_all = gp.T + gs.T
    ys_off = gp.T // gs.tm

    n_c = B + Bs
    n_c_pad = -(-n_c // 8) * 8
    c_all = jnp.concatenate([c_prompt, c_sample, jnp.zeros((n_c_pad - n_c, D), F32)], axis=0)
    mod_all = _ada_mod(c_all, w_ada, b_ada)
    kvmod_all = _ada_mod(c_all, kv_w_ada[None], kv_b_ada[None])[0]
    mods = [[g.mod_array(mod_all[l, lo:lo + g.n_seq]) for l in range(depth)]
            for g, lo in zip(groups, (0, B))]
    kvmods = [g.mod_array(kvmod_all[lo:lo + g.n_seq]) for g, lo in zip(groups, (0, B))]

    pos = (jnp.arange(S), past_len + jnp.arange(1))
    ret_rope = [_rope_table(p, D // RET_HEADS, D // RET_HEADS)[0:2] for p in pos]
    ret_rope = [(c[:, :c.shape[1] // 2], s[:, :s.shape[1] // 2]) for c, s in ret_rope]
    mla_rope = [_rope_table(p, R, LANE) for p in pos]
    tabs_p = _ret_tables(RET_CHUNK)
    gl_s = _ret_tables(1)[3]

    row = lambda g: g.reshape(1, -1)
    w_in = ret_w_in.astype(BF16)
    w_out = ret_w_out.astype(BF16)
    wsg, wsu, wsd = shared_w_gate.astype(BF16), shared_w_up.astype(BF16), shared_w_down.astype(BF16)
    rw_t = router_w.T
    rb = router_bias.reshape(-1, 1)
    w_dkv_c = mla_w_dkv[:, :C].astype(BF16)
    w_dkv_p = _pad_lanes(mla_w_dkv[:, C:]).astype(BF16)
    w_dkv_r = _pad_lanes(_rot_half_cols(mla_w_dkv[:, C:])).astype(BF16)
    w_uk = mla_w_uk.astype(BF16)
    w_uvt = jnp.swapaxes(mla_w_uv, 1, 2).astype(BF16)
    wq = []
    for j in range(depth - n_a):
        w3 = mla_w_uq[j].reshape(-1, H, dn + R)
        w_n = w3[:, :, :dn].reshape(-1, H * dn).astype(BF16)
        w_p = _pad_lanes(w3[:, :, dn:]).reshape(-1, H * LANE).astype(BF16)
        w_r = _pad_lanes(_rot_half_cols(w3[:, :, dn:])).reshape(-1, H * LANE).astype(BF16)
        wq.append((mla_w_dq[j].astype(BF16), row(mla_q_norm_g[j]), w_n, w_p, w_r, w_uk))
    w_o = mla_w_o.astype(BF16)

    kpe_pages = jnp.swapaxes(cache_kpe, 1, 2)

    xs = [x_prompt.reshape(gp.T, D), x_sample.reshape(gs.T, D)]
    y, ypos = None, [None, None]
    states_p, state_s = [], None
    ckv_out, kpe_out, kcat = [None, None], [None, None], [None, None]

    for l in range(depth):
        y_offs = (0, ys_off)
        xmid = [None, None]
        if l < n_a:
            for gi, g in enumerate(groups):
                pm = mods[gi][l - 1] if y is not None else None
                x, q, k, v, gt = _ret_in(g, xs[gi], y, y_offs[gi], ypos[gi], pm, mods[gi][l],
                                         ret_rope[gi][0], ret_rope[gi][1],
                                         row(norm_mix_g[l]), w_in[l], BF16 if gi == 0 else F32)
                if gi == 0:
                    xmid[gi], st = _ret_chunk(B, S, x, q, k, v, gt, mods[gi][l], tabs_p, row(ret_gn_g[l]), w_out[l])
                else:
                    state_s, o = _ret_dec(state_ret if l == 0 else state_s, l, q, k, v, gl_s, in_place=l > 0)
                    xmid[gi], = _tok_call("ret_post", functools.partial(_ret_post_kernel, D=D, H=RET_HEADS), g,
                                          [(x, 0), (o, 0), (gt, 0)], [mods[gi][l]], [],
                                          [row(ret_gn_g[l]), w_out[l]], [(D, F32)])
                if gi == 0:
                    states_p.append(st)
        else:
            j = l - n_a
            for gi, g in enumerate(groups):
                xsg, yy, pm = xs[gi], y, mods[gi][l - 1]
                if l == n_a:
                    res = _kv_stream(g, xsg, y, y_offs[gi], ypos[gi], pm, kvmods[gi],
                                     mla_rope[gi][0], mla_rope[gi][1],
                                     [row(kv_norm_g), w_dkv_c, w_dkv_p, w_dkv_r, row(mla_kv_norm_g)], C, R,
                                     BF16 if gi == 0 else F32, gi == 0)
                    xsg, ckv_out[gi], kpe_out[gi], kcat[gi] = res[:4]
                    if gi == 0:
                        vt_p = res[4]
                    yy, pm = None, None
                x, qcat = _mla_q(g, xsg, yy, y_offs[gi], ypos[gi], pm, mods[gi][l],
                                 mla_rope[gi][0], mla_rope[gi][1],
                                 row(norm_mix_g[l]), wq[j], BF16 if gi == 0 else F32, gi == 0)
                if gi == 0:
                    xmid[gi] = _attn_prompt(B, S, x, qcat, kcat[gi], vt_p, mods[gi][l], w_uvt, w_o[j])
                else:
                    o_lat = _attn_dec(jnp.swapaxes(qcat, 0, 1), kcat[gi][:, None, :], cache_ckv, kpe_pages, page_table)
                    xmid[gi] = pl.pallas_call(
                        functools.partial(_mla_out_kernel, D=D, H=H), grid=(1,),
                        in_specs=[_const_spec(x), pl.BlockSpec((H, Bs, C), lambda i: (0, 0, 0)),
                                  _const_spec(mods[gi][l]), _const_spec(w_uvt), _const_spec(w_o[j])],
                        out_specs=_const_spec(x), out_shape=jax.ShapeDtypeStruct(x.shape, F32),
                        name="mla_out", compiler_params=_cparams(1),
                    )(x, jnp.swapaxes(o_lat, 0, 1), mods[gi][l], w_uvt, w_o[j])
        ffn_in = (gp, gs, xmid[0], xmid[1], mods[0][l], mods[1][l], row(norm_ffn_g[l]))
        cls, gates = _router(*ffn_in, rw_t, rb)
        n_rows, plan = _moe_plan(cls, T_all)
        pos_pad = jnp.concatenate([plan[0], jnp.zeros((cls.shape[0] - T_all,), jnp.int32)])
        hx, xs_all = _moe_pre(*ffn_in, gates, pos_pad, n_rows, wsg[l], wsu[l], wsd[l])
        xs = [xs_all, xs_all]
        y = _moe_routed(hx, plan, T_all, l, moe_w_gate, moe_w_up, moe_w_down)

    outs = []
    for gi, g in enumerate(groups):
        o, = _tok_call("final_norm", functools.partial(_final_kernel, D=D), g,
                       [(xs[gi], (0, ys_off)[gi]), (y, (0, ys_off)[gi])],
                       [mods[gi][depth - 1]], [], [row(final_norm_g)], [(D, F32)], 1, ypos[gi])
        outs.append(o)
    return (outs[0].reshape(B, S, D), outs[1].reshape(Bs, 1, D),
            jnp.stack(states_p), state_s,
            ckv_out[0].reshape(B, S, C), kpe_out[0].reshape(B, S, R),
            ckv_out[1].reshape(Bs, 1, C), kpe_out[1].reshape(Bs, 1, R))
```

```python
import functools

import jax
import jax.numpy as jnp
from jax import lax
from jax.experimental import pallas as pl
from jax.experimental.pallas import tpu as pltpu

F32 = jnp.float32
BF16 = jnp.bfloat16

RET_HEADS = 4
MLA_HEADS = 8
QK_ROPE = 64
N_GROUPS = 4
EXPERTS_PER_GROUP = 4
ROPE_THETA = 10000.0
EPS = 1e-6
LOG2_E = 1.4426950408889634

_PAIRS = ((0, 1), (0, 2), (0, 3), (1, 2), (1, 3), (2, 3))
N_CLASSES = N_GROUPS * len(_PAIRS)

LANE = 128
VMEM_LIMIT_BYTES = 56 * 2 ** 20
TM_PROMPT = 256
MOE_TILE = 256
RET_CHUNK = 256
ATT_TQ = 256
ATT_TK = 256
ATT_CW = 512
DEC_NB = 8
DEC_CHUNK_PAGES = 8


def _dot(a, b):
    return jnp.dot(a, b, preferred_element_type=F32)


def _dot_nt(a, b):
    return lax.dot_general(a, b, (((1,), (1,)), ((), ())), preferred_element_type=F32)


def _dot_tn(a, b):
    return lax.dot_general(a, b, (((0,), (0,)), ((), ())), preferred_element_type=F32)


def _silu(x):
    return x * jax.nn.sigmoid(x)


def _rms(x, g):
    return x * lax.rsqrt(jnp.mean(x * x, axis=-1, keepdims=True) + EPS) * g


def _cparams(n_axes=1):
    return pltpu.CompilerParams(dimension_semantics=("arbitrary",) * n_axes,
                                vmem_limit_bytes=VMEM_LIMIT_BYTES)


class _Group:
    def __init__(self, n_seq, seq_len, tm, per_token_mod):
        self.n_seq, self.seq_len, self.tm = n_seq, seq_len, tm
        self.T = n_seq * seq_len
        self.n_tiles = self.T // tm
        self.per_token_mod = per_token_mod
        self.tiles_per_seq = max(seq_len // tm, 1)

    def tok_spec(self, width, offset=0):
        return pl.BlockSpec((self.tm, width), lambda i: (i + offset, 0))

    def mod_spec(self, width):
        if self.per_token_mod:
            return pl.BlockSpec((self.tm, width), lambda i: (i, 0))
        tps = self.tiles_per_seq
        return pl.BlockSpec((None, 1, width), lambda i: (i // tps, 0, 0))

    def rope_spec(self, width):
        if self.per_token_mod:
            return pl.BlockSpec((1, width), lambda i: (0, 0))
        tps = self.tiles_per_seq
        return pl.BlockSpec((self.tm, width), lambda i: (i % tps, 0))

    def mod_array(self, m):
        return m if self.per_token_mod else m[:, None, :]


def _const_spec(a):
    nd = a.ndim
    return pl.BlockSpec(a.shape, lambda i: (0,) * nd)


def _tile_call(name, body, grp, in_specs, args, out_specs, out_shape):
    return pl.pallas_call(body, grid=(grp.n_tiles,), in_specs=in_specs, out_specs=out_specs,
                          out_shape=out_shape, name=name, compiler_params=_cparams(1))(*args)


def _tok_call(name, body, grp, tok_ins, mod_ins, rope_ins, const_ins, outs):
    in_specs, args = [], []
    for a, off in tok_ins:
        in_specs.append(grp.tok_spec(a.shape[-1], off)); args.append(a)
    for a in mod_ins:
        in_specs.append(grp.mod_spec(a.shape[-1])); args.append(a)
    for a in rope_ins:
        in_specs.append(grp.rope_spec(a.shape[-1])); args.append(a)
    for a in const_ins:
        in_specs.append(_const_spec(a)); args.append(a)
    out_specs = [grp.tok_spec(width) for width, _ in outs]
    out_shape = [jax.ShapeDtypeStruct((grp.T, width), dtype) for width, dtype in outs]
    return _tile_call(name, body, grp, in_specs, args, out_specs, out_shape)


def _mod_kernel(c_ref, w_ref, b_ref, o_ref):
    c = c_ref[...]
    o_ref[...] = _dot(_silu(c).astype(BF16), w_ref[...].astype(BF16)) + b_ref[...]


def _ada_mod(c_all, w, b):
    L, D, N = w.shape
    Bc = c_all.shape[0]
    tn = min(N, 2048)
    return pl.pallas_call(
        _mod_kernel, grid=(L, N // tn),
        in_specs=[pl.BlockSpec((Bc, D), lambda l, j: (0, 0)),
                  pl.BlockSpec((None, D, tn), lambda l, j: (l, 0, j)),
                  pl.BlockSpec((None, 1, tn), lambda l, j: (l, 0, j))],
        out_specs=pl.BlockSpec((None, Bc, tn), lambda l, j: (l, 0, j)),
        out_shape=jax.ShapeDtypeStruct((L, Bc, N), F32),
        name="ada_mod", compiler_params=_cparams(2),
    )(c_all, w, b[:, None, :])


def _resid(xs_ref, y_ref, pmod_ref, D):
    x = xs_ref[...]
    if y_ref is not None:
        x = x + pmod_ref[:, 5 * D:6 * D] * y_ref[...]
    return x


def _ret_in_kernel(*refs, has_y, D, H):
    refs = list(refs)
    xs_ref = refs.pop(0)
    y_ref = refs.pop(0) if has_y else None
    pmod_ref = refs.pop(0) if has_y else None
    mod_ref, cos_ref, sin_ref, g_ref, w_ref = refs[:5]
    outs = refs[5:]
    if has_y:
        xo_ref, outs = outs[0], outs[1:]
    q_ref, k_ref, v_ref, gt_ref = outs
    x = _resid(xs_ref, y_ref, pmod_ref, D)
    if has_y:
        xo_ref[...] = x
    h = _rms(x, g_ref[...]) * (1.0 + mod_ref[:, D:2 * D]) + mod_ref[:, 0:D]
    hb = h.astype(BF16)
    dk = D // H
    half = dk // 2
    cos, sin = cos_ref[...], sin_ref[...]
    for idx, (o_ref, scale) in enumerate(((q_ref, float(dk) ** -0.5), (k_ref, 1.0))):
        t = _dot(hb, w_ref[:, idx * D:(idx + 1) * D])
        for hh in range(H):
            x1 = t[:, hh * dk:hh * dk + half]
            x2 = t[:, hh * dk + half:(hh + 1) * dk]
            o_ref[:, hh * dk:hh * dk + half] = ((x1 * cos - x2 * sin) * scale).astype(o_ref.dtype)
            o_ref[:, hh * dk + half:(hh + 1) * dk] = ((x1 * sin + x2 * cos) * scale).astype(o_ref.dtype)
    v_ref[...] = _dot(hb, w_ref[:, 2 * D:3 * D]).astype(v_ref.dtype)
    gt_ref[...] = _dot(hb, w_ref[:, 3 * D:4 * D])


def _ret_in(grp, xs, y, y_off, pmod, mod, cos, sin, norm_g, w_in, qkv_dtype):
    D = xs.shape[-1]
    has_y = y is not None
    tok = [(xs, y_off if has_y else 0)] + ([(y, y_off)] if has_y else [])
    mods = ([pmod] if has_y else []) + [mod]
    outs = ([(D, F32)] if has_y else []) + [(D, qkv_dtype)] * 3 + [(D, F32)]
    res = _tok_call("ret_in", functools.partial(_ret_in_kernel, has_y=has_y, D=D, H=RET_HEADS), grp,
                    tok, mods, [cos, sin], [norm_g, w_in], outs)
    if not has_y:
        res = [xs] + list(res)
    return res


def _gn_gate(o, g, gn):
    mu = jnp.mean(o, axis=-1, keepdims=True)
    c = o - mu
    var = jnp.mean(c * c, axis=-1, keepdims=True)
    return _silu(g) * (c * lax.rsqrt(var + EPS) * gn)


def _ret_chunk_kernel(x_ref, q_ref, k_ref, v_ref, gt_ref, mod_ref, dec_ref, qd_ref, kd_ref, gl_ref,
                      gn_ref, wo_ref, xo_ref, s_ref, gat_ref, *, D, H):
    @pl.when(pl.program_id(1) == 0)
    def _():
        s_ref[...] = jnp.zeros_like(s_ref)

    dk = D // H
    heads = [slice(h * dk, (h + 1) * dk) for h in range(H)]
    states = [s_ref[h] for h in range(H)]
    raw = [_dot_nt(q_ref[:, hs], k_ref[:, hs]) for hs in heads]
    cross = [_dot((q_ref[:, hs].astype(F32) * qd_ref[:, h:h + 1]).astype(BF16), states[h].astype(BF16))
             for h, hs in enumerate(heads)]
    for h, hs in enumerate(heads):
        kh, vh = k_ref[:, hs], v_ref[:, hs]
        inner = _dot((raw[h] * dec_ref[h]).astype(BF16), vh)
        kdec = (kh.astype(F32) * kd_ref[:, h:h + 1]).astype(BF16)
        s_ref[h] = gl_ref[:, h:h + 1] * states[h] + _dot_tn(kdec, vh)
        gat_ref[:, hs] = _gn_gate(inner + cross[h], gt_ref[:, hs], gn_ref[:, hs]).astype(BF16)
    y = _dot(gat_ref[...], wo_ref[...])
    xo_ref[...] = x_ref[...] + mod_ref[:, 2 * D:3 * D] * y


def _ret_chunk(B, S, x, q, k, v, gt, mod, tabs, gn_g, w_out):
    D = x.shape[-1]
    H = RET_HEADS
    L = RET_CHUNK
    nc = S // L
    dk = D // H
    dec, qd, kd, gl = tabs
    tok = lambda w: pl.BlockSpec((L, w), lambda b, c: (b * nc + c, 0))
    cst = lambda a: pl.BlockSpec(a.shape, lambda b, c: (0,) * a.ndim)
    return pl.pallas_call(
        functools.partial(_ret_chunk_kernel, D=D, H=H), grid=(B, nc),
        in_specs=[tok(D)] * 5 + [pl.BlockSpec((None, 1, mod.shape[-1]), lambda b, c: (b, 0, 0)),
                                 cst(dec), cst(qd), cst(kd), cst(gl), cst(gn_g), cst(w_out)],
        out_specs=[tok(D), pl.BlockSpec((None, H, dk, dk), lambda b, c: (b, 0, 0, 0))],
        out_shape=[jax.ShapeDtypeStruct(x.shape, F32), jax.ShapeDtypeStruct((B, H, dk, dk), F32)],
        scratch_shapes=[pltpu.VMEM((L, D), BF16)],
        name="ret_chunk", compiler_params=_cparams(2),
    )(x, q, k, v, gt, mod, dec, qd, kd, gl, gn_g, w_out)


def _ret_dec_kernel(st_ref, q_ref, k_ref, v_ref, gl_ref, so_ref, o_ref, *, D, H, nb):
    is_update = pl.program_id(0) == pl.num_programs(0) - 1

    @pl.when(jnp.logical_not(is_update))
    def _():
        so_ref[...] = st_ref[...]
        o_ref[...] = jnp.zeros(o_ref.shape, F32)

    @pl.when(is_update)
    def _():
        _ret_dec_update(st_ref, q_ref, k_ref, v_ref, gl_ref, so_ref, o_ref, D=D, H=H, nb=nb)


def _ret_dec_update(st_ref, q_ref, k_ref, v_ref, gl_ref, so_ref, o_ref, *, D, H, nb):
    dk = D // H
    for h in range(H):
        hs = slice(h * dk, (h + 1) * dk)
        gh = gl_ref[:, h:h + 1]
        qh, kh, vh = q_ref[:, hs], k_ref[:, hs], v_ref[:, hs]
        inner = jnp.sum(qh * kh, axis=-1, keepdims=True) * vh
        qT = (qh * gh).T
        kT = kh.T
        rows = []
        for j in range(nb):
            S = st_ref[j, h]
            so_ref[j, h] = gh * S + kT[:, j:j + 1] * vh[j:j + 1, :]
            rows.append(jnp.sum(qT[:, j:j + 1] * S, axis=0, keepdims=True))
        o_ref[:, hs] = inner + jnp.concatenate(rows, axis=0)


def _ret_dec(state, layer, q, k, v, gl, in_place):
    L, B, H, dk, dv = state.shape
    D = q.shape[-1]
    nb = DEC_NB
    phases = 1 if in_place else L
    which = lambda p: (layer + 1 + p) % L if not in_place else layer
    row = pl.BlockSpec((nb, D), lambda p, i: (i, 0))
    st_spec = pl.BlockSpec((None, nb, H, dk, dv), lambda p, i: (which(p), i, 0, 0, 0))
    new_state, o = pl.pallas_call(
        functools.partial(_ret_dec_kernel, D=D, H=H, nb=nb), grid=(phases, B // nb),
        in_specs=[st_spec, row, row, row, pl.BlockSpec(gl.shape, lambda p, i: (0, 0))],
        out_specs=[st_spec, pl.BlockSpec((None, nb, D), lambda p, i: (p, i, 0))],
        out_shape=[jax.ShapeDtypeStruct(state.shape, F32), jax.ShapeDtypeStruct((phases, B, D), F32)],
        input_output_aliases={0: 0} if in_place else {},
        name="ret_dec", compiler_params=_cparams(2),
    )(state, q, k, v, gl)
    return new_state, o[phases - 1]


def _ret_post_kernel(x_ref, o_ref, gt_ref, mod_ref, gn_ref, wo_ref, xo_ref, *, D, H):
    dk = D // H
    parts = []
    for h in range(H):
        hs = slice(h * dk, (h + 1) * dk)
        parts.append(_gn_gate(o_ref[:, hs], gt_ref[:, hs], gn_ref[:, hs]).astype(BF16))
    y = _dot(jnp.concatenate(parts, axis=-1), wo_ref[...])
    xo_ref[...] = x_ref[...] + mod_ref[:, 2 * D:3 * D] * y


def _route(logits_t, bias):
    scores = jax.nn.sigmoid(logits_t)
    sel = scores + bias
    n_e = EXPERTS_PER_GROUP
    row = lambda a, e: a[e:e + 1, :]
    gscore = []
    for g in range(N_GROUPS):
        best2 = None
        for a, b in _PAIRS:
            s = row(sel, n_e * g + a) + row(sel, n_e * g + b)
            best2 = s if best2 is None else jnp.maximum(best2, s)
        gscore.append(best2)
    best, bestv = jnp.zeros_like(gscore[0], dtype=jnp.int32), gscore[0]
    for g in range(1, N_GROUPS):
        upd = gscore[g] > bestv
        best = jnp.where(upd, g, best)
        bestv = jnp.where(upd, gscore[g], bestv)

    def pick(a, j):
        out = row(a, j)
        for g in range(1, N_GROUPS):
            out = jnp.where(best == g, row(a, n_e * g + j), out)
        return out

    v = [pick(sel, j) for j in range(n_e)]
    sc = [pick(scores, j) for j in range(n_e)]
    i1, v1 = jnp.zeros_like(best), v[0]
    for j in range(1, n_e):
        upd = v[j] > v1
        i1 = jnp.where(upd, j, i1)
        v1 = jnp.where(upd, v[j], v1)
    i2, v2 = None, None
    for j in range(n_e):
        ok = i1 != j
        if i2 is None:
            i2 = jnp.where(ok, j, n_e)
            v2 = jnp.where(ok, v[j], -jnp.inf)
        else:
            upd = ok & (v[j] > v2)
            i2 = jnp.where(upd, j, i2)
            v2 = jnp.where(upd, v[j], v2)
    s1 = sc[0]
    s2 = sc[0]
    for j in range(1, n_e):
        s1 = jnp.where(i1 == j, sc[j], s1)
        s2 = jnp.where(i2 == j, sc[j], s2)
    tot = s1 + s2
    w1, w2 = s1 / tot, s2 / tot
    lo = jnp.minimum(i1, i2)
    hi = jnp.maximum(i1, i2)
    base = jnp.where(lo == 0, 0, jnp.where(lo == 1, 3, 5))
    cls = best * len(_PAIRS) + base + hi - lo - 1
    first_is_lo = i1 < i2
    return cls, jnp.where(first_is_lo, w1, w2), jnp.where(first_is_lo, w2, w1)


def _ffn_input(xp_ref, xsm_ref, modp_ref, mods_ref, g_ref, D, n_p):
    is_s = pl.program_id(0) == n_p
    tm, ts = xp_ref.shape[0], xsm_ref.shape[0]
    pad_rows = lambda a: jnp.concatenate([a, jnp.zeros((tm - ts, a.shape[1]), a.dtype)], axis=0)
    x = jnp.where(is_s, pad_rows(xsm_ref[...]), xp_ref[...])
    mod = jnp.where(is_s, pad_rows(mods_ref[:, 3 * D:6 * D]), modp_ref[:, 3 * D:6 * D])
    h = _rms(x, g_ref[...]) * (1.0 + mod[:, D:2 * D]) + mod[:, 0:D]
    return x, mod, h


def _router_kernel(xp_ref, xsm_ref, modp_ref, mods_ref, g_ref, rwt_ref, rb_ref, cls_ref, gate_ref, *, D, n_p):
    _, _, h = _ffn_input(xp_ref, xsm_ref, modp_ref, mods_ref, g_ref, D, n_p)
    hb = h.astype(BF16)
    h_lo = (h - hb.astype(F32)).astype(BF16)
    rw = rwt_ref[...]
    rw_hi = rw.astype(BF16)
    rw_lo = (rw - rw_hi.astype(F32)).astype(BF16)
    logits_t = _dot_nt(rw_hi, hb) + (_dot_nt(rw_lo, hb) + _dot_nt(rw_hi, h_lo))
    cls, wa, wb = _route(logits_t, rb_ref[...])
    cls_ref[0] = cls
    rowi = lax.broadcasted_iota(jnp.int32, gate_ref.shape, 0)
    gate_ref[...] = jnp.where(rowi == 0, wa, jnp.where(rowi == 1, wb, 0.0))


def _moe_pre_kernel(pos_ref, xp_ref, xsm_ref, modp_ref, mods_ref, g_ref, gate_ref, wsg_ref, wsu_ref, wsd_ref,
                    hx_init, hx_hbm, xs_ref, hbuf, sem, *, D, n_p):
    del hx_init
    i = pl.program_id(0)
    slot = i % 2
    tm, ts = xp_ref.shape[0], xsm_ref.shape[0]

    def start_rows(n):
        for r in range(n):
            p = pos_ref[i * tm + r]
            pltpu.make_async_copy(hbuf.at[slot, pl.ds(r, 1)], hx_hbm.at[pl.ds(p, 1)], sem.at[slot]).start()

    def wait_rows(sl, n):
        pltpu.make_async_copy(hbuf.at[sl, pl.ds(0, n)], hx_hbm.at[pl.ds(0, n)], sem.at[sl]).wait()

    @pl.when(i >= 2)
    def _():
        wait_rows(slot, tm)
    x, mod, h = _ffn_input(xp_ref, xsm_ref, modp_ref, mods_ref, g_ref, D, n_p)
    gates = gate_ref[...]
    rowi = lax.broadcasted_iota(jnp.int32, (LANE, tm), 0)
    extra = jnp.where(rowi == 0, gates[0:1], jnp.where(rowi == 1, gates[1:2], 0.0)).T
    hbuf[slot, :, 0:D] = h
    hbuf[slot, :, D:D + LANE] = extra

    def shared_ffn():
        hb = h.astype(BF16)
        hid = _silu(_dot(hb, wsg_ref[...])) * _dot(hb, wsu_ref[...])
        xs_ref[...] = x + mod[:, 2 * D:3 * D] * _dot(hid.astype(BF16), wsd_ref[...])

    @pl.when(i < n_p)
    def _():
        start_rows(tm)
        shared_ffn()

    @pl.when(i == n_p)
    def _():
        start_rows(ts)
        shared_ffn()
        wait_rows(slot, ts)

        @pl.when(i >= 1)
        def _():
            wait_rows(1 - slot, tm)


def _router(gp, gs, x_p, x_s, mod_p, mod_s, norm_g, rw_t, rb):
    D = x_p.shape[-1]
    n_p, tm = gp.n_tiles, gp.tm
    assert gs.T <= tm and gp.T % gs.T == 0
    consts = (norm_g, rw_t, rb)
    cls, gates = pl.pallas_call(
        functools.partial(_router_kernel, D=D, n_p=n_p), grid=(n_p + 1,),
        in_specs=_ffn_input_specs(gp, x_s, mod_p, mod_s, lambda i: i) + [_const_spec(a) for a in consts],
        out_specs=[pl.BlockSpec((1, 1, tm), lambda i: (i, 0, 0)), pl.BlockSpec((None, 8, tm), lambda i: (i, 0, 0))],
        out_shape=[jax.ShapeDtypeStruct((n_p + 1, 1, tm), jnp.int32), jax.ShapeDtypeStruct((n_p + 1, 8, tm), F32)],
        name="router", compiler_params=_cparams(1),
    )(x_p, x_s, mod_p, mod_s, *consts)
    return cls.reshape(-1), gates


def _ffn_input_specs(gp, x_s, mod_p, mod_s, first):
    n_p, tm, tps = gp.n_tiles, gp.tm, gp.tiles_per_seq
    D = x_s.shape[-1]
    clamp = lambda i: jnp.minimum(i, n_p - 1)
    return [pl.BlockSpec((tm, D), lambda *a: (clamp(first(*a)), 0)),
            pl.BlockSpec(x_s.shape, lambda *a: (0, 0)),
            pl.BlockSpec((None, 1, mod_p.shape[-1]), lambda *a: (clamp(first(*a)) // tps, 0, 0)),
            pl.BlockSpec(mod_s.shape, lambda *a: (0, 0))]


def _moe_pre(gp, gs, x_p, x_s, mod_p, mod_s, norm_g, gates, pos_pad, n_rows, wsg, wsu, wsd):
    D = x_p.shape[-1]
    n_p, tm = gp.n_tiles, gp.tm
    T_all = gp.T + gs.T
    W = D + LANE
    take_i = lambda i, pos: i
    cst = lambda a: pl.BlockSpec(a.shape, lambda i, pos: (0,) * a.ndim)
    in_specs = _ffn_input_specs(gp, x_s, mod_p, mod_s, take_i)
    in_specs += [cst(norm_g), pl.BlockSpec((None, 8, tm), lambda i, pos: (i, 0, 0)), cst(wsg), cst(wsu), cst(wsd),
                 pl.BlockSpec(memory_space=pl.ANY)]
    grid_spec = pltpu.PrefetchScalarGridSpec(
        num_scalar_prefetch=1, grid=(n_p + 1,), in_specs=in_specs,
        out_specs=[pl.BlockSpec(memory_space=pl.ANY), pl.BlockSpec((tm, D), lambda i, pos: (i, 0))],
        scratch_shapes=[pltpu.VMEM((2, tm, W), F32), pltpu.SemaphoreType.DMA((2,))])
    hx, xs = pl.pallas_call(
        functools.partial(_moe_pre_kernel, D=D, n_p=n_p), grid_spec=grid_spec,
        out_shape=[jax.ShapeDtypeStruct((n_rows, W), F32), jax.ShapeDtypeStruct((T_all, D), F32)],
        input_output_aliases={10: 0}, name="moe_pre", compiler_params=_cparams(1),
    )(pos_pad, x_p, x_s, mod_p, mod_s, norm_g, gates, wsg, wsu, wsd, jnp.zeros((n_rows, W), F32))
    return hx, xs


def _moe_kernel(pos_ref, te1_ref, te2_ref, nv_ref, dump_ref, nused_ref, hx_ref, wg1, wu1, wd1, wg2, wu2, wd2,
                y_hbm, src_ref, obuf, ssem, *, T_all, n_rows, tm, D):
    i = pl.program_id(0)
    nused = nused_ref[0]
    slot = i % 2

    def scatter_row(tile, sl, r):
        nv = nv_ref[tile]
        t = jnp.where(r < nv, src_ref[tile * tm + r], dump_ref[tile] + (r - nv))
        pltpu.make_async_copy(obuf.at[sl, pl.ds(r, 1)], y_hbm.at[pl.ds(t, 1)], ssem.at[sl]).start()

    def rolled_scatter(tile, sl):
        def step(r, c):
            scatter_row(tile, sl, r)
            return c
        lax.fori_loop(0, tm, step, 0, unroll=8)

    def wait_scatter(sl):
        pltpu.make_async_copy(obuf.at[sl], y_hbm.at[pl.ds(0, tm)], ssem.at[sl]).wait()

    def compute(sl):
        hx = hx_ref[...]
        hb = hx[:, 0:D].astype(BF16)
        wa = hx[:, D:D + 1]
        wb = hx[:, D + 1:D + 2]
        bf = lambda w_ref: w_ref[...].astype(BF16)
        hid_a = (_silu(_dot(hb, bf(wg1))) * _dot(hb, bf(wu1)) * wa).astype(BF16)
        hid_b = (_silu(_dot(hb, bf(wg2))) * _dot(hb, bf(wu2)) * wb).astype(BF16)
        obuf[sl] = _dot(hid_a, bf(wd1)) + _dot(hid_b, bf(wd2))

    @pl.when(i == 0)
    def _():
        def init(r, c):
            src_ref[r] = 0
            return c
        lax.fori_loop(0, n_rows, init, 0, unroll=8)

        def fill(t, c):
            src_ref[pos_ref[t]] = t
            return c
        lax.fori_loop(0, T_all, fill, 0, unroll=8)

    active = i < nused
    steady = jnp.logical_and(i >= 1, i + 1 < nused)

    @pl.when(jnp.logical_and(active, i >= 2))
    def _():
        wait_scatter(slot)

    @pl.when(steady)
    def _():
        compute(slot)
        for r in range(tm):
            scatter_row(i - 1, 1 - slot, r)

    @pl.when(jnp.logical_and(active, jnp.logical_not(steady)))
    def _():
        compute(slot)

        @pl.when(i >= 1)
        def _():
            rolled_scatter(i - 1, 1 - slot)

        @pl.when(i == nused - 1)
        def _():
            rolled_scatter(i, slot)
            wait_scatter(slot)

            @pl.when(i >= 1)
            def _():
                wait_scatter(1 - slot)

    @pl.when(jnp.logical_not(active))
    def _():
        obuf[slot] = jnp.zeros(obuf.shape[1:], F32)
        rolled_scatter(i, slot)
        wait_scatter(slot)


def _moe_plan(cls, T_all):
    tm = MOE_TILE
    n_tiles = (T_all + N_CLASSES * (tm - 1)) // tm
    n_rows = n_tiles * tm
    cls = cls[:T_all]
    onehot = (cls[:, None] == jnp.arange(N_CLASSES, dtype=jnp.int32)[None, :]).astype(jnp.int32)
    counts = jnp.sum(onehot, axis=0)
    rank = jnp.sum((jnp.cumsum(onehot, axis=0) - onehot) * onehot, axis=1)
    ntile_c = (counts + tm - 1) // tm
    tile_end = jnp.cumsum(ntile_c)
    tile_off = tile_end - ntile_c
    pos = (jnp.sum(onehot * tile_off[None, :], axis=1) * tm + rank).astype(jnp.int32)
    nused = tile_end[-1]
    tiles = jnp.arange(n_tiles, dtype=jnp.int32)
    tid = jnp.minimum(tiles, nused - 1)
    tile_cls = jnp.sum((tid[:, None] >= tile_end[None, :]).astype(jnp.int32), axis=1)
    nvalid = jnp.clip(counts[tile_cls] - (tiles - tile_off[tile_cls]) * tm, 0, tm)
    nvalid = jnp.where(tiles < nused, nvalid, 0).astype(jnp.int32)
    dump = (T_all + tiles * tm - (jnp.cumsum(nvalid) - nvalid)).astype(jnp.int32)
    pair_lo = jnp.array([a for a, _ in _PAIRS], jnp.int32)
    pair_hi = jnp.array([b for _, b in _PAIRS], jnp.int32)
    grp_id, pair_id = tile_cls // len(_PAIRS), tile_cls % len(_PAIRS)
    te1 = (grp_id * EXPERTS_PER_GROUP + pair_lo[pair_id]).astype(jnp.int32)
    te2 = (grp_id * EXPERTS_PER_GROUP + pair_hi[pair_id]).astype(jnp.int32)
    return n_rows, (pos, te1, te2, nvalid, dump, nused.reshape(1).astype(jnp.int32))


def _moe_routed(hx, plan, T_all, layer, wg, wu, wd):
    n_rows, W = hx.shape
    D, Fe = wg.shape[-2:]
    tm = MOE_TILE
    w1 = lambda shape: pl.BlockSpec((None, None) + shape, lambda i, pos, te1, *_: (layer, te1[i], 0, 0))
    w2 = lambda shape: pl.BlockSpec((None, None) + shape, lambda i, pos, te1, te2, *_: (layer, te2[i], 0, 0))
    grid_spec = pltpu.PrefetchScalarGridSpec(
        num_scalar_prefetch=6, grid=(n_rows // tm,),
        in_specs=[pl.BlockSpec((tm, W), lambda i, *_: (i, 0)),
                  w1((D, Fe)), w1((D, Fe)), w1((Fe, D)), w2((D, Fe)), w2((D, Fe)), w2((Fe, D))],
        out_specs=pl.BlockSpec(memory_space=pl.ANY),
        scratch_shapes=[pltpu.SMEM((n_rows,), jnp.int32), pltpu.VMEM((2, tm, D), F32),
                        pltpu.SemaphoreType.DMA((2,))])
    return pl.pallas_call(
        functools.partial(_moe_kernel, T_all=T_all, n_rows=n_rows, tm=tm, D=D),
        grid_spec=grid_spec, out_shape=jax.ShapeDtypeStruct((n_rows, D), F32),
        name="moe_routed", compiler_params=_cparams(1),
    )(*plan, hx, wg, wu, wd, wg, wu, wd)


def _kv_kernel(xs_ref, y_ref, pmod_ref, kvmod_ref, cos_ref, sin_ref, g_ref, wc_ref, wp_ref, wr_ref, lg_ref,
               xo_ref, ckv_ref, kpe_ref, kcat_ref, *maybe_vt_ref, D, C, R):
    x = _resid(xs_ref, y_ref, pmod_ref, D)
    xo_ref[...] = x
    hn = (_rms(x, g_ref[...]) * (1.0 + kvmod_ref[:, D:2 * D]) + kvmod_ref[:, 0:D]).astype(BF16)
    ckv = _rms(_dot(hn, wc_ref[...]), lg_ref[...])
    kpe = _dot(hn, wp_ref[...]) * cos_ref[...] + _dot(hn, wr_ref[...]) * sin_ref[...]
    ckv_ref[...] = ckv
    kpe_ref[...] = kpe[:, 0:R]
    kcat_ref[:, 0:C] = ckv.astype(kcat_ref.dtype)
    kcat_ref[:, C:C + LANE] = kpe.astype(kcat_ref.dtype)
    if maybe_vt_ref:
        maybe_vt_ref[0][...] = ckv.T.astype(maybe_vt_ref[0].dtype)


def _kv_stream(grp, xs, y, y_off, pmod, kvmod, cos, sin, consts, C, R, kcat_dtype, with_vt):
    D = xs.shape[-1]
    in_specs = [grp.tok_spec(D, y_off), grp.tok_spec(D, y_off), grp.mod_spec(pmod.shape[-1]),
                grp.mod_spec(kvmod.shape[-1]), grp.rope_spec(LANE), grp.rope_spec(LANE)]
    in_specs += [_const_spec(a) for a in consts]
    widths = [(D, F32), (C, F32), (R, F32), (C + LANE, kcat_dtype)]
    out_specs = [grp.tok_spec(w) for w, _ in widths]
    out_shape = [jax.ShapeDtypeStruct((grp.T, w), dt) for w, dt in widths]
    if with_vt:
        tps = grp.tiles_per_seq
        out_specs.append(pl.BlockSpec((None, C, grp.tm), lambda i: (i // tps, 0, i % tps)))
        out_shape.append(jax.ShapeDtypeStruct((grp.n_seq, C, grp.seq_len), BF16))
    return _tile_call("kv_stream", functools.partial(_kv_kernel, D=D, C=C, R=R), grp, in_specs,
                      [xs, y, pmod, kvmod, cos, sin, *consts], out_specs, out_shape)


def _mla_q_kernel(*refs, has_y, transposed, D, C, H, scale):
    refs = list(refs)
    xs_ref = refs.pop(0)
    y_ref = refs.pop(0) if has_y else None
    pmod_ref = refs.pop(0) if has_y else None
    mod_ref, cos_ref, sin_ref, g_ref, wdq_ref, qg_ref, wn_ref, wp_ref, wr_ref, wuk_ref = refs[:10]
    outs = refs[10:]
    if has_y:
        xo_ref, outs = outs[0], outs[1:]
    qcat_ref, = outs
    x = _resid(xs_ref, y_ref, pmod_ref, D)
    if has_y:
        xo_ref[...] = x
    h = (_rms(x, g_ref[...]) * (1.0 + mod_ref[:, D:2 * D]) + mod_ref[:, 0:D]).astype(BF16)
    cq = _rms(_dot(h, wdq_ref[...]), qg_ref[...]).astype(BF16)
    q_nope = _dot(cq, wn_ref[...]).astype(BF16)
    cos, sin = cos_ref[...], sin_ref[...]
    dn = q_nope.shape[-1] // H
    for hh in range(H):
        ls = slice(hh * LANE, (hh + 1) * LANE)
        q_lat = _dot(q_nope[:, hh * dn:(hh + 1) * dn], wuk_ref[hh])
        q_pe = _dot(cq, wp_ref[:, ls]) * cos + _dot(cq, wr_ref[:, ls]) * sin
        if transposed:
            tm = q_lat.shape[0]
            qcat_ref[0:C, hh * tm:(hh + 1) * tm] = (q_lat * scale).T.astype(qcat_ref.dtype)
            qcat_ref[C:C + LANE, hh * tm:(hh + 1) * tm] = (q_pe * scale).T.astype(qcat_ref.dtype)
        else:
            qcat_ref[hh, :, 0:C] = (q_lat * scale).astype(qcat_ref.dtype)
            qcat_ref[hh, :, C:C + LANE] = (q_pe * scale).astype(qcat_ref.dtype)


def _mla_q(grp, xs, y, y_off, pmod, mod, cos, sin, norm_g, wq, out_dtype, transposed):
    D = xs.shape[-1]
    w_dq, qg, w_n, w_p, w_r, w_uk = wq
    C = w_uk.shape[-1]
    H = MLA_HEADS
    has_y = y is not None
    scale = float(w_uk.shape[1] + QK_ROPE) ** -0.5
    if transposed:
        scale *= LOG2_E
    in_specs = [grp.tok_spec(D, y_off if has_y else 0)]
    in_specs += [grp.tok_spec(D, y_off), grp.mod_spec(pmod.shape[-1])] if has_y else []
    args = [xs] + ([y, pmod] if has_y else [])
    in_specs += [grp.mod_spec(mod.shape[-1]), grp.rope_spec(LANE), grp.rope_spec(LANE)]
    args += [mod, cos, sin]
    for a in (norm_g, w_dq, qg, w_n, w_p, w_r, w_uk):
        in_specs.append(_const_spec(a)); args.append(a)
    if transposed:
        q_spec = pl.BlockSpec((None, C + LANE, H * grp.tm), lambda i: (i, 0, 0))
        q_shape = jax.ShapeDtypeStruct((grp.n_tiles, C + LANE, H * grp.tm), out_dtype)
    else:
        q_spec = pl.BlockSpec((H, grp.tm, C + LANE), lambda i: (0, i, 0))
        q_shape = jax.ShapeDtypeStruct((H, grp.T, C + LANE), out_dtype)
    out_specs = ([grp.tok_spec(D)] if has_y else []) + [q_spec]
    out_shape = ([jax.ShapeDtypeStruct((grp.T, D), F32)] if has_y else []) + [q_shape]
    res = _tile_call(
        "mla_q", functools.partial(_mla_q_kernel, has_y=has_y, transposed=transposed, D=D, C=C, H=H, scale=scale),
        grp, in_specs, args, out_specs, out_shape)
    return (res[0], res[1]) if has_y else (xs, res[0])


def _mla_out(o_lat_heads, x, g1, wuvt_ref, wo_ref):
    parts = [_dot(o.astype(BF16), wuvt_ref[hh]).astype(BF16) for hh, o in enumerate(o_lat_heads)]
    return x + g1 * _dot(jnp.concatenate(parts, axis=-1), wo_ref[...])


def _attn_kernel(x_ref, qt_ref, k_ref, vt_ref, mod_ref, wuvt_ref, wo_ref, xo_ref, m_ref, l_ref, acc_ref, s0_ref,
                 *, D, C, H, tq, tk, cw):
    qi = pl.program_id(1)
    m_ref[...] = jnp.full(m_ref.shape, -jnp.inf, F32)
    l_ref[...] = jnp.zeros(l_ref.shape, F32)
    acc_ref[...] = jnp.zeros(acc_ref.shape, F32)
    n_col = H * tq

    def keys(j):
        return k_ref[pl.ds(pl.multiple_of(j * tk, tk), tk), :]

    s0_ref[...] = _dot(keys(0), qt_ref[:, 0:cw])

    def block(j, masked):
        kb = keys(j)
        vt = vt_ref[:, pl.ds(pl.multiple_of(j * tk, tk), tk)]
        if masked:
            kpos = lax.broadcasted_iota(jnp.int32, (tk, cw), 0)
            qpos = lax.broadcasted_iota(jnp.int32, (tk, cw), 1) & (tq - 1)
            keep = kpos <= qpos
        s_next = s0_ref[...]
        for c0 in range(0, n_col, cw):
            cs = slice(c0, c0 + cw)
            s = s_next
            if c0 + cw < n_col:
                s_next = _dot(kb, qt_ref[:, c0 + cw:c0 + 2 * cw])
            elif not masked:
                s0_ref[...] = _dot(keys(j + 1), qt_ref[:, 0:cw])
            if masked:
                s = jnp.where(keep, s, -1e30)
            m_old = m_ref[:, cs]
            m_new = jnp.maximum(m_old, jnp.max(s, axis=0, keepdims=True))
            alpha = jnp.exp2(m_old - m_new)
            p = jnp.exp2(s - m_new)
            l_ref[:, cs] = alpha * l_ref[:, cs] + jnp.sum(p, axis=0, keepdims=True)
            acc_ref[:, cs] = alpha * acc_ref[:, cs] + _dot(vt, p.astype(BF16))
            m_ref[:, cs] = m_new

    def body(j, c):
        block(j, False)
        return c
    lax.fori_loop(0, qi, body, 0)
    block(qi, True)
    parts = []
    for hh in range(H):
        cs = slice(hh * tq, (hh + 1) * tq)
        o_t = (acc_ref[:, cs] / l_ref[:, cs]).astype(BF16)
        parts.append(_dot_tn(o_t, wuvt_ref[hh]).astype(BF16))
    y = _dot(jnp.concatenate(parts, axis=-1), wo_ref[...])
    xo_ref[...] = x_ref[...] + mod_ref[:, 2 * D:3 * D] * y


def _attn_prompt(B, S, x, qt, kcat, vt, mod, w_uvt, w_o):
    D = x.shape[-1]
    H = MLA_HEADS
    _, Wq, n_col = qt.shape
    C = w_uvt.shape[1]
    tq, tk = ATT_TQ, ATT_TK
    assert tq == tk and tq & (tq - 1) == 0 and n_col == H * tq
    nq = S // tq
    cst = lambda a: pl.BlockSpec(a.shape, lambda b, i: (0,) * a.ndim)
    return pl.pallas_call(
        functools.partial(_attn_kernel, D=D, C=C, H=H, tq=tq, tk=tk, cw=ATT_CW), grid=(B, nq),
        in_specs=[pl.BlockSpec((tq, D), lambda b, i: (b * nq + i, 0)),
                  pl.BlockSpec((None, Wq, n_col), lambda b, i: (b * nq + i, 0, 0)),
                  pl.BlockSpec((S, Wq), lambda b, i: (b, 0)),
                  pl.BlockSpec((None, C, S), lambda b, i: (b, 0, 0)),
                  pl.BlockSpec((None, 1, mod.shape[-1]), lambda b, i: (b, 0, 0)), cst(w_uvt), cst(w_o)],
        out_specs=pl.BlockSpec((tq, D), lambda b, i: (b * nq + i, 0)),
        out_shape=jax.ShapeDtypeStruct(x.shape, F32),
        scratch_shapes=[pltpu.VMEM((1, n_col), F32), pltpu.VMEM((1, n_col), F32), pltpu.VMEM((C, n_col), F32),
                        pltpu.VMEM((tk, ATT_CW), F32)],
        name="attn_prompt", compiler_params=_cparams(2),
    )(x, qt, kcat, vt, mod, w_uvt, w_o)


def _attn_dec_kernel(pt_ref, q_ref, kn_ref, ckv_hbm, kpe_hbm, o_ref, cbuf, pbuf, kb_ref, s_ref, csem, psem,
                     *, C, R, n_pages, page, cp):
    b = pl.program_id(0)
    nb = pl.num_programs(0)
    slot = b % 2

    def start(seq, sl):
        def body(p, c):
            pg = pt_ref[seq * n_pages + p]
            pltpu.make_async_copy(ckv_hbm.at[pl.ds(pg, 1)], cbuf.at[sl, pl.ds(p, 1)], csem.at[sl]).start()
            pltpu.make_async_copy(kpe_hbm.at[pl.ds(pg, 1)], pbuf.at[sl, pl.ds(p, 1)], psem.at[sl]).start()
            return c
        lax.fori_loop(0, n_pages, body, 0, unroll=4)

    @pl.when(b == 0)
    def _():
        start(0, 0)

    @pl.when(b + 1 < nb)
    def _():
        start(b + 1, 1 - slot)

    pltpu.make_async_copy(ckv_hbm.at[pl.ds(0, n_pages)], cbuf.at[slot], csem.at[slot]).wait()
    pltpu.make_async_copy(kpe_hbm.at[pl.ds(0, n_pages)], pbuf.at[slot], psem.at[slot]).wait()

    q = q_ref[...]
    ql = q[:, 0:C].astype(BF16)
    qp = q[:, C:C + R].astype(BF16)
    ck = cp * page
    n_chunks = n_pages // cp
    for c in range(n_chunks):
        kc = cbuf[slot, c * cp:(c + 1) * cp].reshape(ck, C).astype(BF16)
        pct = jnp.concatenate([pbuf[slot, c * cp + i] for i in range(cp)], axis=-1).astype(BF16)
        kb_ref[c * ck:(c + 1) * ck, :] = kc
        s_ref[:, c * ck:(c + 1) * ck] = _dot_nt(ql, kc) + _dot(qp, pct)
    kn = kn_ref[...]
    s_new = jnp.sum(q * kn, axis=-1, keepdims=True)
    s = s_ref[...]
    m = jnp.maximum(jnp.max(s, axis=-1, keepdims=True), s_new)
    p = jnp.exp(s - m)
    p_new = jnp.exp(s_new - m)
    l = jnp.sum(p, axis=-1, keepdims=True) + p_new
    acc = p_new * kn[:, 0:C]
    pb = p.astype(BF16)
    for c in range(n_chunks):
        acc = acc + _dot(pb[:, c * ck:(c + 1) * ck], kb_ref[c * ck:(c + 1) * ck, :])
    o_ref[...] = acc / l


def _attn_dec(q_s, kn, cache_ckv, kpe_pages, page_table):
    B, H, Wq = q_s.shape
    _, page, C = cache_ckv.shape
    R = kpe_pages.shape[1]
    n_pages = page_table.shape[1]
    P = n_pages * page
    any_spec = pl.BlockSpec(memory_space=pl.ANY)
    grid_spec = pltpu.PrefetchScalarGridSpec(
        num_scalar_prefetch=1, grid=(B,),
        in_specs=[pl.BlockSpec((None, H, Wq), lambda b, pt: (b, 0, 0)),
                  pl.BlockSpec((None, 1, Wq), lambda b, pt: (b, 0, 0)), any_spec, any_spec],
        out_specs=pl.BlockSpec((None, H, C), lambda b, pt: (b, 0, 0)),
        scratch_shapes=[pltpu.VMEM((2, n_pages, page, C), F32), pltpu.VMEM((2, n_pages, R, page), F32),
                        pltpu.VMEM((P, C), BF16), pltpu.VMEM((H, P), F32),
                        pltpu.SemaphoreType.DMA((2,)), pltpu.SemaphoreType.DMA((2,))])
    return pl.pallas_call(
        functools.partial(_attn_dec_kernel, C=C, R=R, n_pages=n_pages, page=page, cp=DEC_CHUNK_PAGES),
        grid_spec=grid_spec, out_shape=jax.ShapeDtypeStruct((B, H, C), F32), name="attn_dec",
        compiler_params=_cparams(1),
    )(page_table.reshape(-1), q_s, kn, cache_ckv, kpe_pages)


def _mla_out_kernel(x_ref, o_ref, mod_ref, wuvt_ref, wo_ref, xo_ref, *, D, H):
    heads = [o_ref[hh] for hh in range(H)]
    xo_ref[...] = _mla_out(heads, x_ref[...], mod_ref[:, 2 * D:3 * D], wuvt_ref, wo_ref)


def _final_kernel(xs_ref, y_ref, pmod_ref, g_ref, o_ref, *, D):
    o_ref[...] = _rms(_resid(xs_ref, y_ref, pmod_ref, D), g_ref[...])


def _rope_table(pos, d, width):
    inv = ROPE_THETA ** (-jnp.arange(0, d, 2, dtype=F32) / d)
    ang = pos.astype(F32)[:, None] * inv[None, :]
    cos, sin = jnp.cos(ang), jnp.sin(ang)
    pad = jnp.zeros((pos.shape[0], width - d), F32)
    return jnp.concatenate([cos, cos, pad], axis=-1), jnp.concatenate([sin, sin, pad], axis=-1)


def _rot_half_cols(w):
    half = w.shape[-1] // 2
    return jnp.concatenate([-w[..., half:], w[..., :half]], axis=-1)


def _pad_lanes(w):
    pad = jnp.zeros(w.shape[:-1] + (LANE - w.shape[-1],), w.dtype)
    return jnp.concatenate([w, pad], axis=-1)


def _ret_tables(L):
    log_g = jnp.log1p(-jnp.exp2(-5.0 - jnp.arange(RET_HEADS, dtype=F32)))
    idx = jnp.arange(L, dtype=F32)
    diff = idx[:, None] - idx[None, :]
    dec = jnp.where(diff[None] >= 0, jnp.exp(jnp.maximum(diff, 0.0)[None] * log_g[:, None, None]), 0.0)
    qd = jnp.exp((idx + 1.0)[:, None] * log_g[None, :])
    kd = jnp.exp((L - 1.0 - idx)[:, None] * log_g[None, :])
    gl = jnp.exp(L * log_g)[None, :]
    return dec, qd, kd, gl


def kernel(x_prompt, x_sample, c_prompt, c_sample, state_ret, cache_ckv, cache_kpe, page_table, w_ada, b_ada, norm_mix_g, norm_ffn_g, ret_w_in, ret_gn_g, ret_w_out, kv_w_ada, kv_b_ada, kv_norm_g, mla_w_dkv, mla_kv_norm_g, mla_w_uk, mla_w_uv, mla_w_dq, mla_q_norm_g, mla_w_uq, mla_w_o, router_w, router_bias, moe_w_gate, moe_w_up, moe_w_down, shared_w_gate, shared_w_up, shared_w_down, final_norm_g):
    B, S, D = x_prompt.shape
    Bs, Ss, _ = x_sample.shape
    assert Ss == 1
    depth = w_ada.shape[0]
    n_a = ret_w_in.shape[0]
    H = MLA_HEADS
    C = mla_w_dkv.shape[1] - QK_ROPE
    R = QK_ROPE
    dn = mla_w_uk.shape[1]
    past_len = page_table.shape[1] * cache_ckv.shape[1]

    gp = _Group(B, S, TM_PROMPT, False)
    gs = _Group(Bs, 1, Bs, True)
    groups = (gp, gs)
    T_all = gp.T + gs.T
    ys_off = gp.T // gs.tm

    n_c = B + Bs
    n_c_pad = -(-n_c // 8) * 8
    c_all = jnp.concatenate([c_prompt, c_sample, jnp.zeros((n_c_pad - n_c, D), F32)], axis=0)
    mod_all = _ada_mod(c_all, w_ada, b_ada)
    kvmod_all = _ada_mod(c_all, kv_w_ada[None], kv_b_ada[None])[0]
    mods = [[g.mod_array(mod_all[l, lo:lo + g.n_seq]) for l in range(depth)]
            for g, lo in zip(groups, (0, B))]
    kvmods = [g.mod_array(kvmod_all[lo:lo + g.n_seq]) for g, lo in zip(groups, (0, B))]

    pos = (jnp.arange(S), past_len + jnp.arange(1))
    ret_rope = [_rope_table(p, D // RET_HEADS, D // RET_HEADS)[0:2] for p in pos]
    ret_rope = [(c[:, :c.shape[1] // 2], s[:, :s.shape[1] // 2]) for c, s in ret_rope]
    mla_rope = [_rope_table(p, R, LANE) for p in pos]
    tabs_p = _ret_tables(RET_CHUNK)
    gl_s = _ret_tables(1)[3]

    row = lambda g: g.reshape(1, -1)
    w_in = ret_w_in.astype(BF16)
    w_out = ret_w_out.astype(BF16)
    wsg, wsu, wsd = shared_w_gate.astype(BF16), shared_w_up.astype(BF16), shared_w_down.astype(BF16)
    rw_t = router_w.T
    rb = router_bias.reshape(-1, 1)
    w_dkv_c = mla_w_dkv[:, :C].astype(BF16)
    w_dkv_p = _pad_lanes(mla_w_dkv[:, C:]).astype(BF16)
    w_dkv_r = _pad_lanes(_rot_half_cols(mla_w_dkv[:, C:])).astype(BF16)
    w_uk = mla_w_uk.astype(BF16)
    w_uvt = jnp.swapaxes(mla_w_uv, 1, 2).astype(BF16)
    wq = []
    for j in range(depth - n_a):
        w3 = mla_w_uq[j].reshape(-1, H, dn + R)
        w_n = w3[:, :, :dn].reshape(-1, H * dn).astype(BF16)
        w_p = _pad_lanes(w3[:, :, dn:]).reshape(-1, H * LANE).astype(BF16)
        w_r = _pad_lanes(_rot_half_cols(w3[:, :, dn:])).reshape(-1, H * LANE).astype(BF16)
        wq.append((mla_w_dq[j].astype(BF16), row(mla_q_norm_g[j]), w_n, w_p, w_r, w_uk))
    w_o = mla_w_o.astype(BF16)

    kpe_pages = jnp.swapaxes(cache_kpe, 1, 2)

    xs = [x_prompt.reshape(gp.T, D), x_sample.reshape(gs.T, D)]
    y = None
    states_p, state_s = [], None
    ckv_out, kpe_out, kcat = [None, None], [None, None], [None, None]

    for l in range(depth):
        y_offs = (0, ys_off)
        xmid = [None, None]
        if l < n_a:
            for gi, g in enumerate(groups):
                pm = mods[gi][l - 1] if y is not None else None
                x, q, k, v, gt = _ret_in(g, xs[gi], y, y_offs[gi], pm, mods[gi][l],
                                         ret_rope[gi][0], ret_rope[gi][1],
                                         row(norm_mix_g[l]), w_in[l], BF16 if gi == 0 else F32)
                if gi == 0:
                    xmid[gi], st = _ret_chunk(B, S, x, q, k, v, gt, mods[gi][l], tabs_p, row(ret_gn_g[l]), w_out[l])
                else:
                    state_s, o = _ret_dec(state_ret if l == 0 else state_s, l, q, k, v, gl_s, in_place=l > 0)
                    xmid[gi], = _tok_call("ret_post", functools.partial(_ret_post_kernel, D=D, H=RET_HEADS), g,
                                          [(x, 0), (o, 0), (gt, 0)], [mods[gi][l]], [],
                                          [row(ret_gn_g[l]), w_out[l]], [(D, F32)])
                if gi == 0:
                    states_p.append(st)
        else:
            j = l - n_a
            for gi, g in enumerate(groups):
                xsg, yy, pm = xs[gi], y, mods[gi][l - 1]
                if l == n_a:
                    res = _kv_stream(g, xsg, y, y_offs[gi], pm, kvmods[gi],
                                     mla_rope[gi][0], mla_rope[gi][1],
                                     [row(kv_norm_g), w_dkv_c, w_dkv_p, w_dkv_r, row(mla_kv_norm_g)], C, R,
                                     BF16 if gi == 0 else F32, gi == 0)
                    xsg, ckv_out[gi], kpe_out[gi], kcat[gi] = res[:4]
                    if gi == 0:
                        vt_p = res[4]
                    yy, pm = None, None
                x, qcat = _mla_q(g, xsg, yy, y_offs[gi], pm, mods[gi][l],
                                 mla_rope[gi][0], mla_rope[gi][1],
                                 row(norm_mix_g[l]), wq[j], BF16 if gi == 0 else F32, gi == 0)
                if gi == 0:
                    xmid[gi] = _attn_prompt(B, S, x, qcat, kcat[gi], vt_p, mods[gi][l], w_uvt, w_o[j])
                else:
                    o_lat = _attn_dec(jnp.swapaxes(qcat, 0, 1), kcat[gi][:, None, :], cache_ckv, kpe_pages, page_table)
                    xmid[gi] = pl.pallas_call(
                        functools.partial(_mla_out_kernel, D=D, H=H), grid=(1,),
                        in_specs=[_const_spec(x), pl.BlockSpec((H, Bs, C), lambda i: (0, 0, 0)),
                                  _const_spec(mods[gi][l]), _const_spec(w_uvt), _const_spec(w_o[j])],
                        out_specs=_const_spec(x), out_shape=jax.ShapeDtypeStruct(x.shape, F32),
                        name="mla_out", compiler_params=_cparams(1),
                    )(x, jnp.swapaxes(o_lat, 0, 1), mods[gi][l], w_uvt, w_o[j])
        ffn_in = (gp, gs, xmid[0], xmid[1], mods[0][l], mods[1][l], row(norm_ffn_g[l]))
        cls, gates = _router(*ffn_in, rw_t, rb)
        n_rows, plan = _moe_plan(cls, T_all)
        pos_pad = jnp.concatenate([plan[0], jnp.zeros((cls.shape[0] - T_all,), jnp.int32)])
        hx, xs_all = _moe_pre(*ffn_in, gates, pos_pad, n_rows, wsg[l], wsu[l], wsd[l])
        xs = [xs_all, xs_all]
        y = _moe_routed(hx, plan, T_all, l, moe_w_gate, moe_w_up, moe_w_down)

    outs = []
    for gi, g in enumerate(groups):
        o, = _tok_call("final_norm", functools.partial(_final_kernel, D=D), g,
                       [(xs[gi], (0, ys_off)[gi]), (y, (0, ys_off)[gi])],
                       [mods[gi][depth - 1]], [], [row(final_norm_g)], [(D, F32)])
        outs.append(o)
    return (outs[0].reshape(B, S, D), outs[1].reshape(Bs, 1, D),
            jnp.stack(states_p), state_s,
            ckv_out[0].reshape(B, S, C), kpe_out[0].reshape(B, S, R),
            ckv_out[1].reshape(Bs, 1, C), kpe_out[1].reshape(Bs, 1, R))
```

```python
import functools

import jax
import jax.numpy as jnp
from jax import lax
from jax.experimental import pallas as pl
from jax.experimental.pallas import tpu as pltpu

F32 = jnp.float32
BF16 = jnp.bfloat16

RET_HEADS = 4
MLA_HEADS = 8
QK_ROPE = 64
N_GROUPS = 4
EXPERTS_PER_GROUP = 4
ROPE_THETA = 10000.0
EPS = 1e-6
LOG2_E = 1.4426950408889634

_PAIRS = ((0, 1), (0, 2), (0, 3), (1, 2), (1, 3), (2, 3))
N_CLASSES = N_GROUPS * len(_PAIRS)

LANE = 128
VMEM_LIMIT_BYTES = 56 * 2 ** 20
TM_PROMPT = 256
MOE_TILE = 256
RET_CHUNK = 256
ATT_TQ = 256
ATT_TK = 256
ATT_CW = 512
DEC_NB = 8
DEC_CHUNK_PAGES = 8


def _dot(a, b):
    return jnp.dot(a, b, preferred_element_type=F32)


def _dot_nt(a, b):
    return lax.dot_general(a, b, (((1,), (1,)), ((), ())), preferred_element_type=F32)


def _dot_tn(a, b):
    return lax.dot_general(a, b, (((0,), (0,)), ((), ())), preferred_element_type=F32)


def _silu(x):
    return x * jax.nn.sigmoid(x)


def _rms(x, g):
    return x * lax.rsqrt(jnp.mean(x * x, axis=-1, keepdims=True) + EPS) * g


def _cparams(n_axes=1):
    return pltpu.CompilerParams(dimension_semantics=("arbitrary",) * n_axes,
                                vmem_limit_bytes=VMEM_LIMIT_BYTES)


class _Group:
    def __init__(self, n_seq, seq_len, tm, per_token_mod):
        self.n_seq, self.seq_len, self.tm = n_seq, seq_len, tm
        self.T = n_seq * seq_len
        self.n_tiles = self.T // tm
        self.per_token_mod = per_token_mod
        self.tiles_per_seq = max(seq_len // tm, 1)

    def tok_spec(self, width, offset=0):
        return pl.BlockSpec((self.tm, width), lambda i: (i + offset, 0))

    def mod_spec(self, width):
        if self.per_token_mod:
            return pl.BlockSpec((self.tm, width), lambda i: (i, 0))
        tps = self.tiles_per_seq
        return pl.BlockSpec((None, 1, width), lambda i: (i // tps, 0, 0))

    def rope_spec(self, width):
        if self.per_token_mod:
            return pl.BlockSpec((1, width), lambda i: (0, 0))
        tps = self.tiles_per_seq
        return pl.BlockSpec((self.tm, width), lambda i: (i % tps, 0))

    def mod_array(self, m):
        return m if self.per_token_mod else m[:, None, :]


def _const_spec(a):
    nd = a.ndim
    return pl.BlockSpec(a.shape, lambda i: (0,) * nd)


def _tile_call(name, body, grp, in_specs, args, out_specs, out_shape):
    return pl.pallas_call(body, grid=(grp.n_tiles,), in_specs=in_specs, out_specs=out_specs,
                          out_shape=out_shape, name=name, compiler_params=_cparams(1))(*args)


def _tok_call(name, body, grp, tok_ins, mod_ins, rope_ins, const_ins, outs):
    in_specs, args = [], []
    for a, off in tok_ins:
        in_specs.append(grp.tok_spec(a.shape[-1], off)); args.append(a)
    for a in mod_ins:
        in_specs.append(grp.mod_spec(a.shape[-1])); args.append(a)
    for a in rope_ins:
        in_specs.append(grp.rope_spec(a.shape[-1])); args.append(a)
    for a in const_ins:
        in_specs.append(_const_spec(a)); args.append(a)
    out_specs = [grp.tok_spec(width) for width, _ in outs]
    out_shape = [jax.ShapeDtypeStruct((grp.T, width), dtype) for width, dtype in outs]
    return _tile_call(name, body, grp, in_specs, args, out_specs, out_shape)


def _mod_kernel(c_ref, w_ref, b_ref, o_ref):
    c = c_ref[...]
    o_ref[...] = _dot(_silu(c).astype(BF16), w_ref[...].astype(BF16)) + b_ref[...]


def _ada_mod(c_all, w, b):
    L, D, N = w.shape
    Bc = c_all.shape[0]
    tn = min(N, 2048)
    return pl.pallas_call(
        _mod_kernel, grid=(L, N // tn),
        in_specs=[pl.BlockSpec((Bc, D), lambda l, j: (0, 0)),
                  pl.BlockSpec((None, D, tn), lambda l, j: (l, 0, j)),
                  pl.BlockSpec((None, 1, tn), lambda l, j: (l, 0, j))],
        out_specs=pl.BlockSpec((None, Bc, tn), lambda l, j: (l, 0, j)),
        out_shape=jax.ShapeDtypeStruct((L, Bc, N), F32),
        name="ada_mod", compiler_params=_cparams(2),
    )(c_all, w, b[:, None, :])


def _resid(xs_ref, y_ref, pmod_ref, D):
    x = xs_ref[...]
    if y_ref is not None:
        x = x + pmod_ref[:, 5 * D:6 * D] * y_ref[...]
    return x


def _ret_in_kernel(*refs, has_y, D, H):
    refs = list(refs)
    xs_ref = refs.pop(0)
    y_ref = refs.pop(0) if has_y else None
    pmod_ref = refs.pop(0) if has_y else None
    mod_ref, cos_ref, sin_ref, g_ref, w_ref = refs[:5]
    outs = refs[5:]
    if has_y:
        xo_ref, outs = outs[0], outs[1:]
    q_ref, k_ref, v_ref, gt_ref = outs
    x = _resid(xs_ref, y_ref, pmod_ref, D)
    if has_y:
        xo_ref[...] = x
    h = _rms(x, g_ref[...]) * (1.0 + mod_ref[:, D:2 * D]) + mod_ref[:, 0:D]
    hb = h.astype(BF16)
    dk = D // H
    half = dk // 2
    cos, sin = cos_ref[...], sin_ref[...]
    for idx, (o_ref, scale) in enumerate(((q_ref, float(dk) ** -0.5), (k_ref, 1.0))):
        t = _dot(hb, w_ref[:, idx * D:(idx + 1) * D])
        for hh in range(H):
            x1 = t[:, hh * dk:hh * dk + half]
            x2 = t[:, hh * dk + half:(hh + 1) * dk]
            o_ref[:, hh * dk:hh * dk + half] = ((x1 * cos - x2 * sin) * scale).astype(o_ref.dtype)
            o_ref[:, hh * dk + half:(hh + 1) * dk] = ((x1 * sin + x2 * cos) * scale).astype(o_ref.dtype)
    v_ref[...] = _dot(hb, w_ref[:, 2 * D:3 * D]).astype(v_ref.dtype)
    gt_ref[...] = _dot(hb, w_ref[:, 3 * D:4 * D])


def _ret_in(grp, xs, y, y_off, pmod, mod, cos, sin, norm_g, w_in, qkv_dtype):
    D = xs.shape[-1]
    has_y = y is not None
    tok = [(xs, y_off if has_y else 0)] + ([(y, y_off)] if has_y else [])
    mods = ([pmod] if has_y else []) + [mod]
    outs = ([(D, F32)] if has_y else []) + [(D, qkv_dtype)] * 3 + [(D, F32)]
    res = _tok_call("ret_in", functools.partial(_ret_in_kernel, has_y=has_y, D=D, H=RET_HEADS), grp,
                    tok, mods, [cos, sin], [norm_g, w_in], outs)
    if not has_y:
        res = [xs] + list(res)
    return res


def _gn_gate(o, g, gn):
    mu = jnp.mean(o, axis=-1, keepdims=True)
    c = o - mu
    var = jnp.mean(c * c, axis=-1, keepdims=True)
    return _silu(g) * (c * lax.rsqrt(var + EPS) * gn)


def _ret_chunk_kernel(x_ref, q_ref, k_ref, v_ref, gt_ref, mod_ref, dec_ref, qd_ref, kd_ref, gl_ref,
                      gn_ref, wo_ref, xo_ref, s_ref, gat_ref, *, D, H):
    @pl.when(pl.program_id(1) == 0)
    def _():
        s_ref[...] = jnp.zeros_like(s_ref)

    dk = D // H
    heads = [slice(h * dk, (h + 1) * dk) for h in range(H)]
    states = [s_ref[h] for h in range(H)]
    raw = [_dot_nt(q_ref[:, hs], k_ref[:, hs]) for hs in heads]
    cross = [_dot((q_ref[:, hs].astype(F32) * qd_ref[:, h:h + 1]).astype(BF16), states[h].astype(BF16))
             for h, hs in enumerate(heads)]
    for h, hs in enumerate(heads):
        kh, vh = k_ref[:, hs], v_ref[:, hs]
        inner = _dot((raw[h] * dec_ref[h]).astype(BF16), vh)
        kdec = (kh.astype(F32) * kd_ref[:, h:h + 1]).astype(BF16)
        s_ref[h] = gl_ref[:, h:h + 1] * states[h] + _dot_tn(kdec, vh)
        gat_ref[:, hs] = _gn_gate(inner + cross[h], gt_ref[:, hs], gn_ref[:, hs]).astype(BF16)
    y = _dot(gat_ref[...], wo_ref[...])
    xo_ref[...] = x_ref[...] + mod_ref[:, 2 * D:3 * D] * y


def _ret_chunk(B, S, x, q, k, v, gt, mod, tabs, gn_g, w_out):
    D = x.shape[-1]
    H = RET_HEADS
    L = RET_CHUNK
    nc = S // L
    dk = D // H
    dec, qd, kd, gl = tabs
    tok = lambda w: pl.BlockSpec((L, w), lambda b, c: (b * nc + c, 0))
    cst = lambda a: pl.BlockSpec(a.shape, lambda b, c: (0,) * a.ndim)
    return pl.pallas_call(
        functools.partial(_ret_chunk_kernel, D=D, H=H), grid=(B, nc),
        in_specs=[tok(D)] * 5 + [pl.BlockSpec((None, 1, mod.shape[-1]), lambda b, c: (b, 0, 0)),
                                 cst(dec), cst(qd), cst(kd), cst(gl), cst(gn_g), cst(w_out)],
        out_specs=[tok(D), pl.BlockSpec((None, H, dk, dk), lambda b, c: (b, 0, 0, 0))],
        out_shape=[jax.ShapeDtypeStruct(x.shape, F32), jax.ShapeDtypeStruct((B, H, dk, dk), F32)],
        scratch_shapes=[pltpu.VMEM((L, D), BF16)],
        name="ret_chunk", compiler_params=_cparams(2),
    )(x, q, k, v, gt, mod, dec, qd, kd, gl, gn_g, w_out)


def _ret_dec_kernel(st_ref, q_ref, k_ref, v_ref, gl_ref, so_ref, o_ref, *, D, H, nb):
    is_update = pl.program_id(0) == pl.num_programs(0) - 1

    @pl.when(jnp.logical_not(is_update))
    def _():
        so_ref[...] = st_ref[...]
        o_ref[...] = jnp.zeros(o_ref.shape, F32)

    @pl.when(is_update)
    def _():
        _ret_dec_update(st_ref, q_ref, k_ref, v_ref, gl_ref, so_ref, o_ref, D=D, H=H, nb=nb)


def _ret_dec_update(st_ref, q_ref, k_ref, v_ref, gl_ref, so_ref, o_ref, *, D, H, nb):
    dk = D // H
    for h in range(H):
        hs = slice(h * dk, (h + 1) * dk)
        gh = gl_ref[:, h:h + 1]
        qh, kh, vh = q_ref[:, hs], k_ref[:, hs], v_ref[:, hs]
        inner = jnp.sum(qh * kh, axis=-1, keepdims=True) * vh
        qT = (qh * gh).T
        kT = kh.T
        rows = []
        for j in range(nb):
            S = st_ref[j, h]
            so_ref[j, h] = gh * S + kT[:, j:j + 1] * vh[j:j + 1, :]
            rows.append(jnp.sum(qT[:, j:j + 1] * S, axis=0, keepdims=True))
        o_ref[:, hs] = inner + jnp.concatenate(rows, axis=0)


def _ret_dec(state, layer, q, k, v, gl, in_place):
    L, B, H, dk, dv = state.shape
    D = q.shape[-1]
    nb = DEC_NB
    phases = 1 if in_place else L
    which = lambda p: (layer + 1 + p) % L if not in_place else layer
    row = pl.BlockSpec((nb, D), lambda p, i: (i, 0))
    st_spec = pl.BlockSpec((None, nb, H, dk, dv), lambda p, i: (which(p), i, 0, 0, 0))
    new_state, o = pl.pallas_call(
        functools.partial(_ret_dec_kernel, D=D, H=H, nb=nb), grid=(phases, B // nb),
        in_specs=[st_spec, row, row, row, pl.BlockSpec(gl.shape, lambda p, i: (0, 0))],
        out_specs=[st_spec, pl.BlockSpec((None, nb, D), lambda p, i: (p, i, 0))],
        out_shape=[jax.ShapeDtypeStruct(state.shape, F32), jax.ShapeDtypeStruct((phases, B, D), F32)],
        input_output_aliases={0: 0} if in_place else {},
        name="ret_dec", compiler_params=_cparams(2),
    )(state, q, k, v, gl)
    return new_state, o[phases - 1]


def _ret_post_kernel(x_ref, o_ref, gt_ref, mod_ref, gn_ref, wo_ref, xo_ref, *, D, H):
    dk = D // H
    parts = []
    for h in range(H):
        hs = slice(h * dk, (h + 1) * dk)
        parts.append(_gn_gate(o_ref[:, hs], gt_ref[:, hs], gn_ref[:, hs]).astype(BF16))
    y = _dot(jnp.concatenate(parts, axis=-1), wo_ref[...])
    xo_ref[...] = x_ref[...] + mod_ref[:, 2 * D:3 * D] * y


def _route(logits_t, bias):
    scores = jax.nn.sigmoid(logits_t)
    sel = scores + bias
    n_e = EXPERTS_PER_GROUP
    row = lambda a, e: a[e:e + 1, :]
    gscore = []
    for g in range(N_GROUPS):
        best2 = None
        for a, b in _PAIRS:
            s = row(sel, n_e * g + a) + row(sel, n_e * g + b)
            best2 = s if best2 is None else jnp.maximum(best2, s)
        gscore.append(best2)
    best, bestv = jnp.zeros_like(gscore[0], dtype=jnp.int32), gscore[0]
    for g in range(1, N_GROUPS):
        upd = gscore[g] > bestv
        best = jnp.where(upd, g, best)
        bestv = jnp.where(upd, gscore[g], bestv)

    def pick(a, j):
        out = row(a, j)
        for g in range(1, N_GROUPS):
            out = jnp.where(best == g, row(a, n_e * g + j), out)
        return out

    v = [pick(sel, j) for j in range(n_e)]
    sc = [pick(scores, j) for j in range(n_e)]
    i1, v1 = jnp.zeros_like(best), v[0]
    for j in range(1, n_e):
        upd = v[j] > v1
        i1 = jnp.where(upd, j, i1)
        v1 = jnp.where(upd, v[j], v1)
    i2, v2 = None, None
    for j in range(n_e):
        ok = i1 != j
        if i2 is None:
            i2 = jnp.where(ok, j, n_e)
            v2 = jnp.where(ok, v[j], -jnp.inf)
        else:
            upd = ok & (v[j] > v2)
            i2 = jnp.where(upd, j, i2)
            v2 = jnp.where(upd, v[j], v2)
    s1 = sc[0]
    s2 = sc[0]
    for j in range(1, n_e):
        s1 = jnp.where(i1 == j, sc[j], s1)
        s2 = jnp.where(i2 == j, sc[j], s2)
    tot = s1 + s2
    w1, w2 = s1 / tot, s2 / tot
    lo = jnp.minimum(i1, i2)
    hi = jnp.maximum(i1, i2)
    base = jnp.where(lo == 0, 0, jnp.where(lo == 1, 3, 5))
    cls = best * len(_PAIRS) + base + hi - lo - 1
    first_is_lo = i1 < i2
    return cls, jnp.where(first_is_lo, w1, w2), jnp.where(first_is_lo, w2, w1)


def _ffn_input(xp_ref, xsm_ref, modp_ref, mods_ref, g_ref, D, n_p):
    is_s = pl.program_id(0) == n_p
    tm, ts = xp_ref.shape[0], xsm_ref.shape[0]
    pad_rows = lambda a: jnp.concatenate([a, jnp.zeros((tm - ts, a.shape[1]), a.dtype)], axis=0)
    x = jnp.where(is_s, pad_rows(xsm_ref[...]), xp_ref[...])
    mod = jnp.where(is_s, pad_rows(mods_ref[:, 3 * D:6 * D]), modp_ref[:, 3 * D:6 * D])
    h = _rms(x, g_ref[...]) * (1.0 + mod[:, D:2 * D]) + mod[:, 0:D]
    return x, mod, h


def _router_kernel(xp_ref, xsm_ref, modp_ref, mods_ref, g_ref, rwt_ref, rb_ref, cls_ref, gate_ref, *, D, n_p):
    _, _, h = _ffn_input(xp_ref, xsm_ref, modp_ref, mods_ref, g_ref, D, n_p)
    hb = h.astype(BF16)
    h_lo = (h - hb.astype(F32)).astype(BF16)
    rw = rwt_ref[...]
    rw_hi = rw.astype(BF16)
    rw_lo = (rw - rw_hi.astype(F32)).astype(BF16)
    logits_t = _dot_nt(rw_hi, hb) + (_dot_nt(rw_lo, hb) + _dot_nt(rw_hi, h_lo))
    cls, wa, wb = _route(logits_t, rb_ref[...])
    cls_ref[0] = cls
    rowi = lax.broadcasted_iota(jnp.int32, gate_ref.shape, 0)
    gate_ref[...] = jnp.where(rowi == 0, wa, jnp.where(rowi == 1, wb, 0.0))


def _moe_pre_kernel(pos_ref, xp_ref, xsm_ref, modp_ref, mods_ref, g_ref, gate_ref, wsg_ref, wsu_ref, wsd_ref,
                    hx_init, hx_hbm, xs_ref, hbuf, sem, *, D, n_p):
    del hx_init
    i = pl.program_id(0)
    slot = i % 2
    tm, ts = xp_ref.shape[0], xsm_ref.shape[0]

    def start_rows(n):
        for r in range(n):
            p = pos_ref[i * tm + r]
            pltpu.make_async_copy(hbuf.at[slot, pl.ds(r, 1)], hx_hbm.at[pl.ds(p, 1)], sem.at[slot]).start()

    def wait_rows(sl, n):
        pltpu.make_async_copy(hbuf.at[sl, pl.ds(0, n)], hx_hbm.at[pl.ds(0, n)], sem.at[sl]).wait()

    @pl.when(i >= 2)
    def _():
        wait_rows(slot, tm)
    x, mod, h = _ffn_input(xp_ref, xsm_ref, modp_ref, mods_ref, g_ref, D, n_p)
    gates = gate_ref[...]
    rowi = lax.broadcasted_iota(jnp.int32, (LANE, tm), 0)
    extra = jnp.where(rowi == 0, gates[0:1], jnp.where(rowi == 1, gates[1:2], 0.0)).T
    hbuf[slot, :, 0:D] = h
    hbuf[slot, :, D:D + LANE] = extra

    def shared_ffn():
        hb = h.astype(BF16)
        hid = _silu(_dot(hb, wsg_ref[...])) * _dot(hb, wsu_ref[...])
        xs_ref[...] = x + mod[:, 2 * D:3 * D] * _dot(hid.astype(BF16), wsd_ref[...])

    @pl.when(i < n_p)
    def _():
        start_rows(tm)
        shared_ffn()

    @pl.when(i == n_p)
    def _():
        start_rows(ts)
        shared_ffn()
        wait_rows(slot, ts)

        @pl.when(i >= 1)
        def _():
            wait_rows(1 - slot, tm)


def _router(gp, gs, x_p, x_s, mod_p, mod_s, norm_g, rw_t, rb):
    D = x_p.shape[-1]
    n_p, tm = gp.n_tiles, gp.tm
    assert gs.T <= tm and gp.T % gs.T == 0
    consts = (norm_g, rw_t, rb)
    cls, gates = pl.pallas_call(
        functools.partial(_router_kernel, D=D, n_p=n_p), grid=(n_p + 1,),
        in_specs=_ffn_input_specs(gp, x_s, mod_p, mod_s, lambda i: i) + [_const_spec(a) for a in consts],
        out_specs=[pl.BlockSpec((1, 1, tm), lambda i: (i, 0, 0)), pl.BlockSpec((None, 8, tm), lambda i: (i, 0, 0))],
        out_shape=[jax.ShapeDtypeStruct((n_p + 1, 1, tm), jnp.int32), jax.ShapeDtypeStruct((n_p + 1, 8, tm), F32)],
        name="router", compiler_params=_cparams(1),
    )(x_p, x_s, mod_p, mod_s, *consts)
    return cls.reshape(-1), gates


def _ffn_input_specs(gp, x_s, mod_p, mod_s, first):
    n_p, tm, tps = gp.n_tiles, gp.tm, gp.tiles_per_seq
    D = x_s.shape[-1]
    clamp = lambda i: jnp.minimum(i, n_p - 1)
    return [pl.BlockSpec((tm, D), lambda *a: (clamp(first(*a)), 0)),
            pl.BlockSpec(x_s.shape, lambda *a: (0, 0)),
            pl.BlockSpec((None, 1, mod_p.shape[-1]), lambda *a: (clamp(first(*a)) // tps, 0, 0)),
            pl.BlockSpec(mod_s.shape, lambda *a: (0, 0))]


def _moe_pre(gp, gs, x_p, x_s, mod_p, mod_s, norm_g, gates, pos_pad, n_rows, wsg, wsu, wsd):
    D = x_p.shape[-1]
    n_p, tm = gp.n_tiles, gp.tm
    T_all = gp.T + gs.T
    W = D + LANE
    take_i = lambda i, pos: i
    cst = lambda a: pl.BlockSpec(a.shape, lambda i, pos: (0,) * a.ndim)
    in_specs = _ffn_input_specs(gp, x_s, mod_p, mod_s, take_i)
    in_specs += [cst(norm_g), pl.BlockSpec((None, 8, tm), lambda i, pos: (i, 0, 0)), cst(wsg), cst(wsu), cst(wsd),
                 pl.BlockSpec(memory_space=pl.ANY)]
    grid_spec = pltpu.PrefetchScalarGridSpec(
        num_scalar_prefetch=1, grid=(n_p + 1,), in_specs=in_specs,
        out_specs=[pl.BlockSpec(memory_space=pl.ANY), pl.BlockSpec((tm, D), lambda i, pos: (i, 0))],
        scratch_shapes=[pltpu.VMEM((2, tm, W), F32), pltpu.SemaphoreType.DMA((2,))])
    hx, xs = pl.pallas_call(
        functools.partial(_moe_pre_kernel, D=D, n_p=n_p), grid_spec=grid_spec,
        out_shape=[jax.ShapeDtypeStruct((n_rows, W), F32), jax.ShapeDtypeStruct((T_all, D), F32)],
        input_output_aliases={10: 0}, name="moe_pre", compiler_params=_cparams(1),
    )(pos_pad, x_p, x_s, mod_p, mod_s, norm_g, gates, wsg, wsu, wsd, jnp.zeros((n_rows, W), F32))
    return hx, xs


def _moe_kernel(pos_ref, te1_ref, te2_ref, nv_ref, dump_ref, nused_ref, hx_ref, wg1, wu1, wd1, wg2, wu2, wd2,
                y_hbm, src_ref, obuf, ssem, *, T_all, n_rows, tm, D):
    i = pl.program_id(0)
    nused = nused_ref[0]
    slot = i % 2

    def scatter_row(tile, sl, r):
        nv = nv_ref[tile]
        t = jnp.where(r < nv, src_ref[tile * tm + r], dump_ref[tile] + (r - nv))
        pltpu.make_async_copy(obuf.at[sl, pl.ds(r, 1)], y_hbm.at[pl.ds(t, 1)], ssem.at[sl]).start(priority=1)

    def rolled_scatter(tile, sl):
        def step(r, c):
            scatter_row(tile, sl, r)
            return c
        lax.fori_loop(0, tm, step, 0, unroll=8)

    def wait_scatter(sl):
        pltpu.make_async_copy(obuf.at[sl], y_hbm.at[pl.ds(0, tm)], ssem.at[sl]).wait()

    def compute(sl):
        hx = hx_ref[...]
        hb = hx[:, 0:D].astype(BF16)
        wa = hx[:, D:D + 1]
        wb = hx[:, D + 1:D + 2]
        bf = lambda w_ref: w_ref[...].astype(BF16)
        hid_a = (_silu(_dot(hb, bf(wg1))) * _dot(hb, bf(wu1)) * wa).astype(BF16)
        hid_b = (_silu(_dot(hb, bf(wg2))) * _dot(hb, bf(wu2)) * wb).astype(BF16)
        obuf[sl] = _dot(hid_a, bf(wd1)) + _dot(hid_b, bf(wd2))

    @pl.when(i == 0)
    def _():
        def init(r, c):
            src_ref[r] = 0
            return c
        lax.fori_loop(0, n_rows, init, 0, unroll=8)

        def fill(t, c):
            src_ref[pos_ref[t]] = t
            return c
        lax.fori_loop(0, T_all, fill, 0, unroll=8)

    active = i < nused
    steady = jnp.logical_and(i >= 1, i + 1 < nused)

    @pl.when(jnp.logical_and(active, i >= 2))
    def _():
        wait_scatter(slot)

    @pl.when(steady)
    def _():
        compute(slot)
        for r in range(tm):
            scatter_row(i - 1, 1 - slot, r)

    @pl.when(jnp.logical_and(active, jnp.logical_not(steady)))
    def _():
        compute(slot)

        @pl.when(i >= 1)
        def _():
            rolled_scatter(i - 1, 1 - slot)

        @pl.when(i == nused - 1)
        def _():
            rolled_scatter(i, slot)
            wait_scatter(slot)

            @pl.when(i >= 1)
            def _():
                wait_scatter(1 - slot)

    @pl.when(jnp.logical_not(active))
    def _():
        obuf[slot] = jnp.zeros(obuf.shape[1:], F32)
        rolled_scatter(i, slot)
        wait_scatter(slot)


def _moe_plan(cls, T_all):
    tm = MOE_TILE
    n_tiles = (T_all + N_CLASSES * (tm - 1)) // tm
    n_rows = n_tiles * tm
    cls = cls[:T_all]
    onehot = (cls[:, None] == jnp.arange(N_CLASSES, dtype=jnp.int32)[None, :]).astype(jnp.int32)
    counts = jnp.sum(onehot, axis=0)
    rank = jnp.sum((jnp.cumsum(onehot, axis=0) - onehot) * onehot, axis=1)
    ntile_c = (counts + tm - 1) // tm
    tile_end = jnp.cumsum(ntile_c)
    tile_off = tile_end - ntile_c
    pos = (jnp.sum(onehot * tile_off[None, :], axis=1) * tm + rank).astype(jnp.int32)
    nused = tile_end[-1]
    tiles = jnp.arange(n_tiles, dtype=jnp.int32)
    tid = jnp.minimum(tiles, nused - 1)
    tile_cls = jnp.sum((tid[:, None] >= tile_end[None, :]).astype(jnp.int32), axis=1)
    nvalid = jnp.clip(counts[tile_cls] - (tiles - tile_off[tile_cls]) * tm, 0, tm)
    nvalid = jnp.where(tiles < nused, nvalid, 0).astype(jnp.int32)
    dump = (T_all + tiles * tm - (jnp.cumsum(nvalid) - nvalid)).astype(jnp.int32)
    pair_lo = jnp.array([a for a, _ in _PAIRS], jnp.int32)
    pair_hi = jnp.array([b for _, b in _PAIRS], jnp.int32)
    grp_id, pair_id = tile_cls // len(_PAIRS), tile_cls % len(_PAIRS)
    te1 = (grp_id * EXPERTS_PER_GROUP + pair_lo[pair_id]).astype(jnp.int32)
    te2 = (grp_id * EXPERTS_PER_GROUP + pair_hi[pair_id]).astype(jnp.int32)
    return n_rows, (pos, te1, te2, nvalid, dump, nused.reshape(1).astype(jnp.int32))


def _moe_routed(hx, plan, T_all, layer, wg, wu, wd):
    n_rows, W = hx.shape
    D, Fe = wg.shape[-2:]
    tm = MOE_TILE
    w1 = lambda shape: pl.BlockSpec((None, None) + shape, lambda i, pos, te1, *_: (layer, te1[i], 0, 0))
    w2 = lambda shape: pl.BlockSpec((None, None) + shape, lambda i, pos, te1, te2, *_: (layer, te2[i], 0, 0))
    grid_spec = pltpu.PrefetchScalarGridSpec(
        num_scalar_prefetch=6, grid=(n_rows // tm,),
        in_specs=[pl.BlockSpec((tm, W), lambda i, *_: (i, 0)),
                  w1((D, Fe)), w1((D, Fe)), w1((Fe, D)), w2((D, Fe)), w2((D, Fe)), w2((Fe, D))],
        out_specs=pl.BlockSpec(memory_space=pl.ANY),
        scratch_shapes=[pltpu.SMEM((n_rows,), jnp.int32), pltpu.VMEM((2, tm, D), F32),
                        pltpu.SemaphoreType.DMA((2,))])
    return pl.pallas_call(
        functools.partial(_moe_kernel, T_all=T_all, n_rows=n_rows, tm=tm, D=D),
        grid_spec=grid_spec, out_shape=jax.ShapeDtypeStruct((n_rows, D), F32),
        name="moe_routed", compiler_params=_cparams(1),
    )(*plan, hx, wg, wu, wd, wg, wu, wd)


def _kv_kernel(xs_ref, y_ref, pmod_ref, kvmod_ref, cos_ref, sin_ref, g_ref, wc_ref, wp_ref, wr_ref, lg_ref,
               xo_ref, ckv_ref, kpe_ref, kcat_ref, *maybe_vt_ref, D, C, R):
    x = _resid(xs_ref, y_ref, pmod_ref, D)
    xo_ref[...] = x
    hn = (_rms(x, g_ref[...]) * (1.0 + kvmod_ref[:, D:2 * D]) + kvmod_ref[:, 0:D]).astype(BF16)
    ckv = _rms(_dot(hn, wc_ref[...]), lg_ref[...])
    kpe = _dot(hn, wp_ref[...]) * cos_ref[...] + _dot(hn, wr_ref[...]) * sin_ref[...]
    ckv_ref[...] = ckv
    kpe_ref[...] = kpe[:, 0:R]
    kcat_ref[:, 0:C] = ckv.astype(kcat_ref.dtype)
    kcat_ref[:, C:C + LANE] = kpe.astype(kcat_ref.dtype)
    if maybe_vt_ref:
        maybe_vt_ref[0][...] = ckv.T.astype(maybe_vt_ref[0].dtype)


def _kv_stream(grp, xs, y, y_off, pmod, kvmod, cos, sin, consts, C, R, kcat_dtype, with_vt):
    D = xs.shape[-1]
    in_specs = [grp.tok_spec(D, y_off), grp.tok_spec(D, y_off), grp.mod_spec(pmod.shape[-1]),
                grp.mod_spec(kvmod.shape[-1]), grp.rope_spec(LANE), grp.rope_spec(LANE)]
    in_specs += [_const_spec(a) for a in consts]
    widths = [(D, F32), (C, F32), (R, F32), (C + LANE, kcat_dtype)]
    out_specs = [grp.tok_spec(w) for w, _ in widths]
    out_shape = [jax.ShapeDtypeStruct((grp.T, w), dt) for w, dt in widths]
    if with_vt:
        tps = grp.tiles_per_seq
        out_specs.append(pl.BlockSpec((None, C, grp.tm), lambda i: (i // tps, 0, i % tps)))
        out_shape.append(jax.ShapeDtypeStruct((grp.n_seq, C, grp.seq_len), BF16))
    return _tile_call("kv_stream", functools.partial(_kv_kernel, D=D, C=C, R=R), grp, in_specs,
                      [xs, y, pmod, kvmod, cos, sin, *consts], out_specs, out_shape)


def _mla_q_kernel(*refs, has_y, transposed, D, C, H, scale):
    refs = list(refs)
    xs_ref = refs.pop(0)
    y_ref = refs.pop(0) if has_y else None
    pmod_ref = refs.pop(0) if has_y else None
    mod_ref, cos_ref, sin_ref, g_ref, wdq_ref, qg_ref, wn_ref, wp_ref, wr_ref, wuk_ref = refs[:10]
    outs = refs[10:]
    if has_y:
        xo_ref, outs = outs[0], outs[1:]
    qcat_ref, = outs
    x = _resid(xs_ref, y_ref, pmod_ref, D)
    if has_y:
        xo_ref[...] = x
    h = (_rms(x, g_ref[...]) * (1.0 + mod_ref[:, D:2 * D]) + mod_ref[:, 0:D]).astype(BF16)
    cq = _rms(_dot(h, wdq_ref[...]), qg_ref[...]).astype(BF16)
    q_nope = _dot(cq, wn_ref[...]).astype(BF16)
    cos, sin = cos_ref[...], sin_ref[...]
    dn = q_nope.shape[-1] // H
    for hh in range(H):
        ls = slice(hh * LANE, (hh + 1) * LANE)
        q_lat = _dot(q_nope[:, hh * dn:(hh + 1) * dn], wuk_ref[hh])
        q_pe = _dot(cq, wp_ref[:, ls]) * cos + _dot(cq, wr_ref[:, ls]) * sin
        if transposed:
            tm = q_lat.shape[0]
            qcat_ref[0:C, hh * tm:(hh + 1) * tm] = (q_lat * scale).T.astype(qcat_ref.dtype)
            qcat_ref[C:C + LANE, hh * tm:(hh + 1) * tm] = (q_pe * scale).T.astype(qcat_ref.dtype)
        else:
            qcat_ref[hh, :, 0:C] = (q_lat * scale).astype(qcat_ref.dtype)
            qcat_ref[hh, :, C:C + LANE] = (q_pe * scale).astype(qcat_ref.dtype)


def _mla_q(grp, xs, y, y_off, pmod, mod, cos, sin, norm_g, wq, out_dtype, transposed):
    D = xs.shape[-1]
    w_dq, qg, w_n, w_p, w_r, w_uk = wq
    C = w_uk.shape[-1]
    H = MLA_HEADS
    has_y = y is not None
    scale = float(w_uk.shape[1] + QK_ROPE) ** -0.5
    if transposed:
        scale *= LOG2_E
    in_specs = [grp.tok_spec(D, y_off if has_y else 0)]
    in_specs += [grp.tok_spec(D, y_off), grp.mod_spec(pmod.shape[-1])] if has_y else []
    args = [xs] + ([y, pmod] if has_y else [])
    in_specs += [grp.mod_spec(mod.shape[-1]), grp.rope_spec(LANE), grp.rope_spec(LANE)]
    args += [mod, cos, sin]
    for a in (norm_g, w_dq, qg, w_n, w_p, w_r, w_uk):
        in_specs.append(_const_spec(a)); args.append(a)
    if transposed:
        q_spec = pl.BlockSpec((None, C + LANE, H * grp.tm), lambda i: (i, 0, 0))
        q_shape = jax.ShapeDtypeStruct((grp.n_tiles, C + LANE, H * grp.tm), out_dtype)
    else:
        q_spec = pl.BlockSpec((H, grp.tm, C + LANE), lambda i: (0, i, 0))
        q_shape = jax.ShapeDtypeStruct((H, grp.T, C + LANE), out_dtype)
    out_specs = ([grp.tok_spec(D)] if has_y else []) + [q_spec]
    out_shape = ([jax.ShapeDtypeStruct((grp.T, D), F32)] if has_y else []) + [q_shape]
    res = _tile_call(
        "mla_q", functools.partial(_mla_q_kernel, has_y=has_y, transposed=transposed, D=D, C=C, H=H, scale=scale),
        grp, in_specs, args, out_specs, out_shape)
    return (res[0], res[1]) if has_y else (xs, res[0])


def _mla_out(o_lat_heads, x, g1, wuvt_ref, wo_ref):
    parts = [_dot(o.astype(BF16), wuvt_ref[hh]).astype(BF16) for hh, o in enumerate(o_lat_heads)]
    return x + g1 * _dot(jnp.concatenate(parts, axis=-1), wo_ref[...])


def _attn_kernel(x_ref, qt_ref, k_ref, vt_ref, mod_ref, wuvt_ref, wo_ref, xo_ref, m_ref, l_ref, acc_ref, s0_ref,
                 *, D, C, H, tq, tk, cw):
    qi = pl.program_id(1)
    m_ref[...] = jnp.full(m_ref.shape, -jnp.inf, F32)
    l_ref[...] = jnp.zeros(l_ref.shape, F32)
    acc_ref[...] = jnp.zeros(acc_ref.shape, F32)
    n_col = H * tq

    def keys(j):
        return k_ref[pl.ds(pl.multiple_of(j * tk, tk), tk), :]

    s0_ref[...] = _dot(keys(0), qt_ref[:, 0:cw])

    def block(j, masked):
        kb = keys(j)
        vt = vt_ref[:, pl.ds(pl.multiple_of(j * tk, tk), tk)]
        if masked:
            kpos = lax.broadcasted_iota(jnp.int32, (tk, cw), 0)
            qpos = lax.broadcasted_iota(jnp.int32, (tk, cw), 1) & (tq - 1)
            keep = kpos <= qpos
        s_next = s0_ref[...]
        for c0 in range(0, n_col, cw):
            cs = slice(c0, c0 + cw)
            s = s_next
            if c0 + cw < n_col:
                s_next = _dot(kb, qt_ref[:, c0 + cw:c0 + 2 * cw])
            elif not masked:
                s0_ref[...] = _dot(keys(j + 1), qt_ref[:, 0:cw])
            if masked:
                s = jnp.where(keep, s, -1e30)
            m_old = m_ref[:, cs]
            m_new = jnp.maximum(m_old, jnp.max(s, axis=0, keepdims=True))
            alpha = jnp.exp2(m_old - m_new)
            p = jnp.exp2(s - m_new)
            l_ref[:, cs] = alpha * l_ref[:, cs] + jnp.sum(p, axis=0, keepdims=True)
            acc_ref[:, cs] = alpha * acc_ref[:, cs] + _dot(vt, p.astype(BF16))
            m_ref[:, cs] = m_new

    def body(j, c):
        block(j, False)
        return c
    lax.fori_loop(0, qi, body, 0)
    block(qi, True)
    parts = []
    for hh in range(H):
        cs = slice(hh * tq, (hh + 1) * tq)
        o_t = (acc_ref[:, cs] / l_ref[:, cs]).astype(BF16)
        parts.append(_dot_tn(o_t, wuvt_ref[hh]).astype(BF16))
    y = _dot(jnp.concatenate(parts, axis=-1), wo_ref[...])
    xo_ref[...] = x_ref[...] + mod_ref[:, 2 * D:3 * D] * y


def _attn_prompt(B, S, x, qt, kcat, vt, mod, w_uvt, w_o):
    D = x.shape[-1]
    H = MLA_HEADS
    _, Wq, n_col = qt.shape
    C = w_uvt.shape[1]
    tq, tk = ATT_TQ, ATT_TK
    assert tq == tk and tq & (tq - 1) == 0 and n_col == H * tq
    nq = S // tq
    cst = lambda a: pl.BlockSpec(a.shape, lambda b, i: (0,) * a.ndim)
    return pl.pallas_call(
        functools.partial(_attn_kernel, D=D, C=C, H=H, tq=tq, tk=tk, cw=ATT_CW), grid=(B, nq),
        in_specs=[pl.BlockSpec((tq, D), lambda b, i: (b * nq + i, 0)),
                  pl.BlockSpec((None, Wq, n_col), lambda b, i: (b * nq + i, 0, 0)),
                  pl.BlockSpec((S, Wq), lambda b, i: (b, 0)),
                  pl.BlockSpec((None, C, S), lambda b, i: (b, 0, 0)),
                  pl.BlockSpec((None, 1, mod.shape[-1]), lambda b, i: (b, 0, 0)), cst(w_uvt), cst(w_o)],
        out_specs=pl.BlockSpec((tq, D), lambda b, i: (b * nq + i, 0)),
        out_shape=jax.ShapeDtypeStruct(x.shape, F32),
        scratch_shapes=[pltpu.VMEM((1, n_col), F32), pltpu.VMEM((1, n_col), F32), pltpu.VMEM((C, n_col), F32),
                        pltpu.VMEM((tk, ATT_CW), F32)],
        name="attn_prompt", compiler_params=_cparams(2),
    )(x, qt, kcat, vt, mod, w_uvt, w_o)


def _attn_dec_kernel(pt_ref, q_ref, kn_ref, ckv_hbm, kpe_hbm, o_ref, cbuf, pbuf, kb_ref, s_ref, csem, psem,
                     *, C, R, n_pages, page, cp):
    b = pl.program_id(0)
    nb = pl.num_programs(0)
    slot = b % 2

    def start(seq, sl):
        def body(p, c):
            pg = pt_ref[seq * n_pages + p]
            pltpu.make_async_copy(ckv_hbm.at[pl.ds(pg, 1)], cbuf.at[sl, pl.ds(p, 1)], csem.at[sl]).start()
            pltpu.make_async_copy(kpe_hbm.at[pl.ds(pg, 1)], pbuf.at[sl, pl.ds(p, 1)], psem.at[sl]).start()
            return c
        lax.fori_loop(0, n_pages, body, 0, unroll=4)

    @pl.when(b == 0)
    def _():
        start(0, 0)

    @pl.when(b + 1 < nb)
    def _():
        start(b + 1, 1 - slot)

    pltpu.make_async_copy(ckv_hbm.at[pl.ds(0, n_pages)], cbuf.at[slot], csem.at[slot]).wait()
    pltpu.make_async_copy(kpe_hbm.at[pl.ds(0, n_pages)], pbuf.at[slot], psem.at[slot]).wait()

    q = q_ref[...]
    ql = q[:, 0:C].astype(BF16)
    qp = q[:, C:C + R].astype(BF16)
    ck = cp * page
    n_chunks = n_pages // cp
    for c in range(n_chunks):
        kc = cbuf[slot, c * cp:(c + 1) * cp].reshape(ck, C).astype(BF16)
        pct = jnp.concatenate([pbuf[slot, c * cp + i] for i in range(cp)], axis=-1).astype(BF16)
        kb_ref[c * ck:(c + 1) * ck, :] = kc
        s_ref[:, c * ck:(c + 1) * ck] = _dot_nt(ql, kc) + _dot(qp, pct)
    kn = kn_ref[...]
    s_new = jnp.sum(q * kn, axis=-1, keepdims=True)
    s = s_ref[...]
    m = jnp.maximum(jnp.max(s, axis=-1, keepdims=True), s_new)
    p = jnp.exp(s - m)
    p_new = jnp.exp(s_new - m)
    l = jnp.sum(p, axis=-1, keepdims=True) + p_new
    acc = p_new * kn[:, 0:C]
    pb = p.astype(BF16)
    for c in range(n_chunks):
        acc = acc + _dot(pb[:, c * ck:(c + 1) * ck], kb_ref[c * ck:(c + 1) * ck, :])
    o_ref[...] = acc / l


def _attn_dec(q_s, kn, cache_ckv, kpe_pages, page_table):
    B, H, Wq = q_s.shape
    _, page, C = cache_ckv.shape
    R = kpe_pages.shape[1]
    n_pages = page_table.shape[1]
    P = n_pages * page
    any_spec = pl.BlockSpec(memory_space=pl.ANY)
    grid_spec = pltpu.PrefetchScalarGridSpec(
        num_scalar_prefetch=1, grid=(B,),
        in_specs=[pl.BlockSpec((None, H, Wq), lambda b, pt: (b, 0, 0)),
                  pl.BlockSpec((None, 1, Wq), lambda b, pt: (b, 0, 0)), any_spec, any_spec],
        out_specs=pl.BlockSpec((None, H, C), lambda b, pt: (b, 0, 0)),
        scratch_shapes=[pltpu.VMEM((2, n_pages, page, C), F32), pltpu.VMEM((2, n_pages, R, page), F32),
                        pltpu.VMEM((P, C), BF16), pltpu.VMEM((H, P), F32),
                        pltpu.SemaphoreType.DMA((2,)), pltpu.SemaphoreType.DMA((2,))])
    return pl.pallas_call(
        functools.partial(_attn_dec_kernel, C=C, R=R, n_pages=n_pages, page=page, cp=DEC_CHUNK_PAGES),
        grid_spec=grid_spec, out_shape=jax.ShapeDtypeStruct((B, H, C), F32), name="attn_dec",
        compiler_params=_cparams(1),
    )(page_table.reshape(-1), q_s, kn, cache_ckv, kpe_pages)


def _mla_out_kernel(x_ref, o_ref, mod_ref, wuvt_ref, wo_ref, xo_ref, *, D, H):
    heads = [o_ref[hh] for hh in range(H)]
    xo_ref[...] = _mla_out(heads, x_ref[...], mod_ref[:, 2 * D:3 * D], wuvt_ref, wo_ref)


def _final_kernel(xs_ref, y_ref, pmod_ref, g_ref, o_ref, *, D):
    o_ref[...] = _rms(_resid(xs_ref, y_ref, pmod_ref, D), g_ref[...])


def _rope_table(pos, d, width):
    inv = ROPE_THETA ** (-jnp.arange(0, d, 2, dtype=F32) / d)
    ang = pos.astype(F32)[:, None] * inv[None, :]
    cos, sin = jnp.cos(ang), jnp.sin(ang)
    pad = jnp.zeros((pos.shape[0], width - d), F32)
    return jnp.concatenate([cos, cos, pad], axis=-1), jnp.concatenate([sin, sin, pad], axis=-1)


def _rot_half_cols(w):
    half = w.shape[-1] // 2
    return jnp.concatenate([-w[..., half:], w[..., :half]], axis=-1)


def _pad_lanes(w):
    pad = jnp.zeros(w.shape[:-1] + (LANE - w.shape[-1],), w.dtype)
    return jnp.concatenate([w, pad], axis=-1)


def _ret_tables(L):
    log_g = jnp.log1p(-jnp.exp2(-5.0 - jnp.arange(RET_HEADS, dtype=F32)))
    idx = jnp.arange(L, dtype=F32)
    diff = idx[:, None] - idx[None, :]
    dec = jnp.where(diff[None] >= 0, jnp.exp(jnp.maximum(diff, 0.0)[None] * log_g[:, None, None]), 0.0)
    qd = jnp.exp((idx + 1.0)[:, None] * log_g[None, :])
    kd = jnp.exp((L - 1.0 - idx)[:, None] * log_g[None, :])
    gl = jnp.exp(L * log_g)[None, :]
    return dec, qd, kd, gl


def kernel(x_prompt, x_sample, c_prompt, c_sample, state_ret, cache_ckv, cache_kpe, page_table, w_ada, b_ada, norm_mix_g, norm_ffn_g, ret_w_in, ret_gn_g, ret_w_out, kv_w_ada, kv_b_ada, kv_norm_g, mla_w_dkv, mla_kv_norm_g, mla_w_uk, mla_w_uv, mla_w_dq, mla_q_norm_g, mla_w_uq, mla_w_o, router_w, router_bias, moe_w_gate, moe_w_up, moe_w_down, shared_w_gate, shared_w_up, shared_w_down, final_norm_g):
    B, S, D = x_prompt.shape
    Bs, Ss, _ = x_sample.shape
    assert Ss == 1
    depth = w_ada.shape[0]
    n_a = ret_w_in.shape[0]
    H = MLA_HEADS
    C = mla_w_dkv.shape[1] - QK_ROPE
    R = QK_ROPE
    dn = mla_w_uk.shape[1]
    past_len = page_table.shape[1] * cache_ckv.shape[1]

    gp = _Group(B, S, TM_PROMPT, False)
    gs = _Group(Bs, 1, Bs, True)
    groups = (gp, gs)
    T_all = gp.T + gs.T
    ys_off = gp.T // gs.tm

    n_c = B + Bs
    n_c_pad = -(-n_c // 8) * 8
    c_all = jnp.concatenate([c_prompt, c_sample, jnp.zeros((n_c_pad - n_c, D), F32)], axis=0)
    mod_all = _ada_mod(c_all, w_ada, b_ada)
    kvmod_all = _ada_mod(c_all, kv_w_ada[None], kv_b_ada[None])[0]
    mods = [[g.mod_array(mod_all[l, lo:lo + g.n_seq]) for l in range(depth)]
            for g, lo in zip(groups, (0, B))]
    kvmods = [g.mod_array(kvmod_all[lo:lo + g.n_seq]) for g, lo in zip(groups, (0, B))]

    pos = (jnp.arange(S), past_len + jnp.arange(1))
    ret_rope = [_rope_table(p, D // RET_HEADS, D // RET_HEADS)[0:2] for p in pos]
    ret_rope = [(c[:, :c.shape[1] // 2], s[:, :s.shape[1] // 2]) for c, s in ret_rope]
    mla_rope = [_rope_table(p, R, LANE) for p in pos]
    tabs_p = _ret_tables(RET_CHUNK)
    gl_s = _ret_tables(1)[3]

    row = lambda g: g.reshape(1, -1)
    w_in = ret_w_in.astype(BF16)
    w_out = ret_w_out.astype(BF16)
    wsg, wsu, wsd = shared_w_gate.astype(BF16), shared_w_up.astype(BF16), shared_w_down.astype(BF16)
    rw_t = router_w.T
    rb = router_bias.reshape(-1, 1)
    w_dkv_c = mla_w_dkv[:, :C].astype(BF16)
    w_dkv_p = _pad_lanes(mla_w_dkv[:, C:]).astype(BF16)
    w_dkv_r = _pad_lanes(_rot_half_cols(mla_w_dkv[:, C:])).astype(BF16)
    w_uk = mla_w_uk.astype(BF16)
    w_uvt = jnp.swapaxes(mla_w_uv, 1, 2).astype(BF16)
    wq = []
    for j in range(depth - n_a):
        w3 = mla_w_uq[j].reshape(-1, H, dn + R)
        w_n = w3[:, :, :dn].reshape(-1, H * dn).astype(BF16)
        w_p = _pad_lanes(w3[:, :, dn:]).reshape(-1, H * LANE).astype(BF16)
        w_r = _pad_lanes(_rot_half_cols(w3[:, :, dn:])).reshape(-1, H * LANE).astype(BF16)
        wq.append((mla_w_dq[j].astype(BF16), row(mla_q_norm_g[j]), w_n, w_p, w_r, w_uk))
    w_o = mla_w_o.astype(BF16)

    kpe_pages = jnp.swapaxes(cache_kpe, 1, 2)

    xs = [x_prompt.reshape(gp.T, D), x_sample.reshape(gs.T, D)]
    y = None
    states_p, state_s = [], None
    ckv_out, kpe_out, kcat = [None, None], [None, None], [None, None]

    for l in range(depth):
        y_offs = (0, ys_off)
        xmid = [None, None]
        if l < n_a:
            for gi, g in enumerate(groups):
                pm = mods[gi][l - 1] if y is not None else None
                x, q, k, v, gt = _ret_in(g, xs[gi], y, y_offs[gi], pm, mods[gi][l],
                                         ret_rope[gi][0], ret_rope[gi][1],
                                         row(norm_mix_g[l]), w_in[l], BF16 if gi == 0 else F32)
                if gi == 0:
                    xmid[gi], st = _ret_chunk(B, S, x, q, k, v, gt, mods[gi][l], tabs_p, row(ret_gn_g[l]), w_out[l])
                else:
                    state_s, o = _ret_dec(state_ret if l == 0 else state_s, l, q, k, v, gl_s, in_place=l > 0)
                    xmid[gi], = _tok_call("ret_post", functools.partial(_ret_post_kernel, D=D, H=RET_HEADS), g,
                                          [(x, 0), (o, 0), (gt, 0)], [mods[gi][l]], [],
                                          [row(ret_gn_g[l]), w_out[l]], [(D, F32)])
                if gi == 0:
                    states_p.append(st)
        else:
            j = l - n_a
            for gi, g in enumerate(groups):
                xsg, yy, pm = xs[gi], y, mods[gi][l - 1]
                if l == n_a:
                    res = _kv_stream(g, xsg, y, y_offs[gi], pm, kvmods[gi],
                                     mla_rope[gi][0], mla_rope[gi][1],
                                     [row(kv_norm_g), w_dkv_c, w_dkv_p, w_dkv_r, row(mla_kv_norm_g)], C, R,
                                     BF16 if gi == 0 else F32, gi == 0)
                    xsg, ckv_out[gi], kpe_out[gi], kcat[gi] = res[:4]
                    if gi == 0:
                        vt_p = res[4]
                    yy, pm = None, None
                x, qcat = _mla_q(g, xsg, yy, y_offs[gi], pm, mods[gi][l],
                                 mla_rope[gi][0], mla_rope[gi][1],
                                 row(norm_mix_g[l]), wq[j], BF16 if gi == 0 else F32, gi == 0)
                if gi == 0:
                    xmid[gi] = _attn_prompt(B, S, x, qcat, kcat[gi], vt_p, mods[gi][l], w_uvt, w_o[j])
                else:
                    o_lat = _attn_dec(jnp.swapaxes(qcat, 0, 1), kcat[gi][:, None, :], cache_ckv, kpe_pages, page_table)
                    xmid[gi] = pl.pallas_call(
                        functools.partial(_mla_out_kernel, D=D, H=H), grid=(1,),
                        in_specs=[_const_spec(x), pl.BlockSpec((H, Bs, C), lambda i: (0, 0, 0)),
                                  _const_spec(mods[gi][l]), _const_spec(w_uvt), _const_spec(w_o[j])],
                        out_specs=_const_spec(x), out_shape=jax.ShapeDtypeStruct(x.shape, F32),
                        name="mla_out", compiler_params=_cparams(1),
                    )(x, jnp.swapaxes(o_lat, 0, 1), mods[gi][l], w_uvt, w_o[j])
        ffn_in = (gp, gs, xmid[0], xmid[1], mods[0][l], mods[1][l], row(norm_ffn_g[l]))
        cls, gates = _router(*ffn_in, rw_t, rb)
        n_rows, plan = _moe_plan(cls, T_all)
        pos_pad = jnp.concatenate([plan[0], jnp.zeros((cls.shape[0] - T_all,), jnp.int32)])
        hx, xs_all = _moe_pre(*ffn_in, gates, pos_pad, n_rows, wsg[l], wsu[l], wsd[l])
        xs = [xs_all, xs_all]
        y = _moe_routed(hx, plan, T_all, l, moe_w_gate, moe_w_up, moe_w_down)

    outs = []
    for gi, g in enumerate(groups):
        o, = _tok_call("final_norm", functools.partial(_final_kernel, D=D), g,
                       [(xs[gi], (0, ys_off)[gi]), (y, (0, ys_off)[gi])],
                       [mods[gi][depth - 1]], [], [row(final_norm_g)], [(D, F32)])
        outs.append(o)
    return (outs[0].reshape(B, S, D), outs[1].reshape(Bs, 1, D),
            jnp.stack(states_p), state_s,
            ckv_out[0].reshape(B, S, C), kpe_out[0].reshape(B, S, R),
            ckv_out[1].reshape(Bs, 1, C), kpe_out[1].reshape(Bs, 1, R))
```

```python
import functools

import jax
import jax.numpy as jnp
from jax import lax
from jax.experimental import pallas as pl
from jax.experimental.pallas import tpu as pltpu

F32 = jnp.float32
BF16 = jnp.bfloat16

RET_HEADS = 4
MLA_HEADS = 8
QK_ROPE = 64
N_GROUPS = 4
EXPERTS_PER_GROUP = 4
ROPE_THETA = 10000.0
EPS = 1e-6
LOG2_E = 1.4426950408889634

_PAIRS = ((0, 1), (0, 2), (0, 3), (1, 2), (1, 3), (2, 3))
N_CLASSES = N_GROUPS * len(_PAIRS)

LANE = 128
VMEM_LIMIT_BYTES = 56 * 2 ** 20
TM_PROMPT = 256
MOE_TILE = 256
RET_CHUNK = 256
ATT_TQ = 256
ATT_TK = 256
ATT_CW = 512
DEC_NB = 8
DEC_CHUNK_PAGES = 8


def _dot(a, b):
    return jnp.dot(a, b, preferred_element_type=F32)


def _dot_nt(a, b):
    return lax.dot_general(a, b, (((1,), (1,)), ((), ())), preferred_element_type=F32)


def _dot_tn(a, b):
    return lax.dot_general(a, b, (((0,), (0,)), ((), ())), preferred_element_type=F32)


def _silu(x):
    return x * jax.nn.sigmoid(x)


def _rms(x, g):
    return x * lax.rsqrt(jnp.mean(x * x, axis=-1, keepdims=True) + EPS) * g


def _cparams(n_axes=1):
    return pltpu.CompilerParams(dimension_semantics=("arbitrary",) * n_axes,
                                vmem_limit_bytes=VMEM_LIMIT_BYTES)


class _Group:
    def __init__(self, n_seq, seq_len, tm, per_token_mod):
        self.n_seq, self.seq_len, self.tm = n_seq, seq_len, tm
        self.T = n_seq * seq_len
        self.n_tiles = self.T // tm
        self.per_token_mod = per_token_mod
        self.tiles_per_seq = max(seq_len // tm, 1)

    def tok_spec(self, width, offset=0):
        return pl.BlockSpec((self.tm, width), lambda i: (i + offset, 0))

    def mod_spec(self, width):
        if self.per_token_mod:
            return pl.BlockSpec((self.tm, width), lambda i: (i, 0))
        tps = self.tiles_per_seq
        return pl.BlockSpec((None, 1, width), lambda i: (i // tps, 0, 0))

    def rope_spec(self, width):
        if self.per_token_mod:
            return pl.BlockSpec((1, width), lambda i: (0, 0))
        tps = self.tiles_per_seq
        return pl.BlockSpec((self.tm, width), lambda i: (i % tps, 0))

    def mod_array(self, m):
        return m if self.per_token_mod else m[:, None, :]


def _const_spec(a):
    nd = a.ndim
    return pl.BlockSpec(a.shape, lambda i: (0,) * nd)


def _tile_call(name, body, grp, in_specs, args, out_specs, out_shape):
    return pl.pallas_call(body, grid=(grp.n_tiles,), in_specs=in_specs, out_specs=out_specs,
                          out_shape=out_shape, name=name, compiler_params=_cparams(1))(*args)


def _tok_call(name, body, grp, tok_ins, mod_ins, rope_ins, const_ins, outs):
    in_specs, args = [], []
    for a, off in tok_ins:
        in_specs.append(grp.tok_spec(a.shape[-1], off)); args.append(a)
    for a in mod_ins:
        in_specs.append(grp.mod_spec(a.shape[-1])); args.append(a)
    for a in rope_ins:
        in_specs.append(grp.rope_spec(a.shape[-1])); args.append(a)
    for a in const_ins:
        in_specs.append(_const_spec(a)); args.append(a)
    out_specs = [grp.tok_spec(width) for width, _ in outs]
    out_shape = [jax.ShapeDtypeStruct((grp.T, width), dtype) for width, dtype in outs]
    return _tile_call(name, body, grp, in_specs, args, out_specs, out_shape)


def _mod_kernel(c_ref, w_ref, b_ref, o_ref):
    c = c_ref[...]
    o_ref[...] = _dot(_silu(c).astype(BF16), w_ref[...].astype(BF16)) + b_ref[...]


def _ada_mod(c_all, w, b):
    L, D, N = w.shape
    Bc = c_all.shape[0]
    tn = min(N, 2048)
    return pl.pallas_call(
        _mod_kernel, grid=(L, N // tn),
        in_specs=[pl.BlockSpec((Bc, D), lambda l, j: (0, 0)),
                  pl.BlockSpec((None, D, tn), lambda l, j: (l, 0, j)),
                  pl.BlockSpec((None, 1, tn), lambda l, j: (l, 0, j))],
        out_specs=pl.BlockSpec((None, Bc, tn), lambda l, j: (l, 0, j)),
        out_shape=jax.ShapeDtypeStruct((L, Bc, N), F32),
        name="ada_mod", compiler_params=_cparams(2),
    )(c_all, w, b[:, None, :])


def _resid(xs_ref, y_ref, pmod_ref, D):
    x = xs_ref[...]
    if y_ref is not None:
        x = x + pmod_ref[:, 5 * D:6 * D] * y_ref[...]
    return x


def _ret_in_kernel(*refs, has_y, D, H):
    refs = list(refs)
    xs_ref = refs.pop(0)
    y_ref = refs.pop(0) if has_y else None
    pmod_ref = refs.pop(0) if has_y else None
    mod_ref, cos_ref, sin_ref, g_ref, w_ref = refs[:5]
    outs = refs[5:]
    if has_y:
        xo_ref, outs = outs[0], outs[1:]
    q_ref, k_ref, v_ref, gt_ref = outs
    x = _resid(xs_ref, y_ref, pmod_ref, D)
    if has_y:
        xo_ref[...] = x
    h = _rms(x, g_ref[...]) * (1.0 + mod_ref[:, D:2 * D]) + mod_ref[:, 0:D]
    hb = h.astype(BF16)
    dk = D // H
    half = dk // 2
    cos, sin = cos_ref[...], sin_ref[...]
    for idx, (o_ref, scale) in enumerate(((q_ref, float(dk) ** -0.5), (k_ref, 1.0))):
        t = _dot(hb, w_ref[:, idx * D:(idx + 1) * D])
        for hh in range(H):
            x1 = t[:, hh * dk:hh * dk + half]
            x2 = t[:, hh * dk + half:(hh + 1) * dk]
            o_ref[:, hh * dk:hh * dk + half] = ((x1 * cos - x2 * sin) * scale).astype(o_ref.dtype)
            o_ref[:, hh * dk + half:(hh + 1) * dk] = ((x1 * sin + x2 * cos) * scale).astype(o_ref.dtype)
    v_ref[...] = _dot(hb, w_ref[:, 2 * D:3 * D]).astype(v_ref.dtype)
    gt_ref[...] = _dot(hb, w_ref[:, 3 * D:4 * D])


def _ret_in(grp, xs, y, y_off, pmod, mod, cos, sin, norm_g, w_in, qkv_dtype):
    D = xs.shape[-1]
    has_y = y is not None
    tok = [(xs, y_off if has_y else 0)] + ([(y, y_off)] if has_y else [])
    mods = ([pmod] if has_y else []) + [mod]
    outs = ([(D, F32)] if has_y else []) + [(D, qkv_dtype)] * 3 + [(D, F32)]
    res = _tok_call("ret_in", functools.partial(_ret_in_kernel, has_y=has_y, D=D, H=RET_HEADS), grp,
                    tok, mods, [cos, sin], [norm_g, w_in], outs)
    if not has_y:
        res = [xs] + list(res)
    return res


def _gn_gate(o, g, gn):
    mu = jnp.mean(o, axis=-1, keepdims=True)
    c = o - mu
    var = jnp.mean(c * c, axis=-1, keepdims=True)
    return _silu(g) * (c * lax.rsqrt(var + EPS) * gn)


def _ret_chunk_kernel(x_ref, q_ref, k_ref, v_ref, gt_ref, mod_ref, dec_ref, qd_ref, kd_ref, gl_ref,
                      gn_ref, wo_ref, xo_ref, s_ref, gat_ref, *, D, H):
    @pl.when(pl.program_id(1) == 0)
    def _():
        s_ref[...] = jnp.zeros_like(s_ref)

    dk = D // H
    heads = [slice(h * dk, (h + 1) * dk) for h in range(H)]
    states = [s_ref[h] for h in range(H)]
    raw = [_dot_nt(q_ref[:, hs], k_ref[:, hs]) for hs in heads]
    cross = [_dot((q_ref[:, hs].astype(F32) * qd_ref[:, h:h + 1]).astype(BF16), states[h].astype(BF16))
             for h, hs in enumerate(heads)]
    for h, hs in enumerate(heads):
        kh, vh = k_ref[:, hs], v_ref[:, hs]
        inner = _dot((raw[h] * dec_ref[h]).astype(BF16), vh)
        kdec = (kh.astype(F32) * kd_ref[:, h:h + 1]).astype(BF16)
        s_ref[h] = gl_ref[:, h:h + 1] * states[h] + _dot_tn(kdec, vh)
        gat_ref[:, hs] = _gn_gate(inner + cross[h], gt_ref[:, hs], gn_ref[:, hs]).astype(BF16)
    y = _dot(gat_ref[...], wo_ref[...])
    xo_ref[...] = x_ref[...] + mod_ref[:, 2 * D:3 * D] * y


def _ret_chunk(B, S, x, q, k, v, gt, mod, tabs, gn_g, w_out):
    D = x.shape[-1]
    H = RET_HEADS
    L = RET_CHUNK
    nc = S // L
    dk = D // H
    dec, qd, kd, gl = tabs
    tok = lambda w: pl.BlockSpec((L, w), lambda b, c: (b * nc + c, 0))
    cst = lambda a: pl.BlockSpec(a.shape, lambda b, c: (0,) * a.ndim)
    return pl.pallas_call(
        functools.partial(_ret_chunk_kernel, D=D, H=H), grid=(B, nc),
        in_specs=[tok(D)] * 5 + [pl.BlockSpec((None, 1, mod.shape[-1]), lambda b, c: (b, 0, 0)),
                                 cst(dec), cst(qd), cst(kd), cst(gl), cst(gn_g), cst(w_out)],
        out_specs=[tok(D), pl.BlockSpec((None, H, dk, dk), lambda b, c: (b, 0, 0, 0))],
        out_shape=[jax.ShapeDtypeStruct(x.shape, F32), jax.ShapeDtypeStruct((B, H, dk, dk), F32)],
        scratch_shapes=[pltpu.VMEM((L, D), BF16)],
        name="ret_chunk", compiler_params=_cparams(2),
    )(x, q, k, v, gt, mod, dec, qd, kd, gl, gn_g, w_out)


def _ret_dec_kernel(st_ref, q_ref, k_ref, v_ref, gl_ref, so_ref, o_ref, *, D, H, nb):
    is_update = pl.program_id(0) == pl.num_programs(0) - 1

    @pl.when(jnp.logical_not(is_update))
    def _():
        so_ref[...] = st_ref[...]
        o_ref[...] = jnp.zeros(o_ref.shape, F32)

    @pl.when(is_update)
    def _():
        _ret_dec_update(st_ref, q_ref, k_ref, v_ref, gl_ref, so_ref, o_ref, D=D, H=H, nb=nb)


def _ret_dec_update(st_ref, q_ref, k_ref, v_ref, gl_ref, so_ref, o_ref, *, D, H, nb):
    dk = D // H
    for h in range(H):
        hs = slice(h * dk, (h + 1) * dk)
        gh = gl_ref[:, h:h + 1]
        qh, kh, vh = q_ref[:, hs], k_ref[:, hs], v_ref[:, hs]
        inner = jnp.sum(qh * kh, axis=-1, keepdims=True) * vh
        qT = (qh * gh).T
        kT = kh.T
        rows = []
        for j in range(nb):
            S = st_ref[j, h]
            so_ref[j, h] = gh * S + kT[:, j:j + 1] * vh[j:j + 1, :]
            rows.append(jnp.sum(qT[:, j:j + 1] * S, axis=0, keepdims=True))
        o_ref[:, hs] = inner + jnp.concatenate(rows, axis=0)


def _ret_dec(state, layer, q, k, v, gl, in_place):
    L, B, H, dk, dv = state.shape
    D = q.shape[-1]
    nb = DEC_NB
    phases = 1 if in_place else L
    which = lambda p: (layer + 1 + p) % L if not in_place else layer
    row = pl.BlockSpec((nb, D), lambda p, i: (i, 0))
    st_spec = pl.BlockSpec((None, nb, H, dk, dv), lambda p, i: (which(p), i, 0, 0, 0))
    new_state, o = pl.pallas_call(
        functools.partial(_ret_dec_kernel, D=D, H=H, nb=nb), grid=(phases, B // nb),
        in_specs=[st_spec, row, row, row, pl.BlockSpec(gl.shape, lambda p, i: (0, 0))],
        out_specs=[st_spec, pl.BlockSpec((None, nb, D), lambda p, i: (p, i, 0))],
        out_shape=[jax.ShapeDtypeStruct(state.shape, F32), jax.ShapeDtypeStruct((phases, B, D), F32)],
        input_output_aliases={0: 0} if in_place else {},
        name="ret_dec", compiler_params=_cparams(2),
    )(state, q, k, v, gl)
    return new_state, o[phases - 1]


def _ret_post_kernel(x_ref, o_ref, gt_ref, mod_ref, gn_ref, wo_ref, xo_ref, *, D, H):
    dk = D // H
    parts = []
    for h in range(H):
        hs = slice(h * dk, (h + 1) * dk)
        parts.append(_gn_gate(o_ref[:, hs], gt_ref[:, hs], gn_ref[:, hs]).astype(BF16))
    y = _dot(jnp.concatenate(parts, axis=-1), wo_ref[...])
    xo_ref[...] = x_ref[...] + mod_ref[:, 2 * D:3 * D] * y


def _route(logits_t, bias):
    scores = jax.nn.sigmoid(logits_t)
    sel = scores + bias
    n_e = EXPERTS_PER_GROUP
    row = lambda a, e: a[e:e + 1, :]
    gscore = []
    for g in range(N_GROUPS):
        best2 = None
        for a, b in _PAIRS:
            s = row(sel, n_e * g + a) + row(sel, n_e * g + b)
            best2 = s if best2 is None else jnp.maximum(best2, s)
        gscore.append(best2)
    best, bestv = jnp.zeros_like(gscore[0], dtype=jnp.int32), gscore[0]
    for g in range(1, N_GROUPS):
        upd = gscore[g] > bestv
        best = jnp.where(upd, g, best)
        bestv = jnp.where(upd, gscore[g], bestv)

    def pick(a, j):
        out = row(a, j)
        for g in range(1, N_GROUPS):
            out = jnp.where(best == g, row(a, n_e * g + j), out)
        return out

    v = [pick(sel, j) for j in range(n_e)]
    sc = [pick(scores, j) for j in range(n_e)]
    i1, v1 = jnp.zeros_like(best), v[0]
    for j in range(1, n_e):
        upd = v[j] > v1
        i1 = jnp.where(upd, j, i1)
        v1 = jnp.where(upd, v[j], v1)
    i2, v2 = None, None
    for j in range(n_e):
        ok = i1 != j
        if i2 is None:
            i2 = jnp.where(ok, j, n_e)
            v2 = jnp.where(ok, v[j], -jnp.inf)
        else:
            upd = ok & (v[j] > v2)
            i2 = jnp.where(upd, j, i2)
            v2 = jnp.where(upd, v[j], v2)
    s1 = sc[0]
    s2 = sc[0]
    for j in range(1, n_e):
        s1 = jnp.where(i1 == j, sc[j], s1)
        s2 = jnp.where(i2 == j, sc[j], s2)
    tot = s1 + s2
    w1, w2 = s1 / tot, s2 / tot
    lo = jnp.minimum(i1, i2)
    hi = jnp.maximum(i1, i2)
    base = jnp.where(lo == 0, 0, jnp.where(lo == 1, 3, 5))
    cls = best * len(_PAIRS) + base + hi - lo - 1
    first_is_lo = i1 < i2
    return cls, jnp.where(first_is_lo, w1, w2), jnp.where(first_is_lo, w2, w1)


def _ffn_input(xp_ref, xsm_ref, modp_ref, mods_ref, g_ref, D, n_p):
    is_s = pl.program_id(0) == n_p
    tm, ts = xp_ref.shape[0], xsm_ref.shape[0]
    pad_rows = lambda a: jnp.concatenate([a, jnp.zeros((tm - ts, a.shape[1]), a.dtype)], axis=0)
    x = jnp.where(is_s, pad_rows(xsm_ref[...]), xp_ref[...])
    mod = jnp.where(is_s, pad_rows(mods_ref[:, 3 * D:6 * D]), modp_ref[:, 3 * D:6 * D])
    h = _rms(x, g_ref[...]) * (1.0 + mod[:, D:2 * D]) + mod[:, 0:D]
    return x, mod, h


def _router_kernel(xp_ref, xsm_ref, modp_ref, mods_ref, g_ref, rwt_ref, rb_ref, cls_ref, gate_ref, *, D, n_p):
    _, _, h = _ffn_input(xp_ref, xsm_ref, modp_ref, mods_ref, g_ref, D, n_p)
    hb = h.astype(BF16)
    h_lo = (h - hb.astype(F32)).astype(BF16)
    rw = rwt_ref[...]
    rw_hi = rw.astype(BF16)
    rw_lo = (rw - rw_hi.astype(F32)).astype(BF16)
    logits_t = _dot_nt(rw_hi, hb) + (_dot_nt(rw_lo, hb) + _dot_nt(rw_hi, h_lo))
    cls, wa, wb = _route(logits_t, rb_ref[...])
    cls_ref[0] = cls
    rowi = lax.broadcasted_iota(jnp.int32, gate_ref.shape, 0)
    gate_ref[...] = jnp.where(rowi == 0, wa, jnp.where(rowi == 1, wb, 0.0))


def _moe_pre_kernel(pos_ref, xp_ref, xsm_ref, modp_ref, mods_ref, g_ref, gate_ref, wsg_ref, wsu_ref, wsd_ref,
                    hx_init, hx_hbm, xs_ref, hbuf, sem, *, D, n_p):
    del hx_init
    i = pl.program_id(0)
    slot = i % 2
    tm, ts = xp_ref.shape[0], xsm_ref.shape[0]

    def start_rows(n):
        for r in range(n):
            p = pos_ref[i * tm + r]
            pltpu.make_async_copy(hbuf.at[slot, pl.ds(r, 1)], hx_hbm.at[pl.ds(p, 1)],
                                  sem.at[slot]).start(priority=r % 2)

    def wait_rows(sl, n):
        pltpu.make_async_copy(hbuf.at[sl, pl.ds(0, n)], hx_hbm.at[pl.ds(0, n)], sem.at[sl]).wait()

    @pl.when(i >= 2)
    def _():
        wait_rows(slot, tm)
    x, mod, h = _ffn_input(xp_ref, xsm_ref, modp_ref, mods_ref, g_ref, D, n_p)
    gates = gate_ref[...]
    rowi = lax.broadcasted_iota(jnp.int32, (LANE, tm), 0)
    extra = jnp.where(rowi == 0, gates[0:1], jnp.where(rowi == 1, gates[1:2], 0.0)).T
    hbuf[slot, :, 0:D] = h
    hbuf[slot, :, D:D + LANE] = extra

    def shared_ffn():
        hb = h.astype(BF16)
        hid = _silu(_dot(hb, wsg_ref[...])) * _dot(hb, wsu_ref[...])
        xs_ref[...] = x + mod[:, 2 * D:3 * D] * _dot(hid.astype(BF16), wsd_ref[...])

    @pl.when(i < n_p)
    def _():
        start_rows(tm)
        shared_ffn()

    @pl.when(i == n_p)
    def _():
        start_rows(ts)
        shared_ffn()
        wait_rows(slot, ts)

        @pl.when(i >= 1)
        def _():
            wait_rows(1 - slot, tm)


def _router(gp, gs, x_p, x_s, mod_p, mod_s, norm_g, rw_t, rb):
    D = x_p.shape[-1]
    n_p, tm = gp.n_tiles, gp.tm
    assert gs.T <= tm and gp.T % gs.T == 0
    consts = (norm_g, rw_t, rb)
    cls, gates = pl.pallas_call(
        functools.partial(_router_kernel, D=D, n_p=n_p), grid=(n_p + 1,),
        in_specs=_ffn_input_specs(gp, x_s, mod_p, mod_s, lambda i: i) + [_const_spec(a) for a in consts],
        out_specs=[pl.BlockSpec((1, 1, tm), lambda i: (i, 0, 0)), pl.BlockSpec((None, 8, tm), lambda i: (i, 0, 0))],
        out_shape=[jax.ShapeDtypeStruct((n_p + 1, 1, tm), jnp.int32), jax.ShapeDtypeStruct((n_p + 1, 8, tm), F32)],
        name="router", compiler_params=_cparams(1),
    )(x_p, x_s, mod_p, mod_s, *consts)
    return cls.reshape(-1), gates


def _ffn_input_specs(gp, x_s, mod_p, mod_s, first):
    n_p, tm, tps = gp.n_tiles, gp.tm, gp.tiles_per_seq
    D = x_s.shape[-1]
    clamp = lambda i: jnp.minimum(i, n_p - 1)
    return [pl.BlockSpec((tm, D), lambda *a: (clamp(first(*a)), 0)),
            pl.BlockSpec(x_s.shape, lambda *a: (0, 0)),
            pl.BlockSpec((None, 1, mod_p.shape[-1]), lambda *a: (clamp(first(*a)) // tps, 0, 0)),
            pl.BlockSpec(mod_s.shape, lambda *a: (0, 0))]


def _moe_pre(gp, gs, x_p, x_s, mod_p, mod_s, norm_g, gates, pos_pad, n_rows, wsg, wsu, wsd):
    D = x_p.shape[-1]
    n_p, tm = gp.n_tiles, gp.tm
    T_all = gp.T + gs.T
    W = D + LANE
    take_i = lambda i, pos: i
    cst = lambda a: pl.BlockSpec(a.shape, lambda i, pos: (0,) * a.ndim)
    in_specs = _ffn_input_specs(gp, x_s, mod_p, mod_s, take_i)
    in_specs += [cst(norm_g), pl.BlockSpec((None, 8, tm), lambda i, pos: (i, 0, 0)), cst(wsg), cst(wsu), cst(wsd),
                 pl.BlockSpec(memory_space=pl.ANY)]
    grid_spec = pltpu.PrefetchScalarGridSpec(
        num_scalar_prefetch=1, grid=(n_p + 1,), in_specs=in_specs,
        out_specs=[pl.BlockSpec(memory_space=pl.ANY), pl.BlockSpec((tm, D), lambda i, pos: (i, 0))],
        scratch_shapes=[pltpu.VMEM((2, tm, W), F32), pltpu.SemaphoreType.DMA((2,))])
    hx, xs = pl.pallas_call(
        functools.partial(_moe_pre_kernel, D=D, n_p=n_p), grid_spec=grid_spec,
        out_shape=[jax.ShapeDtypeStruct((n_rows, W), F32), jax.ShapeDtypeStruct((T_all, D), F32)],
        input_output_aliases={10: 0}, name="moe_pre", compiler_params=_cparams(1),
    )(pos_pad, x_p, x_s, mod_p, mod_s, norm_g, gates, wsg, wsu, wsd, jnp.zeros((n_rows, W), F32))
    return hx, xs


def _moe_kernel(pos_ref, te1_ref, te2_ref, nv_ref, dump_ref, nused_ref, hx_ref, wg1, wu1, wd1, wg2, wu2, wd2,
                y_hbm, src_ref, obuf, ssem, *, T_all, n_rows, tm, D):
    i = pl.program_id(0)
    nused = nused_ref[0]
    slot = i % 2

    def scatter_row(tile, sl, r):
        nv = nv_ref[tile]
        t = jnp.where(r < nv, src_ref[tile * tm + r], dump_ref[tile] + (r - nv))
        queue = r % 2 if isinstance(r, int) else 0
        pltpu.make_async_copy(obuf.at[sl, pl.ds(r, 1)], y_hbm.at[pl.ds(t, 1)], ssem.at[sl]).start(priority=queue)

    def rolled_scatter(tile, sl):
        def step(r, c):
            scatter_row(tile, sl, r)
            return c
        lax.fori_loop(0, tm, step, 0, unroll=8)

    def wait_scatter(sl):
        pltpu.make_async_copy(obuf.at[sl], y_hbm.at[pl.ds(0, tm)], ssem.at[sl]).wait()

    def compute(sl):
        hx = hx_ref[...]
        hb = hx[:, 0:D].astype(BF16)
        wa = hx[:, D:D + 1]
        wb = hx[:, D + 1:D + 2]
        bf = lambda w_ref: w_ref[...].astype(BF16)
        hid_a = (_silu(_dot(hb, bf(wg1))) * _dot(hb, bf(wu1)) * wa).astype(BF16)
        hid_b = (_silu(_dot(hb, bf(wg2))) * _dot(hb, bf(wu2)) * wb).astype(BF16)
        obuf[sl] = _dot(hid_a, bf(wd1)) + _dot(hid_b, bf(wd2))

    @pl.when(i == 0)
    def _():
        def init(r, c):
            src_ref[r] = 0
            return c
        lax.fori_loop(0, n_rows, init, 0, unroll=8)

        def fill(t, c):
            src_ref[pos_ref[t]] = t
            return c
        lax.fori_loop(0, T_all, fill, 0, unroll=8)

    active = i < nused
    steady = jnp.logical_and(i >= 1, i + 1 < nused)

    @pl.when(jnp.logical_and(active, i >= 2))
    def _():
        wait_scatter(slot)

    @pl.when(steady)
    def _():
        compute(slot)
        for r in range(tm):
            scatter_row(i - 1, 1 - slot, r)

    @pl.when(jnp.logical_and(active, jnp.logical_not(steady)))
    def _():
        compute(slot)

        @pl.when(i >= 1)
        def _():
            rolled_scatter(i - 1, 1 - slot)

        @pl.when(i == nused - 1)
        def _():
            rolled_scatter(i, slot)
            wait_scatter(slot)

            @pl.when(i >= 1)
            def _():
                wait_scatter(1 - slot)

    @pl.when(jnp.logical_not(active))
    def _():
        obuf[slot] = jnp.zeros(obuf.shape[1:], F32)
        rolled_scatter(i, slot)
        wait_scatter(slot)


def _moe_plan(cls, T_all):
    tm = MOE_TILE
    n_tiles = (T_all + N_CLASSES * (tm - 1)) // tm
    n_rows = n_tiles * tm
    cls = cls[:T_all]
    onehot = (cls[:, None] == jnp.arange(N_CLASSES, dtype=jnp.int32)[None, :]).astype(jnp.int32)
    counts = jnp.sum(onehot, axis=0)
    rank = jnp.sum((jnp.cumsum(onehot, axis=0) - onehot) * onehot, axis=1)
    ntile_c = (counts + tm - 1) // tm
    tile_end = jnp.cumsum(ntile_c)
    tile_off = tile_end - ntile_c
    pos = (jnp.sum(onehot * tile_off[None, :], axis=1) * tm + rank).astype(jnp.int32)
    nused = tile_end[-1]
    tiles = jnp.arange(n_tiles, dtype=jnp.int32)
    tid = jnp.minimum(tiles, nused - 1)
    tile_cls = jnp.sum((tid[:, None] >= tile_end[None, :]).astype(jnp.int32), axis=1)
    nvalid = jnp.clip(counts[tile_cls] - (tiles - tile_off[tile_cls]) * tm, 0, tm)
    nvalid = jnp.where(tiles < nused, nvalid, 0).astype(jnp.int32)
    dump = (T_all + tiles * tm - (jnp.cumsum(nvalid) - nvalid)).astype(jnp.int32)
    pair_lo = jnp.array([a for a, _ in _PAIRS], jnp.int32)
    pair_hi = jnp.array([b for _, b in _PAIRS], jnp.int32)
    grp_id, pair_id = tile_cls // len(_PAIRS), tile_cls % len(_PAIRS)
    te1 = (grp_id * EXPERTS_PER_GROUP + pair_lo[pair_id]).astype(jnp.int32)
    te2 = (grp_id * EXPERTS_PER_GROUP + pair_hi[pair_id]).astype(jnp.int32)
    return n_rows, (pos, te1, te2, nvalid, dump, nused.reshape(1).astype(jnp.int32))


def _moe_routed(hx, plan, T_all, layer, wg, wu, wd):
    n_rows, W = hx.shape
    D, Fe = wg.shape[-2:]
    tm = MOE_TILE
    w1 = lambda shape: pl.BlockSpec((None, None) + shape, lambda i, pos, te1, *_: (layer, te1[i], 0, 0))
    w2 = lambda shape: pl.BlockSpec((None, None) + shape, lambda i, pos, te1, te2, *_: (layer, te2[i], 0, 0))
    grid_spec = pltpu.PrefetchScalarGridSpec(
        num_scalar_prefetch=6, grid=(n_rows // tm,),
        in_specs=[pl.BlockSpec((tm, W), lambda i, *_: (i, 0)),
                  w1((D, Fe)), w1((D, Fe)), w1((Fe, D)), w2((D, Fe)), w2((D, Fe)), w2((Fe, D))],
        out_specs=pl.BlockSpec(memory_space=pl.ANY),
        scratch_shapes=[pltpu.SMEM((n_rows,), jnp.int32), pltpu.VMEM((2, tm, D), F32),
                        pltpu.SemaphoreType.DMA((2,))])
    return pl.pallas_call(
        functools.partial(_moe_kernel, T_all=T_all, n_rows=n_rows, tm=tm, D=D),
        grid_spec=grid_spec, out_shape=jax.ShapeDtypeStruct((n_rows, D), F32),
        name="moe_routed", compiler_params=_cparams(1),
    )(*plan, hx, wg, wu, wd, wg, wu, wd)


def _kv_kernel(xs_ref, y_ref, pmod_ref, kvmod_ref, cos_ref, sin_ref, g_ref, wc_ref, wp_ref, wr_ref, lg_ref,
               xo_ref, ckv_ref, kpe_ref, kcat_ref, *maybe_vt_ref, D, C, R):
    x = _resid(xs_ref, y_ref, pmod_ref, D)
    xo_ref[...] = x
    hn = (_rms(x, g_ref[...]) * (1.0 + kvmod_ref[:, D:2 * D]) + kvmod_ref[:, 0:D]).astype(BF16)
    ckv = _rms(_dot(hn, wc_ref[...]), lg_ref[...])
    kpe = _dot(hn, wp_ref[...]) * cos_ref[...] + _dot(hn, wr_ref[...]) * sin_ref[...]
    ckv_ref[...] = ckv
    kpe_ref[...] = kpe[:, 0:R]
    kcat_ref[:, 0:C] = ckv.astype(kcat_ref.dtype)
    kcat_ref[:, C:C + LANE] = kpe.astype(kcat_ref.dtype)
    if maybe_vt_ref:
        maybe_vt_ref[0][...] = ckv.T.astype(maybe_vt_ref[0].dtype)


def _kv_stream(grp, xs, y, y_off, pmod, kvmod, cos, sin, consts, C, R, kcat_dtype, with_vt):
    D = xs.shape[-1]
    in_specs = [grp.tok_spec(D, y_off), grp.tok_spec(D, y_off), grp.mod_spec(pmod.shape[-1]),
                grp.mod_spec(kvmod.shape[-1]), grp.rope_spec(LANE), grp.rope_spec(LANE)]
    in_specs += [_const_spec(a) for a in consts]
    widths = [(D, F32), (C, F32), (R, F32), (C + LANE, kcat_dtype)]
    out_specs = [grp.tok_spec(w) for w, _ in widths]
    out_shape = [jax.ShapeDtypeStruct((grp.T, w), dt) for w, dt in widths]
    if with_vt:
        tps = grp.tiles_per_seq
        out_specs.append(pl.BlockSpec((None, C, grp.tm), lambda i: (i // tps, 0, i % tps)))
        out_shape.append(jax.ShapeDtypeStruct((grp.n_seq, C, grp.seq_len), BF16))
    return _tile_call("kv_stream", functools.partial(_kv_kernel, D=D, C=C, R=R), grp, in_specs,
                      [xs, y, pmod, kvmod, cos, sin, *consts], out_specs, out_shape)


def _mla_q_kernel(*refs, has_y, transposed, D, C, H, scale):
    refs = list(refs)
    xs_ref = refs.pop(0)
    y_ref = refs.pop(0) if has_y else None
    pmod_ref = refs.pop(0) if has_y else None
    mod_ref, cos_ref, sin_ref, g_ref, wdq_ref, qg_ref, wn_ref, wp_ref, wr_ref, wuk_ref = refs[:10]
    outs = refs[10:]
    if has_y:
        xo_ref, outs = outs[0], outs[1:]
    qcat_ref, = outs
    x = _resid(xs_ref, y_ref, pmod_ref, D)
    if has_y:
        xo_ref[...] = x
    h = (_rms(x, g_ref[...]) * (1.0 + mod_ref[:, D:2 * D]) + mod_ref[:, 0:D]).astype(BF16)
    cq = _rms(_dot(h, wdq_ref[...]), qg_ref[...]).astype(BF16)
    q_nope = _dot(cq, wn_ref[...]).astype(BF16)
    cos, sin = cos_ref[...], sin_ref[...]
    dn = q_nope.shape[-1] // H
    for hh in range(H):
        ls = slice(hh * LANE, (hh + 1) * LANE)
        q_lat = _dot(q_nope[:, hh * dn:(hh + 1) * dn], wuk_ref[hh])
        q_pe = _dot(cq, wp_ref[:, ls]) * cos + _dot(cq, wr_ref[:, ls]) * sin
        if transposed:
            tm = q_lat.shape[0]
            qcat_ref[0:C, hh * tm:(hh + 1) * tm] = (q_lat * scale).T.astype(qcat_ref.dtype)
            qcat_ref[C:C + LANE, hh * tm:(hh + 1) * tm] = (q_pe * scale).T.astype(qcat_ref.dtype)
        else:
            qcat_ref[hh, :, 0:C] = (q_lat * scale).astype(qcat_ref.dtype)
            qcat_ref[hh, :, C:C + LANE] = (q_pe * scale).astype(qcat_ref.dtype)


def _mla_q(grp, xs, y, y_off, pmod, mod, cos, sin, norm_g, wq, out_dtype, transposed):
    D = xs.shape[-1]
    w_dq, qg, w_n, w_p, w_r, w_uk = wq
    C = w_uk.shape[-1]
    H = MLA_HEADS
    has_y = y is not None
    scale = float(w_uk.shape[1] + QK_ROPE) ** -0.5
    if transposed:
        scale *= LOG2_E
    in_specs = [grp.tok_spec(D, y_off if has_y else 0)]
    in_specs += [grp.tok_spec(D, y_off), grp.mod_spec(pmod.shape[-1])] if has_y else []
    args = [xs] + ([y, pmod] if has_y else [])
    in_specs += [grp.mod_spec(mod.shape[-1]), grp.rope_spec(LANE), grp.rope_spec(LANE)]
    args += [mod, cos, sin]
    for a in (norm_g, w_dq, qg, w_n, w_p, w_r, w_uk):
        in_specs.append(_const_spec(a)); args.append(a)
    if transposed:
        q_spec = pl.BlockSpec((None, C + LANE, H * grp.tm), lambda i: (i, 0, 0))
        q_shape = jax.ShapeDtypeStruct((grp.n_tiles, C + LANE, H * grp.tm), out_dtype)
    else:
        q_spec = pl.BlockSpec((H, grp.tm, C + LANE), lambda i: (0, i, 0))
        q_shape = jax.ShapeDtypeStruct((H, grp.T, C + LANE), out_dtype)
    out_specs = ([grp.tok_spec(D)] if has_y else []) + [q_spec]
    out_shape = ([jax.ShapeDtypeStruct((grp.T, D), F32)] if has_y else []) + [q_shape]
    res = _tile_call(
        "mla_q", functools.partial(_mla_q_kernel, has_y=has_y, transposed=transposed, D=D, C=C, H=H, scale=scale),
        grp, in_specs, args, out_specs, out_shape)
    return (res[0], res[1]) if has_y else (xs, res[0])


def _mla_out(o_lat_heads, x, g1, wuvt_ref, wo_ref):
    parts = [_dot(o.astype(BF16), wuvt_ref[hh]).astype(BF16) for hh, o in enumerate(o_lat_heads)]
    return x + g1 * _dot(jnp.concatenate(parts, axis=-1), wo_ref[...])


def _attn_kernel(x_ref, qt_ref, k_ref, vt_ref, mod_ref, wuvt_ref, wo_ref, xo_ref, m_ref, l_ref, acc_ref, s0_ref,
                 *, D, C, H, tq, tk, cw):
    qi = pl.program_id(1)
    m_ref[...] = jnp.full(m_ref.shape, -jnp.inf, F32)
    l_ref[...] = jnp.zeros(l_ref.shape, F32)
    acc_ref[...] = jnp.zeros(acc_ref.shape, F32)
    n_col = H * tq

    def keys(j):
        return k_ref[pl.ds(pl.multiple_of(j * tk, tk), tk), :]

    s0_ref[...] = _dot(keys(0), qt_ref[:, 0:cw])

    def block(j, masked):
        kb = keys(j)
        vt = vt_ref[:, pl.ds(pl.multiple_of(j * tk, tk), tk)]
        if masked:
            kpos = lax.broadcasted_iota(jnp.int32, (tk, cw), 0)
            qpos = lax.broadcasted_iota(jnp.int32, (tk, cw), 1) & (tq - 1)
            keep = kpos <= qpos
        s_next = s0_ref[...]
        for c0 in range(0, n_col, cw):
            cs = slice(c0, c0 + cw)
            s = s_next
            if c0 + cw < n_col:
                s_next = _dot(kb, qt_ref[:, c0 + cw:c0 + 2 * cw])
            elif not masked:
                s0_ref[...] = _dot(keys(j + 1), qt_ref[:, 0:cw])
            if masked:
                s = jnp.where(keep, s, -1e30)
            m_old = m_ref[:, cs]
            m_new = jnp.maximum(m_old, jnp.max(s, axis=0, keepdims=True))
            alpha = jnp.exp2(m_old - m_new)
            p = jnp.exp2(s - m_new)
            l_ref[:, cs] = alpha * l_ref[:, cs] + jnp.sum(p, axis=0, keepdims=True)
            acc_ref[:, cs] = alpha * acc_ref[:, cs] + _dot(vt, p.astype(BF16))
            m_ref[:, cs] = m_new

    def body(j, c):
        block(j, False)
        return c
    lax.fori_loop(0, qi, body, 0)
    block(qi, True)
    parts = []
    for hh in range(H):
        cs = slice(hh * tq, (hh + 1) * tq)
        o_t = (acc_ref[:, cs] / l_ref[:, cs]).astype(BF16)
        parts.append(_dot_tn(o_t, wuvt_ref[hh]).astype(BF16))
    y = _dot(jnp.concatenate(parts, axis=-1), wo_ref[...])
    xo_ref[...] = x_ref[...] + mod_ref[:, 2 * D:3 * D] * y


def _attn_prompt(B, S, x, qt, kcat, vt, mod, w_uvt, w_o):
    D = x.shape[-1]
    H = MLA_HEADS
    _, Wq, n_col = qt.shape
    C = w_uvt.shape[1]
    tq, tk = ATT_TQ, ATT_TK
    assert tq == tk and tq & (tq - 1) == 0 and n_col == H * tq
    nq = S // tq
    cst = lambda a: pl.BlockSpec(a.shape, lambda b, i: (0,) * a.ndim)
    return pl.pallas_call(
        functools.partial(_attn_kernel, D=D, C=C, H=H, tq=tq, tk=tk, cw=ATT_CW), grid=(B, nq),
        in_specs=[pl.BlockSpec((tq, D), lambda b, i: (b * nq + i, 0)),
                  pl.BlockSpec((None, Wq, n_col), lambda b, i: (b * nq + i, 0, 0)),
                  pl.BlockSpec((S, Wq), lambda b, i: (b, 0)),
                  pl.BlockSpec((None, C, S), lambda b, i: (b, 0, 0)),
                  pl.BlockSpec((None, 1, mod.shape[-1]), lambda b, i: (b, 0, 0)), cst(w_uvt), cst(w_o)],
        out_specs=pl.BlockSpec((tq, D), lambda b, i: (b * nq + i, 0)),
        out_shape=jax.ShapeDtypeStruct(x.shape, F32),
        scratch_shapes=[pltpu.VMEM((1, n_col), F32), pltpu.VMEM((1, n_col), F32), pltpu.VMEM((C, n_col), F32),
                        pltpu.VMEM((tk, ATT_CW), F32)],
        name="attn_prompt", compiler_params=_cparams(2),
    )(x, qt, kcat, vt, mod, w_uvt, w_o)


def _attn_dec_kernel(pt_ref, q_ref, kn_ref, ckv_hbm, kpe_hbm, o_ref, cbuf, pbuf, kb_ref, s_ref, csem, psem,
                     *, C, R, n_pages, page, cp):
    b = pl.program_id(0)
    nb = pl.num_programs(0)
    slot = b % 2

    def start(seq, sl):
        def body(p, c):
            pg = pt_ref[seq * n_pages + p]
            pltpu.make_async_copy(ckv_hbm.at[pl.ds(pg, 1)], cbuf.at[sl, pl.ds(p, 1)], csem.at[sl]).start()
            pltpu.make_async_copy(kpe_hbm.at[pl.ds(pg, 1)], pbuf.at[sl, pl.ds(p, 1)], psem.at[sl]).start()
            return c
        lax.fori_loop(0, n_pages, body, 0, unroll=4)

    @pl.when(b == 0)
    def _():
        start(0, 0)

    @pl.when(b + 1 < nb)
    def _():
        start(b + 1, 1 - slot)

    pltpu.make_async_copy(ckv_hbm.at[pl.ds(0, n_pages)], cbuf.at[slot], csem.at[slot]).wait()
    pltpu.make_async_copy(kpe_hbm.at[pl.ds(0, n_pages)], pbuf.at[slot], psem.at[slot]).wait()

    q = q_ref[...]
    ql = q[:, 0:C].astype(BF16)
    qp = q[:, C:C + R].astype(BF16)
    ck = cp * page
    n_chunks = n_pages // cp
    for c in range(n_chunks):
        kc = cbuf[slot, c * cp:(c + 1) * cp].reshape(ck, C).astype(BF16)
        pct = jnp.concatenate([pbuf[slot, c * cp + i] for i in range(cp)], axis=-1).astype(BF16)
        kb_ref[c * ck:(c + 1) * ck, :] = kc
        s_ref[:, c * ck:(c + 1) * ck] = _dot_nt(ql, kc) + _dot(qp, pct)
    kn = kn_ref[...]
    s_new = jnp.sum(q * kn, axis=-1, keepdims=True)
    s = s_ref[...]
    m = jnp.maximum(jnp.max(s, axis=-1, keepdims=True), s_new)
    p = jnp.exp(s - m)
    p_new = jnp.exp(s_new - m)
    l = jnp.sum(p, axis=-1, keepdims=True) + p_new
    acc = p_new * kn[:, 0:C]
    pb = p.astype(BF16)
    for c in range(n_chunks):
        acc = acc + _dot(pb[:, c * ck:(c + 1) * ck], kb_ref[c * ck:(c + 1) * ck, :])
    o_ref[...] = acc / l


def _attn_dec(q_s, kn, cache_ckv, kpe_pages, page_table):
    B, H, Wq = q_s.shape
    _, page, C = cache_ckv.shape
    R = kpe_pages.shape[1]
    n_pages = page_table.shape[1]
    P = n_pages * page
    any_spec = pl.BlockSpec(memory_space=pl.ANY)
    grid_spec = pltpu.PrefetchScalarGridSpec(
        num_scalar_prefetch=1, grid=(B,),
        in_specs=[pl.BlockSpec((None, H, Wq), lambda b, pt: (b, 0, 0)),
                  pl.BlockSpec((None, 1, Wq), lambda b, pt: (b, 0, 0)), any_spec, any_spec],
        out_specs=pl.BlockSpec((None, H, C), lambda b, pt: (b, 0, 0)),
        scratch_shapes=[pltpu.VMEM((2, n_pages, page, C), F32), pltpu.VMEM((2, n_pages, R, page), F32),
                        pltpu.VMEM((P, C), BF16), pltpu.VMEM((H, P), F32),
                        pltpu.SemaphoreType.DMA((2,)), pltpu.SemaphoreType.DMA((2,))])
    return pl.pallas_call(
        functools.partial(_attn_dec_kernel, C=C, R=R, n_pages=n_pages, page=page, cp=DEC_CHUNK_PAGES),
        grid_spec=grid_spec, out_shape=jax.ShapeDtypeStruct((B, H, C), F32), name="attn_dec",
        compiler_params=_cparams(1),
    )(page_table.reshape(-1), q_s, kn, cache_ckv, kpe_pages)


def _mla_out_kernel(x_ref, o_ref, mod_ref, wuvt_ref, wo_ref, xo_ref, *, D, H):
    heads = [o_ref[hh] for hh in range(H)]
    xo_ref[...] = _mla_out(heads, x_ref[...], mod_ref[:, 2 * D:3 * D], wuvt_ref, wo_ref)


def _final_kernel(xs_ref, y_ref, pmod_ref, g_ref, o_ref, *, D):
    o_ref[...] = _rms(_resid(xs_ref, y_ref, pmod_ref, D), g_ref[...])


def _rope_table(pos, d, width):
    inv = ROPE_THETA ** (-jnp.arange(0, d, 2, dtype=F32) / d)
    ang = pos.astype(F32)[:, None] * inv[None, :]
    cos, sin = jnp.cos(ang), jnp.sin(ang)
    pad = jnp.zeros((pos.shape[0], width - d), F32)
    return jnp.concatenate([cos, cos, pad], axis=-1), jnp.concatenate([sin, sin, pad], axis=-1)


def _rot_half_cols(w):
    half = w.shape[-1] // 2
    return jnp.concatenate([-w[..., half:], w[..., :half]], axis=-1)


def _pad_lanes(w):
    pad = jnp.zeros(w.shape[:-1] + (LANE - w.shape[-1],), w.dtype)
    return jnp.concatenate([w, pad], axis=-1)


def _ret_tables(L):
    log_g = jnp.log1p(-jnp.exp2(-5.0 - jnp.arange(RET_HEADS, dtype=F32)))
    idx = jnp.arange(L, dtype=F32)
    diff = idx[:, None] - idx[None, :]
    dec = jnp.where(diff[None] >= 0, jnp.exp(jnp.maximum(diff, 0.0)[None] * log_g[:, None, None]), 0.0)
    qd = jnp.exp((idx + 1.0)[:, None] * log_g[None, :])
    kd = jnp.exp((L - 1.0 - idx)[:, None] * log_g[None, :])
    gl = jnp.exp(L * log_g)[None, :]
    return dec, qd, kd, gl


def kernel(x_prompt, x_sample, c_prompt, c_sample, state_ret, cache_ckv, cache_kpe, page_table, w_ada, b_ada, norm_mix_g, norm_ffn_g, ret_w_in, ret_gn_g, ret_w_out, kv_w_ada, kv_b_ada, kv_norm_g, mla_w_dkv, mla_kv_norm_g, mla_w_uk, mla_w_uv, mla_w_dq, mla_q_norm_g, mla_w_uq, mla_w_o, router_w, router_bias, moe_w_gate, moe_w_up, moe_w_down, shared_w_gate, shared_w_up, shared_w_down, final_norm_g):
    B, S, D = x_prompt.shape
    Bs, Ss, _ = x_sample.shape
    assert Ss == 1
    depth = w_ada.shape[0]
    n_a = ret_w_in.shape[0]
    H = MLA_HEADS
    C = mla_w_dkv.shape[1] - QK_ROPE
    R = QK_ROPE
    dn = mla_w_uk.shape[1]
    past_len = page_table.shape[1] * cache_ckv.shape[1]

    gp = _Group(B, S, TM_PROMPT, False)
    gs = _Group(Bs, 1, Bs, True)
    groups = (gp, gs)
    T_all = gp.T + gs.T
    ys_off = gp.T // gs.tm

    n_c = B + Bs
    n_c_pad = -(-n_c // 8) * 8
    c_all = jnp.concatenate([c_prompt, c_sample, jnp.zeros((n_c_pad - n_c, D), F32)], axis=0)
    mod_all = _ada_mod(c_all, w_ada, b_ada)
    kvmod_all = _ada_mod(c_all, kv_w_ada[None], kv_b_ada[None])[0]
    mods = [[g.mod_array(mod_all[l, lo:lo + g.n_seq]) for l in range(depth)]
            for g, lo in zip(groups, (0, B))]
    kvmods = [g.mod_array(kvmod_all[lo:lo + g.n_seq]) for g, lo in zip(groups, (0, B))]

    pos = (jnp.arange(S), past_len + jnp.arange(1))
    ret_rope = [_rope_table(p, D // RET_HEADS, D // RET_HEADS)[0:2] for p in pos]
    ret_rope = [(c[:, :c.shape[1] // 2], s[:, :s.shape[1] // 2]) for c, s in ret_rope]
    mla_rope = [_rope_table(p, R, LANE) for p in pos]
    tabs_p = _ret_tables(RET_CHUNK)
    gl_s = _ret_tables(1)[3]

    row = lambda g: g.reshape(1, -1)
    w_in = ret_w_in.astype(BF16)
    w_out = ret_w_out.astype(BF16)
    wsg, wsu, wsd = shared_w_gate.astype(BF16), shared_w_up.astype(BF16), shared_w_down.astype(BF16)
    rw_t = router_w.T
    rb = router_bias.reshape(-1, 1)
    w_dkv_c = mla_w_dkv[:, :C].astype(BF16)
    w_dkv_p = _pad_lanes(mla_w_dkv[:, C:]).astype(BF16)
    w_dkv_r = _pad_lanes(_rot_half_cols(mla_w_dkv[:, C:])).astype(BF16)
    w_uk = mla_w_uk.astype(BF16)
    w_uvt = jnp.swapaxes(mla_w_uv, 1, 2).astype(BF16)
    wq = []
    for j in range(depth - n_a):
        w3 = mla_w_uq[j].reshape(-1, H, dn + R)
        w_n = w3[:, :, :dn].reshape(-1, H * dn).astype(BF16)
        w_p = _pad_lanes(w3[:, :, dn:]).reshape(-1, H * LANE).astype(BF16)
        w_r = _pad_lanes(_rot_half_cols(w3[:, :, dn:])).reshape(-1, H * LANE).astype(BF16)
        wq.append((mla_w_dq[j].astype(BF16), row(mla_q_norm_g[j]), w_n, w_p, w_r, w_uk))
    w_o = mla_w_o.astype(BF16)

    kpe_pages = jnp.swapaxes(cache_kpe, 1, 2)

    xs = [x_prompt.reshape(gp.T, D), x_sample.reshape(gs.T, D)]
    y = None
    states_p, state_s = [], None
    ckv_out, kpe_out, kcat = [None, None], [None, None], [None, None]

    for l in range(depth):
        y_offs = (0, ys_off)
        xmid = [None, None]
        if l < n_a:
            for gi, g in enumerate(groups):
                pm = mods[gi][l - 1] if y is not None else None
                x, q, k, v, gt = _ret_in(g, xs[gi], y, y_offs[gi], pm, mods[gi][l],
                                         ret_rope[gi][0], ret_rope[gi][1],
                                         row(norm_mix_g[l]), w_in[l], BF16 if gi == 0 else F32)
                if gi == 0:
                    xmid[gi], st = _ret_chunk(B, S, x, q, k, v, gt, mods[gi][l], tabs_p, row(ret_gn_g[l]), w_out[l])
                else:
                    state_s, o = _ret_dec(state_ret if l == 0 else state_s, l, q, k, v, gl_s, in_place=l > 0)
                    xmid[gi], = _tok_call("ret_post", functools.partial(_ret_post_kernel, D=D, H=RET_HEADS), g,
                                          [(x, 0), (o, 0), (gt, 0)], [mods[gi][l]], [],
                                          [row(ret_gn_g[l]), w_out[l]], [(D, F32)])
                if gi == 0:
                    states_p.append(st)
        else:
            j = l - n_a
            for gi, g in enumerate(groups):
                xsg, yy, pm = xs[gi], y, mods[gi][l - 1]
                if l == n_a:
                    res = _kv_stream(g, xsg, y, y_offs[gi], pm, kvmods[gi],
                                     mla_rope[gi][0], mla_rope[gi][1],
                                     [row(kv_norm_g), w_dkv_c, w_dkv_p, w_dkv_r, row(mla_kv_norm_g)], C, R,
                                     BF16 if gi == 0 else F32, gi == 0)
                    xsg, ckv_out[gi], kpe_out[gi], kcat[gi] = res[:4]
                    if gi == 0:
                        vt_p = res[4]
                    yy, pm = None, None
                x, qcat = _mla_q(g, xsg, yy, y_offs[gi], pm, mods[gi][l],
                                 mla_rope[gi][0], mla_rope[gi][1],
                                 row(norm_mix_g[l]), wq[j], BF16 if gi == 0 else F32, gi == 0)
                if gi == 0:
                    xmid[gi] = _attn_prompt(B, S, x, qcat, kcat[gi], vt_p, mods[gi][l], w_uvt, w_o[j])
                else:
                    o_lat = _attn_dec(jnp.swapaxes(qcat, 0, 1), kcat[gi][:, None, :], cache_ckv, kpe_pages, page_table)
                    xmid[gi] = pl.pallas_call(
                        functools.partial(_mla_out_kernel, D=D, H=H), grid=(1,),
                        in_specs=[_const_spec(x), pl.BlockSpec((H, Bs, C), lambda i: (0, 0, 0)),
                                  _const_spec(mods[gi][l]), _const_spec(w_uvt), _const_spec(w_o[j])],
                        out_specs=_const_spec(x), out_shape=jax.ShapeDtypeStruct(x.shape, F32),
                        name="mla_out", compiler_params=_cparams(1),
                    )(x, jnp.swapaxes(o_lat, 0, 1), mods[gi][l], w_uvt, w_o[j])
        ffn_in = (gp, gs, xmid[0], xmid[1], mods[0][l], mods[1][l], row(norm_ffn_g[l]))
        cls, gates = _router(*ffn_in, rw_t, rb)
        n_rows, plan = _moe_plan(cls, T_all)
        pos_pad = jnp.concatenate([plan[0], jnp.zeros((cls.shape[0] - T_all,), jnp.int32)])
        hx, xs_all = _moe_pre(*ffn_in, gates, pos_pad, n_rows, wsg[l], wsu[l], wsd[l])
        xs = [xs_all, xs_all]
        y = _moe_routed(hx, plan, T_all, l, moe_w_gate, moe_w_up, moe_w_down)

    outs = []
    for gi, g in enumerate(groups):
        o, = _tok_call("final_norm", functools.partial(_final_kernel, D=D), g,
                       [(xs[gi], (0, ys_off)[gi]), (y, (0, ys_off)[gi])],
                       [mods[gi][depth - 1]], [], [row(final_norm_g)], [(D, F32)])
        outs.append(o)
    return (outs[0].reshape(B, S, D), outs[1].reshape(Bs, 1, D),
            jnp.stack(states_p), state_s,
            ckv_out[0].reshape(B, S, C), kpe_out[0].reshape(B, S, R),
            ckv_out[1].reshape(Bs, 1, C), kpe_out[1].reshape(Bs, 1, R))
```
